```python
import jax, jax.numpy as jnp
from jax import lax
import numpy as np

D_MODEL = 4096
BATCH = 4
SEQ = 4096
DEPTH = 2
DEC_BATCH = 8
DEC_SEQ = 64
PAST_LEN = 1024

CHUNK = 64
N_META = 16
N_A = DEPTH // 2
N_B = DEPTH - N_A
H_A = 16
DK_A = D_MODEL // H_A
DV_A = 2 * DK_A
QK_W_A = H_A * DK_A
V_W_A = H_A * DV_A
H_B = 32
HD_B = D_MODEL // H_B
D_FF = ((8 * D_MODEL // 3 + 127) // 128) * 128
CONV_W = 3
Q_BLOCK = 128
ROPE_BASE = 10000.0
EPS = 1e-6
FORGET_BIAS = 2.0
NEG_INF = -1e30

kernel_name = 'yoco_retention_fox_streaming_step'


def rmsnorm(x, g=None):
    xf = x.astype(jnp.float32)
    y = xf * lax.rsqrt(jnp.mean(xf * xf, axis=-1, keepdims=True) + EPS)
    if g is not None:
        y = y * g.astype(jnp.float32)
    return y.astype(x.dtype)


def rope(x, pos):
    half = x.shape[-1] // 2
    inv = ROPE_BASE ** (-jnp.arange(half, dtype=jnp.float32) / half)
    ang = pos.astype(jnp.float32)[:, None] * inv[None, :]
    cos = jnp.cos(ang)[None, :, None, :].astype(x.dtype)
    sin = jnp.sin(ang)[None, :, None, :].astype(x.dtype)
    x1, x2 = x[..., :half], x[..., half:]
    return jnp.concatenate([x1 * cos - x2 * sin, x1 * sin + x2 * cos], axis=-1)


def retention_log_decay():
    return jnp.log1p(-jnp.exp2(-5.0 - jnp.arange(H_A, dtype=jnp.float32)))


def retention_block(q, k, v, S, log_g):
    T = q.shape[1]
    idx = jnp.arange(T, dtype=jnp.float32)
    rel = idx[:, None] - idx[None, :]
    causal = rel >= 0
    decay = jnp.where(causal[None], jnp.exp(jnp.where(causal, rel, 0.0)[None] * log_g[:, None, None]), 0.0)
    scores = jnp.einsum('bthd,bshd->bhts', q, k) * decay.astype(q.dtype)[None]
    o = jnp.einsum('bhts,bshe->bthe', scores, v)
    q_dec = jnp.exp((idx + 1.0)[:, None] * log_g[None, :]).astype(q.dtype)
    o = o + jnp.einsum('bthd,bhde->bthe', q * q_dec[None, :, :, None], S.astype(q.dtype))
    k_dec = jnp.exp((T - 1.0 - idx)[:, None] * log_g[None, :]).astype(k.dtype)
    S_new = (jnp.exp(T * log_g).astype(S.dtype)[None, :, None, None] * S
             + jnp.einsum('bthd,bthe->bhde', k * k_dec[None, :, :, None], v).astype(S.dtype))
    return o, S_new


def retention_mixer(h, pos, w_in, w_out, S0, is_prompt):
    B, T, _ = h.shape
    proj = h @ w_in
    q = rope(proj[..., :QK_W_A].reshape(B, T, H_A, DK_A), pos)
    k = rope(proj[..., QK_W_A:2 * QK_W_A].reshape(B, T, H_A, DK_A), pos) * (DK_A ** -0.5)
    v = proj[..., 2 * QK_W_A:2 * QK_W_A + V_W_A].reshape(B, T, H_A, DV_A)
    gate = proj[..., 2 * QK_W_A + V_W_A:]
    log_g = retention_log_decay()
    if is_prompt:
        pad = (-T) % CHUNK
        nc = (T + pad) // CHUNK

        def to_blocks(a):
            a = jnp.pad(a, ((0, 0), (pad, 0), (0, 0), (0, 0)))
            return jnp.moveaxis(a.reshape(B, nc, CHUNK, H_A, a.shape[-1]), 1, 0)

        def step(S, qkv):
            o_blk, S_next = retention_block(qkv[0], qkv[1], qkv[2], S, log_g)
            return S_next, o_blk

        S, o = lax.scan(step, S0, (to_blocks(q), to_blocks(k), to_blocks(v)))
        o = jnp.moveaxis(o, 0, 1).reshape(B, nc * CHUNK, H_A, DV_A)[:, pad:]
    else:
        o, S = retention_block(q, k, v, S0, log_g)
    o = rmsnorm(o).reshape(B, T, V_W_A)
    y = (jax.nn.silu(gate) * o) @ w_out
    return y, S


def conv_ffn(x, g, w_gu, cw, cb, w_down, buf):
    T = x.shape[1]
    gu = rmsnorm(x, g) @ w_gu
    a, u = gu[..., :D_FF], gu[..., D_FF:]
    ext = jnp.concatenate([buf.astype(a.dtype), a], axis=1)
    c = cb + cw[0] * ext[:, 0:T]
    for j in range(1, CONV_W):
        c = c + cw[j] * ext[:, j:j + T]
    y = (jax.nn.silu(c) * u) @ w_down
    return y, ext[:, T:]


def kv_side(x, g_kv, w_kvf, b_f, g_k):
    B, T, _ = x.shape
    proj = rmsnorm(x, g_kv) @ w_kvf
    k = rmsnorm(proj[..., :D_MODEL].reshape(B, T, H_B, HD_B), g_k)
    v = proj[..., D_MODEL:2 * D_MODEL].reshape(B, T, H_B, HD_B)
    logf = jax.nn.log_sigmoid((proj[..., 2 * D_MODEL:] + b_f).astype(jnp.float32)).astype(x.dtype)
    return k, v, logf


def fox_attend(q, k, v, c_q, c_k, q_pos, k_pos):
    s = jnp.einsum('bqhd,bkhd->bhqk', q, k).astype(jnp.float32)
    s = s + jnp.swapaxes(c_q, 1, 2)[..., :, None] - jnp.swapaxes(c_k, 1, 2)[..., None, :]
    s = jnp.where((k_pos[None, :] <= q_pos[:, None])[None, None], s, NEG_INF)
    p = jax.nn.softmax(s, axis=-1).astype(v.dtype)
    return jnp.einsum('bhqk,bkhd->bqhd', p, v)


def fox_prompt(q, k, v, logf):
    B, T = q.shape[0], q.shape[1]
    c = jnp.cumsum(logf.astype(jnp.float32), axis=1)
    pad = (-T) % Q_BLOCK
    padt = lambda a: jnp.pad(a, ((0, 0), (0, pad)) + ((0, 0),) * (a.ndim - 2))
    qp, kp, vp, cp = padt(q), padt(k), padt(v), padt(c)
    Lp = T + pad
    k_pos = jnp.arange(Lp)

    def one_block(i):
        start = i * Q_BLOCK
        qb = lax.dynamic_slice_in_dim(qp, start, Q_BLOCK, axis=1)
        cqb = lax.dynamic_slice_in_dim(cp, start, Q_BLOCK, axis=1)
        return fox_attend(qb, kp, vp, cqb, cp, start + jnp.arange(Q_BLOCK), k_pos)

    o = lax.map(one_block, jnp.arange(Lp // Q_BLOCK))
    return jnp.moveaxis(o, 0, 1).reshape(B, Lp, H_B, HD_B)[:, :T]


def fox_sample(q, k_new, v_new, logf_new, cache_k, cache_v, cache_logf):
    P, T = cache_k.shape[1], q.shape[1]
    k = jnp.concatenate([cache_k.astype(k_new.dtype), k_new], axis=1)
    v = jnp.concatenate([cache_v.astype(v_new.dtype), v_new], axis=1)
    c = jnp.cumsum(jnp.concatenate([cache_logf.astype(jnp.float32), logf_new.astype(jnp.float32)], axis=1), axis=1)
    return fox_attend(q, k, v, c[:, P:], c, P + jnp.arange(T), jnp.arange(P + T))


def fox_mixer(h, w_q, g_q, w_o, k, v, logf, kv_cache, is_prompt):
    B, T, _ = h.shape
    q = rmsnorm((h @ w_q).reshape(B, T, H_B, HD_B), g_q) * (HD_B ** -0.5)
    if is_prompt:
        o = fox_prompt(q, k, v, logf)
    else:
        o = fox_sample(q, k, v, logf, kv_cache[0], kv_cache[1], kv_cache[2])
    return o.reshape(B, T, D_MODEL) @ w_o


def trunk(x, pos, ret_states, conv_bufs, kv_cache, is_prompt,
          g_attn, g_ffn, w_ret_in, w_ret_out, g_kv, w_kvf, b_f, g_k,
          w_q_b, g_q, w_o_b, w_gu, conv_w, conv_b, w_down):
    new_ret, new_conv = [], []
    k = v = logf = None
    for l in range(DEPTH):
        h = rmsnorm(x, g_attn[l])
        if l < N_A:
            y, S = retention_mixer(h, pos, w_ret_in[l], w_ret_out[l], ret_states[l], is_prompt)
            new_ret.append(S)
        else:
            if l == N_A:
                k, v, logf = kv_side(x, g_kv, w_kvf, b_f, g_k)
            j = l - N_A
            y = fox_mixer(h, w_q_b[j], g_q[j], w_o_b[j], k, v, logf, kv_cache, is_prompt)
        x = x + y
        y, buf = conv_ffn(x, g_ffn[l], w_gu[l], conv_w[l], conv_b[l], w_down[l], conv_bufs[l])
        new_conv.append(buf)
        x = x + y
    return x, k, v, logf, jnp.stack(new_ret), jnp.stack(new_conv)


def setup_inputs(seed: int = 0) -> dict:
    key = jax.random.key(seed)
    ks = jax.random.split(key, 24)
    f32 = jnp.float32
    nrm = lambda kk, shape, scale: jax.random.normal(kk, shape, f32) * scale
    return {
        'x_prompt': nrm(ks[0], (BATCH, SEQ, D_MODEL), 1.0),
        'x_sample': nrm(ks[1], (DEC_BATCH, DEC_SEQ, D_MODEL), 1.0),
        'cache_k': nrm(ks[2], (DEC_BATCH, PAST_LEN, H_B, HD_B), 1.0),
        'cache_v': nrm(ks[3], (DEC_BATCH, PAST_LEN, H_B, HD_B), 1.0),
        'cache_logf': jax.nn.log_sigmoid(FORGET_BIAS + nrm(ks[4], (DEC_BATCH, PAST_LEN, H_B), 1.0)),
        'state_ret': nrm(ks[5], (N_A, DEC_BATCH, H_A, DK_A, DV_A), 0.05),
        'state_conv': nrm(ks[6], (DEPTH, DEC_BATCH, CONV_W - 1, D_FF), 1.0),
        'meta': nrm(ks[7], (N_META, D_MODEL), 1.0),
        'g_attn': 1.0 + nrm(ks[8], (DEPTH, D_MODEL), 0.02),
        'g_ffn': 1.0 + nrm(ks[9], (DEPTH, D_MODEL), 0.02),
        'w_ret_in': nrm(ks[10], (N_A, D_MODEL, 2 * QK_W_A + 2 * V_W_A), D_MODEL ** -0.5),
        'w_ret_out': nrm(ks[11], (N_A, V_W_A, D_MODEL), V_W_A ** -0.5),
        'g_kv': 1.0 + nrm(ks[12], (D_MODEL,), 0.02),
        'w_kvf': nrm(ks[13], (D_MODEL, 2 * D_MODEL + H_B), D_MODEL ** -0.5),
        'b_f': FORGET_BIAS + nrm(ks[14], (H_B,), 0.1),
        'g_k': 1.0 + nrm(ks[15], (HD_B,), 0.02),
        'w_q_b': nrm(ks[16], (N_B, D_MODEL, D_MODEL), D_MODEL ** -0.5),
        'g_q': 1.0 + nrm(ks[17], (N_B, HD_B), 0.02),
        'w_o_b': nrm(ks[18], (N_B, D_MODEL, D_MODEL), D_MODEL ** -0.5),
        'w_gu': nrm(ks[19], (DEPTH, D_MODEL, 2 * D_FF), D_MODEL ** -0.5),
        'conv_w': nrm(ks[20], (DEPTH, CONV_W, D_FF), CONV_W ** -0.5),
        'conv_b': nrm(ks[21], (DEPTH, D_FF), 0.02),
        'w_down': nrm(ks[22], (DEPTH, D_FF, D_MODEL), D_FF ** -0.5),
    }


def reference(x_prompt, x_sample, cache_k, cache_v, cache_logf, state_ret, state_conv,
              meta, g_attn, g_ffn, w_ret_in, w_ret_out, g_kv, w_kvf, b_f, g_k,
              w_q_b, g_q, w_o_b, w_gu, conv_w, conv_b, w_down):
    params = (g_attn, g_ffn, w_ret_in, w_ret_out, g_kv, w_kvf, b_f, g_k,
              w_q_b, g_q, w_o_b, w_gu, conv_w, conv_b, w_down)
    Bp, Sp = x_prompt.shape[0], x_prompt.shape[1]
    Ts = x_sample.shape[1]
    P = cache_k.shape[1]
    dt = x_prompt.dtype
    xp = jnp.concatenate([jnp.broadcast_to(meta.astype(dt)[None], (Bp, N_META, D_MODEL)), x_prompt], axis=1)
    pos_p = jnp.arange(N_META + Sp)
    ret0 = jnp.zeros((N_A, Bp, H_A, DK_A, DV_A), dt)
    conv0 = jnp.zeros((DEPTH, Bp, CONV_W - 1, D_FF), dt)
    yp, k_p, v_p, lf_p, ret_p, conv_p = trunk(xp, pos_p, ret0, conv0, None, True, *params)
    pos_s = N_META + P + jnp.arange(Ts)
    ys, k_s, v_s, lf_s, ret_s, conv_s = trunk(x_sample, pos_s, state_ret, state_conv,
                                              (cache_k, cache_v, cache_logf), False, *params)
    return (yp[:, N_META:], ys, k_p, v_p, lf_p, ret_p, conv_p, k_s, v_s, lf_s, ret_s, conv_s)
```

```python
import functools

import jax
import jax.numpy as jnp
from jax import lax
from jax.experimental import pallas as pl
from jax.experimental.pallas import tpu as pltpu

EPS = 1e-6
ROPE_BASE = 10000.0
MASK_VALUE = -1e30
LANE = 128
SUBLANE = 8
VMEM_BUDGET = 44 * 2 ** 20
F32 = jnp.float32
BF16 = jnp.bfloat16


def _params(semantics, vmem_bytes):
    return pltpu.CompilerParams(dimension_semantics=semantics,
                                vmem_limit_bytes=int(min(vmem_bytes + 8 * 2 ** 20, 60 * 2 ** 20)))


def _divisor_tile(n, cap, mult):
    if n <= cap:
        return n
    t = cap - cap % mult
    while t >= mult:
        if n % t == 0:
            return t
        t -= mult
    raise ValueError(f"no tile for {n} (cap {cap}, multiple {mult})")


def _rmsnorm_kernel(x_ref, g_ref, *o_refs):
    x = x_ref[...]
    y = x * lax.rsqrt(jnp.mean(x * x, axis=-1, keepdims=True) + EPS)
    for n, o_ref in enumerate(o_refs):
        o_ref[...] = (y * g_ref[n:n + 1, :]).astype(o_ref.dtype)


def rmsnorm_rows(x, gains):
    m, d = x.shape
    g = gains.shape[0]
    tm = _divisor_tile(m, 256, 16)
    vmem = 2 * tm * d * (4 + 2 * g)
    outs = pl.pallas_call(
        _rmsnorm_kernel,
        grid=(m // tm,),
        in_specs=[pl.BlockSpec((tm, d), lambda i: (i, 0)),
                  pl.BlockSpec((g, d), lambda i: (0, 0))],
        out_specs=[pl.BlockSpec((tm, d), lambda i: (i, 0)) for _ in range(g)],
        out_shape=[jax.ShapeDtypeStruct((m, d), BF16) for _ in range(g)],
        compiler_params=_params(("parallel",), vmem),
        name="rmsnorm",
    )(x, gains)
    return tuple(outs)


def _epi_plain(acc, o_ref):
    o_ref[...] = acc.astype(o_ref.dtype)


def _epi_residual(acc, res_ref, o_ref):
    o_ref[...] = res_ref[...] + acc


def _epi_rope(acc, cos_ref, sin_ref, o_ref, *, head_dim, scale):
    half = head_dim // 2
    cos = cos_ref[...]
    sin = sin_ref[...]
    for h in range(acc.shape[1] // head_dim):
        x1 = acc[:, h * head_dim:h * head_dim + half]
        x2 = acc[:, h * head_dim + half:(h + 1) * head_dim]
        o_ref[:, h * head_dim:h * head_dim + half] = ((x1 * cos - x2 * sin) * scale).astype(o_ref.dtype)
        o_ref[:, h * head_dim + half:(h + 1) * head_dim] = ((x1 * sin + x2 * cos) * scale).astype(o_ref.dtype)


def _epi_headnorm(acc, g_ref, o_ref, *, head_dim, scale):
    g = g_ref[...]
    for h in range(acc.shape[1] // head_dim):
        x = acc[:, h * head_dim:(h + 1) * head_dim]
        y = x * lax.rsqrt(jnp.mean(x * x, axis=-1, keepdims=True) + EPS) * g
        o_ref[:, h * head_dim:(h + 1) * head_dim] = (y * scale).astype(o_ref.dtype)


def _epi_logsigmoid(acc, b_ref, o_ref):
    z = acc + b_ref[...]
    o_ref[...] = jnp.minimum(z, 0.0) - jnp.log1p(jnp.exp(-jnp.abs(z)))


def _mm_kernel(x_ref, w_ref, *rest, epilogue, nk):
    if nk == 1:
        epilogue(jnp.dot(x_ref[...], w_ref[...], preferred_element_type=F32), *rest)
        return
    acc_ref = rest[-1]
    k = pl.program_id(2)

    @pl.when(k == 0)
    def _():
        acc_ref[...] = jnp.zeros_like(acc_ref)

    acc_ref[...] += jnp.dot(x_ref[...], w_ref[...], preferred_element_type=F32)

    @pl.when(k == nk - 1)
    def _():
        epilogue(acc_ref[...], *rest[:-1])


def _mm_tiles(m, k, n, out_bytes, extra_bytes_per_elem):
    tm = _divisor_tile(m, 1024, 16)
    for tn in (1024, 512, 256, 128):
        if n % tn:
            continue
        for nk in (1, 2, 4, 8):
            if k % nk or (k // nk) % LANE:
                continue
            tk = k // nk
            ws = 2 * (tm * tk * 2 + tk * tn * 2 + tm * tn * (out_bytes + extra_bytes_per_elem))
            ws += tm * tn * 4 * (2 if nk > 1 else 1)
            if ws <= VMEM_BUDGET:
                return tm, tn, tk, ws
    raise ValueError(f"no matmul tiling for {(m, k, n)}")


def matmul(x, w, *, col_off=0, n_out=None, epilogue=_epi_plain, extras=(), extra_specs=lambda tm, tn: [],
           out_dtype=F32, extra_bytes_per_elem=0, row_tile=None):
    m, k = x.shape
    n_out = w.shape[1] - col_off if n_out is None else n_out
    tm, tn, tk, ws = _mm_tiles(m, k, n_out, jnp.dtype(out_dtype).itemsize, extra_bytes_per_elem)
    if row_tile is not None:
        tm = row_tile
    assert col_off % tn == 0 and m % tm == 0
    nk = k // tk
    joff = col_off // tn
    kern = functools.partial(_mm_kernel, epilogue=epilogue, nk=nk)
    return pl.pallas_call(
        kern,
        grid=(m // tm, n_out // tn, nk),
        in_specs=[pl.BlockSpec((tm, tk), lambda i, j, kk: (i, kk)),
                  pl.BlockSpec((tk, tn), lambda i, j, kk: (kk, j + joff)),
                  *extra_specs(tm, tn)],
        out_specs=pl.BlockSpec((tm, tn), lambda i, j, kk: (i, j)),
        out_shape=jax.ShapeDtypeStruct((m, n_out), out_dtype),
        scratch_shapes=[pltpu.VMEM((tm, tn), F32)] if nk > 1 else [],
        compiler_params=_params(("parallel", "parallel", "arbitrary"), ws),
        name="matmul_" + getattr(epilogue, "func", epilogue).__name__.lstrip("_"),
    )(x, w, *extras)


def matmul_residual(x, w, res):
    return matmul(x, w, epilogue=_epi_residual, extras=(res,), extra_bytes_per_elem=4,
                  extra_specs=lambda tm, tn: [pl.BlockSpec((tm, tn), lambda i, j, kk: (i, j))])


def matmul_rope(x, w, cos, sin, *, col_off, n_out, head_dim, scale, period_rows):
    half = head_dim // 2

    def specs(tm, tn):
        assert period_rows % tm == 0 and tn % head_dim == 0
        nper = period_rows // tm
        return [pl.BlockSpec((tm, half), lambda i, j, kk: (i % nper, 0))] * 2

    row_tile = _divisor_tile(period_rows, _divisor_tile(x.shape[0], 1024, 16), 16)
    return matmul(x, w, col_off=col_off, n_out=n_out, out_dtype=BF16, extras=(cos, sin), extra_specs=specs,
                  row_tile=row_tile,
                  epilogue=functools.partial(_epi_rope, head_dim=head_dim, scale=scale))


def matmul_headnorm(x, w, gain, *, col_off, n_out, head_dim, scale, out_dtype):
    return matmul(x, w, col_off=col_off, n_out=n_out, out_dtype=out_dtype, extras=(gain.reshape(1, head_dim),),
                  extra_specs=lambda tm, tn: [pl.BlockSpec((1, head_dim), lambda i, j, kk: (0, 0))],
                  epilogue=functools.partial(_epi_headnorm, head_dim=head_dim, scale=scale))


def matmul_logsigmoid(x, w, bias):
    return matmul(x, w, extras=(bias,), epilogue=_epi_logsigmoid,
                  extra_specs=lambda tm, tn: [pl.BlockSpec((1, tn), lambda i, j, kk: (0, j))])


def _retention_kernel(lg_ref, q_ref, k_ref, v_ref, gate_ref, s0_ref, og_ref, sout_ref, s_scr, *, nc):
    h = pl.program_id(1)
    c = pl.program_id(2)

    @pl.when(c == 0)
    def _():
        s_scr[...] = s0_ref[0, 0]

    lg = lg_ref[h]
    q = q_ref[...]
    k = k_ref[...]
    v = v_ref[...]
    t = q.shape[0]
    rel = (lax.broadcasted_iota(jnp.int32, (t, t), 0) - lax.broadcasted_iota(jnp.int32, (t, t), 1)).astype(F32)
    causal = rel >= 0.0
    decay = jnp.where(causal, jnp.exp(jnp.where(causal, rel, 0.0) * lg), 0.0)
    scores = lax.dot_general(q, k, (((1,), (1,)), ((), ())), preferred_element_type=F32) * decay
    o = jnp.dot(scores.astype(BF16), v, preferred_element_type=F32)
    idx = lax.broadcasted_iota(jnp.int32, (t, 1), 0).astype(F32)
    s_old = s_scr[...]
    q_dec = (q.astype(F32) * jnp.exp((idx + 1.0) * lg)).astype(BF16)
    o = o + jnp.dot(q_dec, s_old.astype(BF16), preferred_element_type=F32)
    k_dec = (k.astype(F32) * jnp.exp((t - 1.0 - idx) * lg)).astype(BF16)
    kv = lax.dot_general(k_dec, v, (((0,), (0,)), ((), ())), preferred_element_type=F32)
    s_new = jnp.exp(jnp.full((1, 1), t, F32) * lg) * s_old + kv
    s_scr[...] = s_new
    on = o * lax.rsqrt(jnp.mean(o * o, axis=-1, keepdims=True) + EPS)
    g = gate_ref[...]
    og_ref[...] = ((g * jax.nn.sigmoid(g)) * on).astype(og_ref.dtype)

    @pl.when(c == nc - 1)
    def _():
        sout_ref[0, 0] = s_new


def retention(log_g, q, k, v, gate, s0, *, nb, rows, chunk, row_off, heads):
    dk = q.shape[1] // heads
    dv = v.shape[1] // heads
    nc = rows // chunk
    assert rows % chunk == 0 and row_off % chunk == 0
    roff = row_off // chunk
    shared = s0.shape[0] == 1
    row_map = lambda b, h, c: (roff + b * nc + c, h)
    vmem = 2 * chunk * (2 * dk * 2 + dv * 2 + dv * 4 + dv * 2) + 5 * dk * dv * 4 + 8 * chunk * chunk * 4
    og, s_out = pl.pallas_call(
        functools.partial(_retention_kernel, nc=nc),
        grid=(nb, heads, nc),
        in_specs=[pl.BlockSpec(memory_space=pltpu.SMEM),
                  pl.BlockSpec((chunk, dk), row_map),
                  pl.BlockSpec((chunk, dk), row_map),
                  pl.BlockSpec((chunk, dv), row_map),
                  pl.BlockSpec((chunk, dv), row_map),
                  pl.BlockSpec((1, 1, dk, dv), lambda b, h, c: (0 if shared else b, h, 0, 0))],
        out_specs=[pl.BlockSpec((chunk, dv), lambda b, h, c: (b * nc + c, h)),
                   pl.BlockSpec((1, 1, dk, dv), lambda b, h, c: (b, h, 0, 0))],
        out_shape=[jax.ShapeDtypeStruct((nb * rows, heads * dv), BF16),
                   jax.ShapeDtypeStruct((nb, heads, dk, dv), F32)],
        scratch_shapes=[pltpu.VMEM((dk, dv), F32)],
        compiler_params=_params(("parallel", "parallel", "arbitrary"), vmem),
        name="retention",
    )(log_g, q, k, v, gate, s0)
    return og, s_out


def _conv_gate_kernel(a_ref, u_ref, halo_ref, state_ref, cw_ref, cb_ref, o_ref, *, tiles_per_stream):
    i = pl.program_id(0)
    a = a_ref[...]
    tr = a.shape[0]
    first = (i % tiles_per_stream) == 0
    halo = jnp.where(first, state_ref[0], halo_ref[...])
    row = lax.broadcasted_iota(jnp.int32, a.shape, 0)
    prev1 = jnp.where(row == 0, halo[7:8, :], pltpu.roll(a, 1, axis=0))
    prev2 = jnp.where(row == 0, halo[6:7, :], jnp.where(row == 1, halo[7:8, :], pltpu.roll(a, 2, axis=0)))
    c = cb_ref[...] + cw_ref[0:1, :] * prev2
    c = c + cw_ref[1:2, :] * prev1
    c = c + cw_ref[2:3, :] * a
    o_ref[...] = ((c * jax.nn.sigmoid(c)) * u_ref[...]).astype(o_ref.dtype)


def conv_gate(gu, state, cw, cb, *, rows_per_stream, row_tile):
    m = gu.shape[0]
    f = gu.shape[1] // 2
    tr = row_tile
    tc = _divisor_tile(f, 1024, LANE)
    assert rows_per_stream % tr == 0 and tr % SUBLANE == 0
    tps = rows_per_stream // tr
    nfb = f // tc
    hb = tr // SUBLANE
    vmem = 2 * (tr * tc * (4 + 4 + 2) + 3 * 8 * tc * 4) + 6 * tr * tc * 4
    return pl.pallas_call(
        functools.partial(_conv_gate_kernel, tiles_per_stream=tps),
        grid=(m // tr, nfb),
        in_specs=[pl.BlockSpec((tr, tc), lambda i, j: (i, j)),
                  pl.BlockSpec((tr, tc), lambda i, j: (i, j + nfb)),
                  pl.BlockSpec((SUBLANE, tc), lambda i, j: (jnp.maximum(i * hb - 1, 0), j)),
                  pl.BlockSpec((1, SUBLANE, tc), lambda i, j: (i // tps, 0, j)),
                  pl.BlockSpec((SUBLANE, tc), lambda i, j: (0, j)),
                  pl.BlockSpec((1, tc), lambda i, j: (0, j))],
        out_specs=pl.BlockSpec((tr, tc), lambda i, j: (i, j)),
        out_shape=jax.ShapeDtypeStruct((m, f), BF16),
        compiler_params=_params(("parallel", "parallel"), vmem),
        name="conv_gate",
    )(gu, gu, gu, state, cw, cb)


def _cumsum_kernel(x_ref, c0_ref, o_ref, *, blk):
    t = x_ref.shape[1]
    r = lax.broadcasted_iota(jnp.int32, (blk, blk), 0)
    s = lax.broadcasted_iota(jnp.int32, (blk, blk), 1)
    tri = (s <= r).astype(BF16)
    carry = c0_ref[0]
    for b in range(t // blk):
        x = x_ref[0, b * blk:(b + 1) * blk, :]
        x_hi = x.astype(BF16)
        r1 = x - x_hi.astype(F32)
        x_mid = r1.astype(BF16)
        x_lo = (r1 - x_mid.astype(F32)).astype(BF16)
        c = jnp.dot(tri, x_hi, preferred_element_type=F32)
        c = c + jnp.dot(tri, x_mid, preferred_element_type=F32)
        c = c + jnp.dot(tri, x_lo, preferred_element_type=F32)
        c = c + carry
        o_ref[0, b * blk:(b + 1) * blk, :] = c
        carry = c[blk - 1:blk, :]


def cumsum_rows(x, c0):
    nb, t, l = x.shape
    blk = _divisor_tile(t, 256, 16)
    shared = c0.shape[0] == 1
    return pl.pallas_call(
        functools.partial(_cumsum_kernel, blk=blk),
        grid=(nb,),
        in_specs=[pl.BlockSpec((1, t, l), lambda b: (b, 0, 0)),
                  pl.BlockSpec((1, 1, l), lambda b: (0 if shared else b, 0, 0))],
        out_specs=pl.BlockSpec((1, t, l), lambda b: (b, 0, 0)),
        out_shape=jax.ShapeDtypeStruct((nb, t, l), F32),
        compiler_params=_params(("parallel",), 4 * t * l * 4 + 2 ** 20),
        name="cumsum",
    )(x, c0)


def _fox_kernel(*refs, tq, prefix):
    if prefix:
        q_ref, cq_ref, km_ref, vm_ref, ckm_ref, kp_ref, vp_ref, ckp_ref, o_ref = refs
    else:
        q_ref, cq_ref, km_ref, vm_ref, ckm_ref, o_ref = refs
    qi = pl.program_id(2)
    q = q_ref[...]
    cq = cq_ref[0]
    hd = q.shape[1]

    def attend(carry, kb, vb, ck, mask):
        m, l, acc = carry
        s = lax.dot_general(q, kb.astype(BF16), (((1,), (1,)), ((), ())), preferred_element_type=F32)
        s = s + cq - ck
        if mask is not None:
            s = jnp.where(mask, s, MASK_VALUE)
        m_new = jnp.maximum(m, jnp.max(s, axis=-1, keepdims=True))
        alpha = jnp.exp(m - m_new)
        p = jnp.exp(s - m_new)
        l = alpha * l + jnp.sum(p, axis=-1, keepdims=True)
        acc = alpha * acc + jnp.dot(p.astype(BF16), vb.astype(BF16), preferred_element_type=F32)
        return m_new, l, acc

    carry = (jnp.full((tq, 1), MASK_VALUE, F32), jnp.zeros((tq, 1), F32), jnp.zeros((tq, hd), F32))
    if prefix:
        carry = attend(carry, kp_ref[0], vp_ref[0], ckp_ref[0], None)

    def body(j, carry):
        start = pl.multiple_of(j * tq, tq)
        return attend(carry, km_ref[pl.ds(start, tq), :], vm_ref[pl.ds(start, tq), :], ckm_ref[0, j], None)

    carry = lax.fori_loop(0, qi, body, carry)
    start = pl.multiple_of(qi * tq, tq)
    diag = lax.broadcasted_iota(jnp.int32, (tq, tq), 1) <= lax.broadcasted_iota(jnp.int32, (tq, tq), 0)
    m, l, acc = attend(carry, km_ref[pl.ds(start, tq), :], vm_ref[pl.ds(start, tq), :], ckm_ref[0, qi], diag)
    o_ref[...] = (acc / l).astype(o_ref.dtype)


def fox_attention(q, cq, km, vm, ckm, prefix, *, nb, rows, row_off, heads, q_tile):
    hd = q.shape[1] // heads
    tq = q_tile
    nq = rows // tq
    assert rows % tq == 0 and row_off % rows == 0
    qoff = row_off // tq
    boff = row_off // rows
    in_specs = [pl.BlockSpec((tq, hd), lambda b, h, i: (qoff + b * nq + i, h)),
                pl.BlockSpec((1, tq, 1), lambda b, h, i: (b * heads + h, i, 0)),
                pl.BlockSpec((rows, hd), lambda b, h, i: (boff + b, h)),
                pl.BlockSpec((rows, hd), lambda b, h, i: (boff + b, h)),
                pl.BlockSpec((1, nq, 1, tq), lambda b, h, i: (b * heads + h, 0, 0, 0))]
    args = [q, cq, km, vm, ckm.reshape(nb * heads, nq, 1, tq)]
    vmem = 2 * (tq * hd * 2 + tq * LANE * 4 + 2 * rows * hd * 4 + 8 * rows * 4) + 8 * tq * tq * 4
    if prefix is not None:
        kp, vp, ckp = prefix
        p = kp.shape[1]
        shared = kp.shape[0] == 1
        in_specs += [pl.BlockSpec((1, p, hd), lambda b, h, i: (0 if shared else b, 0, h)),
                     pl.BlockSpec((1, p, hd), lambda b, h, i: (0 if shared else b, 0, h)),
                     pl.BlockSpec((1, 1, p), lambda b, h, i: (h if shared else b * heads + h, 0, 0))]
        args += [kp, vp, ckp]
        vmem += 2 * (2 * p * hd * 4 + 8 * p * 4) + 4 * tq * p * 4
    return pl.pallas_call(
        functools.partial(_fox_kernel, tq=tq, prefix=prefix is not None),
        grid=(nb, heads, nq),
        in_specs=in_specs,
        out_specs=pl.BlockSpec((tq, hd), lambda b, h, i: (b * nq + i, h)),
        out_shape=jax.ShapeDtypeStruct((nb * rows, heads * hd), BF16),
        compiler_params=_params(("parallel", "parallel", "parallel"), vmem),
        name="fox_attention",
    )(*args)


def _rope_tables(pos, half):
    inv = ROPE_BASE ** (-jnp.arange(half, dtype=F32) / half)
    ang = pos.astype(F32)[:, None] * inv[None, :]
    return jnp.cos(ang), jnp.sin(ang)


def _head_major(c, heads):
    nb, t, _ = c.shape
    ct = jnp.swapaxes(c[:, :, :heads], 1, 2).reshape(nb * heads, t)
    return ct[:, :, None], ct[:, None, :]


def kernel(x_prompt, x_sample, cache_k, cache_v, cache_logf, state_ret, state_conv, meta, g_attn, g_ffn,
           w_ret_in, w_ret_out, g_kv, w_kvf, b_f, g_k, w_q_b, g_q, w_o_b, w_gu, conv_w, conv_b, w_down):
    nbp, seq, d = x_prompt.shape
    nbs, dseq, _ = x_sample.shape
    n_meta = meta.shape[0]
    past = cache_k.shape[1]
    depth = g_attn.shape[0]
    n_a = w_ret_in.shape[0]
    h_a, dk_a, dv_a = state_ret.shape[2:]
    h_b, hd_b = cache_k.shape[2:]
    qk_w, v_w = h_a * dk_a, h_a * dv_a
    d_ff = w_down.shape[1]
    cw_taps = conv_w.shape[1]
    assert n_a == 1 and depth == 2 and cw_taps == 3, "kernel is written for one retention + one attention layer"
    ff_tile = 1024 if d_ff >= 4096 else 256
    ffp = -(-d_ff // ff_tile) * ff_tile
    m_main = nbp * seq
    m_s = nbs * dseq
    m_rest = m_s + n_meta
    dt = x_prompt.dtype

    w_in = w_ret_in[0].astype(BF16)
    w_out = w_ret_out[0].astype(BF16)
    w_kv = w_kvf[:, :2 * d].astype(BF16)
    w_f = jnp.pad(w_kvf[:, 2 * d:], ((0, 0), (0, LANE - h_b))).astype(BF16)
    b_fp = jnp.pad(b_f, (0, LANE - h_b)).reshape(1, LANE)
    w_q = w_q_b[0].astype(BF16)
    w_o = w_o_b[0].astype(BF16)
    padc = lambda a: jnp.pad(a, ((0, 0),) * (a.ndim - 1) + ((0, ffp - d_ff),))
    w_gu_p = [jnp.concatenate([padc(w_gu[l][:, :d_ff]), padc(w_gu[l][:, d_ff:])], axis=1).astype(BF16)
              for l in range(depth)]
    w_dn_p = [jnp.pad(w_down[l], ((0, ffp - d_ff), (0, 0))).astype(BF16) for l in range(depth)]
    cw_p = [jnp.pad(padc(conv_w[l]), ((0, SUBLANE - cw_taps), (0, 0))) for l in range(depth)]
    cb_p = [padc(conv_b[l]).reshape(1, ffp) for l in range(depth)]
    log_g = jnp.log1p(-jnp.exp2(-5.0 - jnp.arange(h_a, dtype=F32)))

    x_main = x_prompt.reshape(m_main, d)
    x_rest = jnp.concatenate([x_sample.reshape(m_s, d), meta.astype(dt)], axis=0)
    cos_m, sin_m = _rope_tables(n_meta + jnp.arange(seq), dk_a // 2)
    pos_rest = jnp.concatenate([jnp.tile(n_meta + past + jnp.arange(dseq), nbs), jnp.arange(n_meta)])
    cos_r, sin_r = _rope_tables(pos_rest, dk_a // 2)

    def state_rows(rows2):
        return jnp.pad(padc(rows2), ((0, 0), (SUBLANE - 2, 0), (0, 0)))

    def conv_ffn(x, l, state, rows_per_stream, row_tile):
        (hn,) = rmsnorm_rows(x, g_ffn[l:l + 1])
        gu = matmul(hn, w_gu_p[l])
        act = conv_gate(gu, state, cw_p[l], cb_p[l], rows_per_stream=rows_per_stream, row_tile=row_tile)
        return matmul_residual(act, w_dn_p[l], x), gu

    def ret_in(h, cos, sin, period):
        q = matmul_rope(h, w_in, cos, sin, col_off=0, n_out=qk_w, head_dim=dk_a, scale=1.0, period_rows=period)
        k = matmul_rope(h, w_in, cos, sin, col_off=qk_w, n_out=qk_w, head_dim=dk_a, scale=dk_a ** -0.5,
                        period_rows=period)
        v = matmul(h, w_in, col_off=2 * qk_w, n_out=v_w, out_dtype=BF16)
        gate = matmul(h, w_in, col_off=2 * qk_w + v_w, n_out=v_w)
        return q, k, v, gate

    (h_r,) = rmsnorm_rows(x_rest, g_attn[0:1])
    q_r, k_r, v_r, gate_r = ret_in(h_r, cos_r, sin_r, m_rest)
    og_s, ret_s = retention(log_g, q_r, k_r, v_r, gate_r, state_ret[0], nb=nbs, rows=dseq, chunk=dseq,
                            row_off=0, heads=h_a)
    og_m, ret_meta = retention(log_g, q_r, k_r, v_r, gate_r, jnp.zeros((1, h_a, dk_a, dv_a), F32), nb=1,
                               rows=n_meta, chunk=n_meta, row_off=m_s, heads=h_a)
    x_r1 = matmul_residual(jnp.concatenate([og_s, og_m], axis=0), w_out, x_rest)
    state_r0 = jnp.concatenate([state_rows(state_conv[0]), jnp.zeros((1, SUBLANE, ffp), F32)], axis=0)
    assert dseq % n_meta == 0 and n_meta % SUBLANE == 0
    stream_of_tile = dseq // n_meta

    def conv_ffn_rest(x, l, state):
        (hn,) = rmsnorm_rows(x, g_ffn[l:l + 1])
        gu = matmul(hn, w_gu_p[l])
        act = conv_gate(gu, state, cw_p[l], cb_p[l], rows_per_stream=dseq, row_tile=n_meta)
        return matmul_residual(act, w_dn_p[l], x), gu

    x_r2, gu_r0 = conv_ffn_rest(x_r1, 0, state_r0)

    (h_m,) = rmsnorm_rows(x_main, g_attn[0:1])
    q_m, k_m, v_m, gate_m = ret_in(h_m, cos_m, sin_m, seq)
    chunk = _divisor_tile(seq, 256, 16)
    og_main, ret_p = retention(log_g, q_m, k_m, v_m, gate_m, ret_meta, nb=nbp, rows=seq, chunk=chunk,
                               row_off=0, heads=h_a)
    x_m1 = matmul_residual(og_main, w_out, x_main)
    a_meta0 = gu_r0[m_rest - 2:, :ffp]
    state_m0 = jnp.broadcast_to(state_rows(a_meta0[None, :, :d_ff]), (nbp, SUBLANE, ffp))
    conv_tile = _divisor_tile(seq, 512, SUBLANE)
    x_m2, gu_m0 = conv_ffn(x_m1, 0, state_m0, seq, conv_tile)

    def kv_side(x, g_row):
        h, hk = rmsnorm_rows(x, jnp.stack([g_row, g_kv]))
        kk = matmul_headnorm(hk, w_kv, g_k, col_off=0, n_out=d, head_dim=hd_b, scale=1.0, out_dtype=F32)
        vv = matmul(hk, w_kv, col_off=d, n_out=d)
        lf = matmul_logsigmoid(hk, w_f, b_fp)
        q = matmul_headnorm(h, w_q, g_q[0], col_off=0, n_out=d, head_dim=hd_b, scale=hd_b ** -0.5, out_dtype=BF16)
        return q, kk, vv, lf

    q_r, kk_r, vv_r, lf_r = kv_side(x_r2, g_attn[1])
    q_m, kk_m, vv_m, lf_m = kv_side(x_m2, g_attn[1])

    c_cache = cumsum_rows(jnp.pad(cache_logf, ((0, 0), (0, 0), (0, LANE - h_b))), jnp.zeros((1, 1, LANE), F32))
    c_s = cumsum_rows(lf_r[:m_s].reshape(nbs, dseq, LANE), c_cache[:, past - 1:past, :])
    c_meta = cumsum_rows(lf_r[m_s:].reshape(1, n_meta, LANE), jnp.zeros((1, 1, LANE), F32))
    c_main = cumsum_rows(lf_m.reshape(nbp, seq, LANE), c_meta[:, n_meta - 1:n_meta, :])

    cq_s, ck_s = _head_major(c_s, h_b)
    _, ck_cache = _head_major(c_cache, h_b)
    o_s = fox_attention(q_r, cq_s, kk_r, vv_r, ck_s,
                        (cache_k.reshape(nbs, past, d), cache_v.reshape(nbs, past, d), ck_cache),
                        nb=nbs, rows=dseq, row_off=0, heads=h_b, q_tile=dseq)
    cq_meta, ck_meta = _head_major(c_meta, h_b)
    o_meta = fox_attention(q_r, cq_meta, kk_r, vv_r, ck_meta, None,
                           nb=1, rows=n_meta, row_off=m_s, heads=h_b, q_tile=n_meta)
    cq_main, ck_main = _head_major(c_main, h_b)
    k_meta = kk_r[m_s:]
    v_meta = vv_r[m_s:]
    o_main = fox_attention(q_m, cq_main, kk_m, vv_m, ck_main, (k_meta[None], v_meta[None], ck_meta),
                           nb=nbp, rows=seq, row_off=0, heads=h_b, q_tile=_divisor_tile(seq, 512, 16))

    x_r3 = matmul_residual(jnp.concatenate([o_s, o_meta], axis=0), w_o, x_r2)
    state_r1 = jnp.concatenate([state_rows(state_conv[1]), jnp.zeros((1, SUBLANE, ffp), F32)], axis=0)
    x_r4, gu_r1 = conv_ffn_rest(x_r3, 1, state_r1)
    x_m3 = matmul_residual(o_main, w_o, x_m2)
    a_meta1 = gu_r1[m_rest - 2:, :ffp]
    state_m1 = jnp.broadcast_to(state_rows(a_meta1[None, :, :d_ff]), (nbp, SUBLANE, ffp))
    x_m4, gu_m1 = conv_ffn(x_m3, 1, state_m1, seq, conv_tile)

    def with_meta(meta_rows, main_rows, tail):
        mr = jnp.broadcast_to(meta_rows.reshape((1, n_meta) + tail), (nbp, n_meta) + tail)
        return jnp.concatenate([mr, main_rows.reshape((nbp, seq) + tail)], axis=1)

    y_prompt = x_m4.reshape(nbp, seq, d)
    y_sample = x_r4[:m_s].reshape(nbs, dseq, d)
    k_prompt = with_meta(k_meta, kk_m, (h_b, hd_b))
    v_prompt = with_meta(v_meta, vv_m, (h_b, hd_b))
    logf_prompt = with_meta(lf_r[m_s:, :h_b], lf_m[:, :h_b], (h_b,))
    conv_prompt = jnp.stack([gu.reshape(nbp, seq, 2 * ffp)[:, seq - 2:, :d_ff] for gu in (gu_m0, gu_m1)])
    conv_sample = jnp.stack([gu[:m_s].reshape(nbs, dseq, 2 * ffp)[:, dseq - 2:, :d_ff] for gu in (gu_r0, gu_r1)])
    return (y_prompt, y_sample, k_prompt, v_prompt, logf_prompt, ret_p[None], conv_prompt,
            kk_r[:m_s].reshape(nbs, dseq, h_b, hd_b), vv_r[:m_s].reshape(nbs, dseq, h_b, hd_b),
            lf_r[:m_s, :h_b].reshape(nbs, dseq, h_b), ret_s[None], conv_sample)
```

```python
import functools

import jax
import jax.numpy as jnp
from jax import lax
from jax.experimental import pallas as pl
from jax.experimental.pallas import tpu as pltpu

EPS = 1e-6
LOG2E = 1.4426950408889634
ROPE_BASE = 10000.0
MASK_VALUE = -1e30
LANE = 128
SUBLANE = 8
VMEM_BUDGET = 44 * 2 ** 20
F32 = jnp.float32
BF16 = jnp.bfloat16


def _params(semantics, vmem_bytes):
    return pltpu.CompilerParams(dimension_semantics=semantics,
                                vmem_limit_bytes=int(min(vmem_bytes + 8 * 2 ** 20, 60 * 2 ** 20)))


def _divisor_tile(n, cap, mult):
    if n <= cap:
        return n
    t = cap - cap % mult
    while t >= mult:
        if n % t == 0:
            return t
        t -= mult
    raise ValueError(f"no tile for {n} (cap {cap}, multiple {mult})")


def _rmsnorm_kernel(x_ref, g_ref, *o_refs):
    x = x_ref[...]
    y = x * lax.rsqrt(jnp.mean(x * x, axis=-1, keepdims=True) + EPS)
    for n, o_ref in enumerate(o_refs):
        o_ref[...] = (y * g_ref[n:n + 1, :]).astype(o_ref.dtype)


def rmsnorm_rows(x, gains):
    m, d = x.shape
    g = gains.shape[0]
    tm = _divisor_tile(m, 256, 16)
    vmem = 2 * tm * d * (4 + 2 * g)
    outs = pl.pallas_call(
        _rmsnorm_kernel,
        grid=(m // tm,),
        in_specs=[pl.BlockSpec((tm, d), lambda i: (i, 0)),
                  pl.BlockSpec((g, d), lambda i: (0, 0))],
        out_specs=[pl.BlockSpec((tm, d), lambda i: (i, 0)) for _ in range(g)],
        out_shape=[jax.ShapeDtypeStruct((m, d), BF16) for _ in range(g)],
        compiler_params=_params(("parallel",), vmem),
        name="rmsnorm",
    )(x, gains)
    return tuple(outs)


def _epi_plain(acc, o_ref):
    o_ref[...] = acc.astype(o_ref.dtype)


def _epi_residual(acc, res_ref, o_ref):
    o_ref[...] = res_ref[...] + acc


def _epi_rope(acc, cos_ref, sin_ref, o_ref, *, head_dim, scale):
    half = head_dim // 2
    cos = cos_ref[...]
    sin = sin_ref[...]
    for h in range(acc.shape[1] // head_dim):
        x1 = acc[:, h * head_dim:h * head_dim + half]
        x2 = acc[:, h * head_dim + half:(h + 1) * head_dim]
        o_ref[:, h * head_dim:h * head_dim + half] = ((x1 * cos - x2 * sin) * scale).astype(o_ref.dtype)
        o_ref[:, h * head_dim + half:(h + 1) * head_dim] = ((x1 * sin + x2 * cos) * scale).astype(o_ref.dtype)


def _epi_headnorm(acc, g_ref, o_ref, *, head_dim, scale):
    g = g_ref[...]
    for h in range(acc.shape[1] // head_dim):
        x = acc[:, h * head_dim:(h + 1) * head_dim]
        y = x * lax.rsqrt(jnp.mean(x * x, axis=-1, keepdims=True) + EPS) * g
        o_ref[:, h * head_dim:(h + 1) * head_dim] = (y * scale).astype(o_ref.dtype)


def _epi_logsigmoid(acc, b_ref, o_ref):
    z = acc + b_ref[...]
    o_ref[...] = jnp.minimum(z, 0.0) - jnp.log1p(jnp.exp(-jnp.abs(z)))


def _mm_kernel(x_ref, w_ref, *rest, epilogue, nk):
    if nk == 1:
        epilogue(jnp.dot(x_ref[...], w_ref[...], preferred_element_type=F32), *rest)
        return
    acc_ref = rest[-1]
    k = pl.program_id(2)

    @pl.when(k == 0)
    def _():
        acc_ref[...] = jnp.zeros_like(acc_ref)

    acc_ref[...] += jnp.dot(x_ref[...], w_ref[...], preferred_element_type=F32)

    @pl.when(k == nk - 1)
    def _():
        epilogue(acc_ref[...], *rest[:-1])


def _mm_tiles(m, k, n, out_bytes, extra_bytes_per_elem):
    tm = _divisor_tile(m, 1024, 16)
    for tn in (1024, 512, 256, 128):
        if n % tn:
            continue
        for nk in (1, 2, 4, 8):
            if k % nk or (k // nk) % LANE:
                continue
            tk = k // nk
            ws = 2 * (tm * tk * 2 + tk * tn * 2 + tm * tn * (out_bytes + extra_bytes_per_elem))
            ws += tm * tn * 4 * (2 if nk > 1 else 1)
            if ws <= VMEM_BUDGET:
                return tm, tn, tk, ws
    raise ValueError(f"no matmul tiling for {(m, k, n)}")


def matmul(x, w, *, col_off=0, n_out=None, epilogue=_epi_plain, extras=(), extra_specs=lambda tm, tn: [],
           out_dtype=F32, extra_bytes_per_elem=0, row_tile=None):
    m, k = x.shape
    n_out = w.shape[1] - col_off if n_out is None else n_out
    tm, tn, tk, ws = _mm_tiles(m, k, n_out, jnp.dtype(out_dtype).itemsize, extra_bytes_per_elem)
    if row_tile is not None:
        tm = row_tile
    assert col_off % tn == 0 and m % tm == 0
    nk = k // tk
    joff = col_off // tn
    kern = functools.partial(_mm_kernel, epilogue=epilogue, nk=nk)
    return pl.pallas_call(
        kern,
        grid=(m // tm, n_out // tn, nk),
        in_specs=[pl.BlockSpec((tm, tk), lambda i, j, kk: (i, kk)),
                  pl.BlockSpec((tk, tn), lambda i, j, kk: (kk, j + joff)),
                  *extra_specs(tm, tn)],
        out_specs=pl.BlockSpec((tm, tn), lambda i, j, kk: (i, j)),
        out_shape=jax.ShapeDtypeStruct((m, n_out), out_dtype),
        scratch_shapes=[pltpu.VMEM((tm, tn), F32)] if nk > 1 else [],
        compiler_params=_params(("parallel", "parallel", "arbitrary"), ws),
        name="matmul_" + getattr(epilogue, "func", epilogue).__name__.lstrip("_"),
    )(x, w, *extras)


def matmul_residual(x, w, res):
    return matmul(x, w, epilogue=_epi_residual, extras=(res,), extra_bytes_per_elem=4,
                  extra_specs=lambda tm, tn: [pl.BlockSpec((tm, tn), lambda i, j, kk: (i, j))])


def matmul_rope(x, w, cos, sin, *, col_off, n_out, head_dim, scale, period_rows):
    half = head_dim // 2

    def specs(tm, tn):
        assert period_rows % tm == 0 and tn % head_dim == 0
        nper = period_rows // tm
        return [pl.BlockSpec((tm, half), lambda i, j, kk: (i % nper, 0))] * 2

    row_tile = _divisor_tile(period_rows, _divisor_tile(x.shape[0], 1024, 16), 16)
    return matmul(x, w, col_off=col_off, n_out=n_out, out_dtype=BF16, extras=(cos, sin), extra_specs=specs,
                  row_tile=row_tile,
                  epilogue=functools.partial(_epi_rope, head_dim=head_dim, scale=scale))


def matmul_headnorm(x, w, gain, *, col_off, n_out, head_dim, scale, out_dtype):
    return matmul(x, w, col_off=col_off, n_out=n_out, out_dtype=out_dtype, extras=(gain.reshape(1, head_dim),),
                  extra_specs=lambda tm, tn: [pl.BlockSpec((1, head_dim), lambda i, j, kk: (0, 0))],
                  epilogue=functools.partial(_epi_headnorm, head_dim=head_dim, scale=scale))


def matmul_logsigmoid(x, w, bias):
    return matmul(x, w, extras=(bias,), epilogue=_epi_logsigmoid,
                  extra_specs=lambda tm, tn: [pl.BlockSpec((1, tn), lambda i, j, kk: (0, j))])


def _retention_kernel(lg_ref, q_ref, k_ref, v_ref, gate_ref, s0_ref, og_ref, sout_ref, s_scr, *, nc):
    h = pl.program_id(1)
    c = pl.program_id(2)

    @pl.when(c == 0)
    def _():
        s_scr[...] = s0_ref[0, 0]

    lg = lg_ref[h]
    q = q_ref[...]
    k = k_ref[...]
    v = v_ref[...]
    t = q.shape[0]
    rel = (lax.broadcasted_iota(jnp.int32, (t, t), 0) - lax.broadcasted_iota(jnp.int32, (t, t), 1)).astype(F32)
    causal = rel >= 0.0
    decay = jnp.where(causal, jnp.exp(jnp.where(causal, rel, 0.0) * lg), 0.0)
    scores = lax.dot_general(q, k, (((1,), (1,)), ((), ())), preferred_element_type=F32) * decay
    o = jnp.dot(scores.astype(BF16), v, preferred_element_type=F32)
    idx = lax.broadcasted_iota(jnp.int32, (t, 1), 0).astype(F32)
    s_old = s_scr[...]
    q_dec = (q.astype(F32) * jnp.exp((idx + 1.0) * lg)).astype(BF16)
    o = o + jnp.dot(q_dec, s_old.astype(BF16), preferred_element_type=F32)
    k_dec = (k.astype(F32) * jnp.exp((t - 1.0 - idx) * lg)).astype(BF16)
    kv = lax.dot_general(k_dec, v, (((0,), (0,)), ((), ())), preferred_element_type=F32)
    s_new = jnp.exp(jnp.full((1, 1), t, F32) * lg) * s_old + kv
    s_scr[...] = s_new
    on = o * lax.rsqrt(jnp.mean(o * o, axis=-1, keepdims=True) + EPS)
    g = gate_ref[...]
    og_ref[...] = ((g * jax.nn.sigmoid(g)) * on).astype(og_ref.dtype)

    @pl.when(c == nc - 1)
    def _():
        sout_ref[0, 0] = s_new


def retention(log_g, q, k, v, gate, s0, *, nb, rows, chunk, row_off, heads):
    dk = q.shape[1] // heads
    dv = v.shape[1] // heads
    nc = rows // chunk
    assert rows % chunk == 0 and row_off % chunk == 0
    roff = row_off // chunk
    shared = s0.shape[0] == 1
    row_map = lambda b, h, c: (roff + b * nc + c, h)
    vmem = 2 * chunk * (2 * dk * 2 + dv * 2 + dv * 4 + dv * 2) + 5 * dk * dv * 4 + 8 * chunk * chunk * 4
    og, s_out = pl.pallas_call(
        functools.partial(_retention_kernel, nc=nc),
        grid=(nb, heads, nc),
        in_specs=[pl.BlockSpec(memory_space=pltpu.SMEM),
                  pl.BlockSpec((chunk, dk), row_map),
                  pl.BlockSpec((chunk, dk), row_map),
                  pl.BlockSpec((chunk, dv), row_map),
                  pl.BlockSpec((chunk, dv), row_map),
                  pl.BlockSpec((1, 1, dk, dv), lambda b, h, c: (0 if shared else b, h, 0, 0))],
        out_specs=[pl.BlockSpec((chunk, dv), lambda b, h, c: (b * nc + c, h)),
                   pl.BlockSpec((1, 1, dk, dv), lambda b, h, c: (b, h, 0, 0))],
        out_shape=[jax.ShapeDtypeStruct((nb * rows, heads * dv), BF16),
                   jax.ShapeDtypeStruct((nb, heads, dk, dv), F32)],
        scratch_shapes=[pltpu.VMEM((dk, dv), F32)],
        compiler_params=_params(("parallel", "parallel", "arbitrary"), vmem),
        name="retention",
    )(log_g, q, k, v, gate, s0)
    return og, s_out


def _conv_gate_kernel(a_ref, u_ref, halo_ref, state_ref, cw_ref, cb_ref, o_ref, *, tiles_per_stream):
    i = pl.program_id(0)
    a = a_ref[...]
    tr = a.shape[0]
    first = (i % tiles_per_stream) == 0
    halo = jnp.where(first, state_ref[0], halo_ref[...])
    row = lax.broadcasted_iota(jnp.int32, a.shape, 0)
    prev1 = jnp.where(row == 0, halo[7:8, :], pltpu.roll(a, 1, axis=0))
    prev2 = jnp.where(row == 0, halo[6:7, :], jnp.where(row == 1, halo[7:8, :], pltpu.roll(a, 2, axis=0)))
    c = cb_ref[...] + cw_ref[0:1, :] * prev2
    c = c + cw_ref[1:2, :] * prev1
    c = c + cw_ref[2:3, :] * a
    o_ref[...] = ((c * jax.nn.sigmoid(c)) * u_ref[...]).astype(o_ref.dtype)


def conv_gate(gu, state, cw, cb, *, rows_per_stream, row_tile):
    m = gu.shape[0]
    f = gu.shape[1] // 2
    tr = row_tile
    tc = _divisor_tile(f, 1024, LANE)
    assert rows_per_stream % tr == 0 and tr % SUBLANE == 0
    tps = rows_per_stream // tr
    nfb = f // tc
    hb = tr // SUBLANE
    vmem = 2 * (tr * tc * (4 + 4 + 2) + 3 * 8 * tc * 4) + 6 * tr * tc * 4
    return pl.pallas_call(
        functools.partial(_conv_gate_kernel, tiles_per_stream=tps),
        grid=(m // tr, nfb),
        in_specs=[pl.BlockSpec((tr, tc), lambda i, j: (i, j)),
                  pl.BlockSpec((tr, tc), lambda i, j: (i, j + nfb)),
                  pl.BlockSpec((SUBLANE, tc), lambda i, j: (jnp.maximum(i * hb - 1, 0), j)),
                  pl.BlockSpec((1, SUBLANE, tc), lambda i, j: (i // tps, 0, j)),
                  pl.BlockSpec((SUBLANE, tc), lambda i, j: (0, j)),
                  pl.BlockSpec((1, tc), lambda i, j: (0, j))],
        out_specs=pl.BlockSpec((tr, tc), lambda i, j: (i, j)),
        out_shape=jax.ShapeDtypeStruct((m, f), BF16),
        compiler_params=_params(("parallel", "parallel"), vmem),
        name="conv_gate",
    )(gu, gu, gu, state, cw, cb)


def _cumsum_kernel(x_ref, c0_ref, o_ref, *, blk):
    t = x_ref.shape[1]
    r = lax.broadcasted_iota(jnp.int32, (blk, blk), 0)
    s = lax.broadcasted_iota(jnp.int32, (blk, blk), 1)
    tri = (s <= r).astype(BF16)
    carry = c0_ref[0]
    for b in range(t // blk):
        x = x_ref[0, b * blk:(b + 1) * blk, :]
        x_hi = x.astype(BF16)
        r1 = x - x_hi.astype(F32)
        x_mid = r1.astype(BF16)
        x_lo = (r1 - x_mid.astype(F32)).astype(BF16)
        c = jnp.dot(tri, x_hi, preferred_element_type=F32)
        c = c + jnp.dot(tri, x_mid, preferred_element_type=F32)
        c = c + jnp.dot(tri, x_lo, preferred_element_type=F32)
        c = c + carry
        o_ref[0, b * blk:(b + 1) * blk, :] = c
        carry = c[blk - 1:blk, :]


def cumsum_rows(x, c0):
    nb, t, l = x.shape
    blk = _divisor_tile(t, 256, 16)
    shared = c0.shape[0] == 1
    return pl.pallas_call(
        functools.partial(_cumsum_kernel, blk=blk),
        grid=(nb,),
        in_specs=[pl.BlockSpec((1, t, l), lambda b: (b, 0, 0)),
                  pl.BlockSpec((1, 1, l), lambda b: (0 if shared else b, 0, 0))],
        out_specs=pl.BlockSpec((1, t, l), lambda b: (b, 0, 0)),
        out_shape=jax.ShapeDtypeStruct((nb, t, l), F32),
        compiler_params=_params(("parallel",), 4 * t * l * 4 + 2 ** 20),
        name="cumsum",
    )(x, c0)


def _fox_kernel(*refs, tq, prefix):
    m_scr, cq_scr, alpha_scr, acc_scr, sa_scr, sb_scr, pa_scr, pb_scr, kb_scr, v1_scr = refs[-10:]
    if prefix:
        q_ref, cq_ref, km_ref, vm_ref, ckm_ref, kp_ref, vp_ref, ckp_ref, o_ref = refs[:-10]
    else:
        q_ref, cq_ref, km_ref, vm_ref, ckm_ref, o_ref = refs[:-10]
    qi = pl.program_id(2)
    q = q_ref[...]
    hd = q.shape[1]
    m_scr[...] = jnp.full(m_scr.shape, MASK_VALUE, F32)
    acc_scr[...] = jnp.zeros(acc_scr.shape, F32)
    cq_scr[...] = jnp.broadcast_to(cq_ref[0], cq_scr.shape)

    def lanes(x, width):
        return jnp.tile(x, (1, width // LANE)) if width % LANE == 0 else x[:, :1]

    def ones_columns(vb):
        return jnp.concatenate([vb.astype(BF16), jnp.ones((vb.shape[0], LANE), BF16)], axis=1)

    @pl.when(qi == 0)
    def _():
        for c in range(kb_scr.shape[0] // tq):
            rows = slice(c * tq, (c + 1) * tq)
            kb_scr[rows, :] = km_ref[rows, :].astype(BF16)
            v1_scr[rows, :] = ones_columns(vm_ref[rows, :])

    def scores(s_ref, kb):
        s_ref[:, :kb.shape[0]] = lax.dot_general(q, kb, (((1,), (1,)), ((), ())), preferred_element_type=F32)

    def softmax(s_ref, p_ref, ck, causal):
        width = ck.shape[1]
        rb = min(tq, max(16, 32768 // max(width, LANE)))
        for r in range(tq // rb):
            rows = slice(r * rb, (r + 1) * rb)
            s = s_ref[rows, :width] - ck
            if causal:
                row = r * rb + lax.broadcasted_iota(jnp.int32, (rb, width), 0)
                s = jnp.where(lax.broadcasted_iota(jnp.int32, (rb, width), 1) <= row, s, MASK_VALUE)
            cq = cq_scr[rows, :]
            m_old = m_scr[rows, :]
            m_new = jnp.maximum(m_old, jnp.max(s, axis=-1, keepdims=True) + cq)
            p_ref[rows, :width] = jnp.exp2(s - lanes(m_new - cq, width)).astype(BF16)
            m_scr[rows, :] = m_new
            alpha_scr[rows, :] = jnp.exp2(m_old - m_new)

    def accumulate(p, v1, rescale):
        pv = jnp.dot(p, v1, preferred_element_type=F32)
        if rescale:
            acc_scr[...] = (acc_scr[...] + pv) * jnp.tile(alpha_scr[...], (1, 2))
        else:
            acc_scr[...] += pv

    if prefix:
        pw = kp_ref.shape[1]
        scores(sa_scr, kp_ref[0].astype(BF16))
        softmax(sa_scr, pa_scr, ckp_ref[0], False)
        accumulate(pa_scr[:, :pw], ones_columns(vp_ref[0]), False)

    def block(j):
        return pl.ds(pl.multiple_of(j * tq, tq), tq)

    scores(sa_scr, kb_scr[block(0), :])
    pb_scr[...] = jnp.zeros(pb_scr.shape, BF16)

    def pair(t, carry):
        j = 2 * t
        scores(sb_scr, kb_scr[block(j + 1), :])
        softmax(sa_scr, pa_scr, ckm_ref[0, j], False)
        accumulate(pb_scr[:, :tq], v1_scr[block(jnp.maximum(j - 1, 0)), :], True)
        scores(sa_scr, kb_scr[block(j + 2), :])
        softmax(sb_scr, pb_scr, ckm_ref[0, j + 1], False)
        accumulate(pa_scr[:, :tq], v1_scr[block(j), :], True)
        return carry

    lax.fori_loop(0, qi // 2, pair, 0)
    last = jnp.maximum(qi - 1, 0)

    @pl.when(qi % 2 == 0)
    def _():
        softmax(sa_scr, pa_scr, ckm_ref[0, qi], True)
        accumulate(pb_scr[:, :tq], v1_scr[block(last), :], True)
        accumulate(pa_scr[:, :tq], v1_scr[block(qi), :], False)

    @pl.when(qi % 2 == 1)
    def _():
        scores(sb_scr, kb_scr[block(qi), :])
        softmax(sa_scr, pa_scr, ckm_ref[0, last], False)
        accumulate(pb_scr[:, :tq], v1_scr[block(jnp.maximum(qi - 2, 0)), :], True)
        softmax(sb_scr, pb_scr, ckm_ref[0, qi], True)
        accumulate(pa_scr[:, :tq], v1_scr[block(last), :], True)
        accumulate(pb_scr[:, :tq], v1_scr[block(qi), :], False)

    o_ref[...] = (acc_scr[:, :hd] / acc_scr[:, hd:]).astype(o_ref.dtype)


def fox_attention(q, cq, km, vm, ckm, prefix, *, nb, rows, row_off, heads, q_tile):
    hd = q.shape[1] // heads
    tq = q_tile
    nq = rows // tq
    assert rows % tq == 0 and row_off % rows == 0
    qoff = row_off // tq
    boff = row_off // rows
    in_specs = [pl.BlockSpec((tq, hd), lambda b, h, i: (qoff + b * nq + i, h)),
                pl.BlockSpec((1, tq, 1), lambda b, h, i: (b * heads + h, i, 0)),
                pl.BlockSpec((rows, hd), lambda b, h, i: (boff + b, h)),
                pl.BlockSpec((rows, hd), lambda b, h, i: (boff + b, h)),
                pl.BlockSpec((1, nq, 1, tq), lambda b, h, i: (b * heads + h, 0, 0, 0))]
    args = [q, cq, km, vm, ckm.reshape(nb * heads, nq, 1, tq)]
    assert hd == LANE
    vmem = 2 * (tq * hd * 2 + tq * LANE * 4 + 2 * rows * hd * 4 + 8 * rows * 4)
    width = tq
    if prefix is not None:
        kp, vp, ckp = prefix
        p = kp.shape[1]
        width = max(tq, p)
        shared = kp.shape[0] == 1
        in_specs += [pl.BlockSpec((1, p, hd), lambda b, h, i: (0 if shared else b, 0, h)),
                     pl.BlockSpec((1, p, hd), lambda b, h, i: (0 if shared else b, 0, h)),
                     pl.BlockSpec((1, 1, p), lambda b, h, i: (h if shared else b * heads + h, 0, 0))]
        args += [kp, vp, ckp]
        vmem += 2 * (2 * p * hd * 4 + 8 * p * 4)
    width = -(-width // LANE) * LANE
    vmem += tq * width * 12 + 5 * tq * LANE * 4 + rows * hd * 6 + 2 ** 21
    return pl.pallas_call(
        functools.partial(_fox_kernel, tq=tq, prefix=prefix is not None),
        grid=(nb, heads, nq),
        in_specs=in_specs,
        out_specs=pl.BlockSpec((tq, hd), lambda b, h, i: (b * nq + i, h)),
        out_shape=jax.ShapeDtypeStruct((nb * rows, heads * hd), BF16),
        scratch_shapes=[pltpu.VMEM((tq, LANE), F32), pltpu.VMEM((tq, LANE), F32), pltpu.VMEM((tq, LANE), F32),
                        pltpu.VMEM((tq, 2 * hd), F32),
                        pltpu.VMEM((tq, width), F32), pltpu.VMEM((tq, width), F32),
                        pltpu.VMEM((tq, width), BF16), pltpu.VMEM((tq, width), BF16),
                        pltpu.VMEM((rows, hd), BF16), pltpu.VMEM((rows, 2 * hd), BF16)],
        compiler_params=_params(("parallel", "parallel", "arbitrary"), vmem),
        name="fox_attention",
    )(*args)


def _rope_tables(pos, half):
    inv = ROPE_BASE ** (-jnp.arange(half, dtype=F32) / half)
    ang = pos.astype(F32)[:, None] * inv[None, :]
    return jnp.cos(ang), jnp.sin(ang)


def _head_major(c, heads):
    nb, t, _ = c.shape
    ct = jnp.swapaxes(c[:, :, :heads] * LOG2E, 1, 2).reshape(nb * heads, t)
    return ct[:, :, None], ct[:, None, :]


def kernel(x_prompt, x_sample, cache_k, cache_v, cache_logf, state_ret, state_conv, meta, g_attn, g_ffn,
           w_ret_in, w_ret_out, g_kv, w_kvf, b_f, g_k, w_q_b, g_q, w_o_b, w_gu, conv_w, conv_b, w_down):
    nbp, seq, d = x_prompt.shape
    nbs, dseq, _ = x_sample.shape
    n_meta = meta.shape[0]
    past = cache_k.shape[1]
    depth = g_attn.shape[0]
    n_a = w_ret_in.shape[0]
    h_a, dk_a, dv_a = state_ret.shape[2:]
    h_b, hd_b = cache_k.shape[2:]
    qk_w, v_w = h_a * dk_a, h_a * dv_a
    d_ff = w_down.shape[1]
    cw_taps = conv_w.shape[1]
    assert n_a == 1 and depth == 2 and cw_taps == 3, "kernel is written for one retention + one attention layer"
    ff_tile = 1024 if d_ff >= 4096 else 256
    ffp = -(-d_ff // ff_tile) * ff_tile
    m_main = nbp * seq
    m_s = nbs * dseq
    m_rest = m_s + n_meta
    dt = x_prompt.dtype

    w_in = w_ret_in[0].astype(BF16)
    w_out = w_ret_out[0].astype(BF16)
    w_kv = w_kvf[:, :2 * d].astype(BF16)
    w_f = jnp.pad(w_kvf[:, 2 * d:], ((0, 0), (0, LANE - h_b))).astype(BF16)
    b_fp = jnp.pad(b_f, (0, LANE - h_b)).reshape(1, LANE)
    w_q = w_q_b[0].astype(BF16)
    w_o = w_o_b[0].astype(BF16)
    padc = lambda a: jnp.pad(a, ((0, 0),) * (a.ndim - 1) + ((0, ffp - d_ff),))
    w_gu_p = [jnp.concatenate([padc(w_gu[l][:, :d_ff]), padc(w_gu[l][:, d_ff:])], axis=1).astype(BF16)
              for l in range(depth)]
    w_dn_p = [jnp.pad(w_down[l], ((0, ffp - d_ff), (0, 0))).astype(BF16) for l in range(depth)]
    cw_p = [jnp.pad(padc(conv_w[l]), ((0, SUBLANE - cw_taps), (0, 0))) for l in range(depth)]
    cb_p = [padc(conv_b[l]).reshape(1, ffp) for l in range(depth)]
    log_g = jnp.log1p(-jnp.exp2(-5.0 - jnp.arange(h_a, dtype=F32)))

    x_main = x_prompt.reshape(m_main, d)
    x_rest = jnp.concatenate([x_sample.reshape(m_s, d), meta.astype(dt)], axis=0)
    cos_m, sin_m = _rope_tables(n_meta + jnp.arange(seq), dk_a // 2)
    pos_rest = jnp.concatenate([jnp.tile(n_meta + past + jnp.arange(dseq), nbs), jnp.arange(n_meta)])
    cos_r, sin_r = _rope_tables(pos_rest, dk_a // 2)

    def state_rows(rows2):
        return jnp.pad(padc(rows2), ((0, 0), (SUBLANE - 2, 0), (0, 0)))

    def conv_ffn(x, l, state, rows_per_stream, row_tile):
        (hn,) = rmsnorm_rows(x, g_ffn[l:l + 1])
        gu = matmul(hn, w_gu_p[l])
        act = conv_gate(gu, state, cw_p[l], cb_p[l], rows_per_stream=rows_per_stream, row_tile=row_tile)
        return matmul_residual(act, w_dn_p[l], x), gu

    def ret_in(h, cos, sin, period):
        q = matmul_rope(h, w_in, cos, sin, col_off=0, n_out=qk_w, head_dim=dk_a, scale=1.0, period_rows=period)
        k = matmul_rope(h, w_in, cos, sin, col_off=qk_w, n_out=qk_w, head_dim=dk_a, scale=dk_a ** -0.5,
                        period_rows=period)
        v = matmul(h, w_in, col_off=2 * qk_w, n_out=v_w, out_dtype=BF16)
        gate = matmul(h, w_in, col_off=2 * qk_w + v_w, n_out=v_w)
        return q, k, v, gate

    (h_r,) = rmsnorm_rows(x_rest, g_attn[0:1])
    q_r, k_r, v_r, gate_r = ret_in(h_r, cos_r, sin_r, m_rest)
    og_s, ret_s = retention(log_g, q_r, k_r, v_r, gate_r, state_ret[0], nb=nbs, rows=dseq, chunk=dseq,
                            row_off=0, heads=h_a)
    og_m, ret_meta = retention(log_g, q_r, k_r, v_r, gate_r, jnp.zeros((1, h_a, dk_a, dv_a), F32), nb=1,
                               rows=n_meta, chunk=n_meta, row_off=m_s, heads=h_a)
    x_r1 = matmul_residual(jnp.concatenate([og_s, og_m], axis=0), w_out, x_rest)
    state_r0 = jnp.concatenate([state_rows(state_conv[0]), jnp.zeros((1, SUBLANE, ffp), F32)], axis=0)
    assert dseq % n_meta == 0 and n_meta % SUBLANE == 0
    stream_of_tile = dseq // n_meta

    def conv_ffn_rest(x, l, state):
        (hn,) = rmsnorm_rows(x, g_ffn[l:l + 1])
        gu = matmul(hn, w_gu_p[l])
        act = conv_gate(gu, state, cw_p[l], cb_p[l], rows_per_stream=dseq, row_tile=n_meta)
        return matmul_residual(act, w_dn_p[l], x), gu

    x_r2, gu_r0 = conv_ffn_rest(x_r1, 0, state_r0)

    (h_m,) = rmsnorm_rows(x_main, g_attn[0:1])
    q_m, k_m, v_m, gate_m = ret_in(h_m, cos_m, sin_m, seq)
    chunk = _divisor_tile(seq, 256, 16)
    og_main, ret_p = retention(log_g, q_m, k_m, v_m, gate_m, ret_meta, nb=nbp, rows=seq, chunk=chunk,
                               row_off=0, heads=h_a)
    x_m1 = matmul_residual(og_main, w_out, x_main)
    a_meta0 = gu_r0[m_rest - 2:, :ffp]
    state_m0 = jnp.broadcast_to(state_rows(a_meta0[None, :, :d_ff]), (nbp, SUBLANE, ffp))
    conv_tile = _divisor_tile(seq, 512, SUBLANE)
    x_m2, gu_m0 = conv_ffn(x_m1, 0, state_m0, seq, conv_tile)

    def kv_side(x, g_row):
        h, hk = rmsnorm_rows(x, jnp.stack([g_row, g_kv]))
        kk = matmul_headnorm(hk, w_kv, g_k, col_off=0, n_out=d, head_dim=hd_b, scale=1.0, out_dtype=F32)
        vv = matmul(hk, w_kv, col_off=d, n_out=d)
        lf = matmul_logsigmoid(hk, w_f, b_fp)
        q = matmul_headnorm(h, w_q, g_q[0], col_off=0, n_out=d, head_dim=hd_b, scale=hd_b ** -0.5 * LOG2E,
                            out_dtype=BF16)
        return q, kk, vv, lf

    q_r, kk_r, vv_r, lf_r = kv_side(x_r2, g_attn[1])
    q_m, kk_m, vv_m, lf_m = kv_side(x_m2, g_attn[1])

    c_cache = cumsum_rows(jnp.pad(cache_logf, ((0, 0), (0, 0), (0, LANE - h_b))), jnp.zeros((1, 1, LANE), F32))
    c_s = cumsum_rows(lf_r[:m_s].reshape(nbs, dseq, LANE), c_cache[:, past - 1:past, :])
    c_meta = cumsum_rows(lf_r[m_s:].reshape(1, n_meta, LANE), jnp.zeros((1, 1, LANE), F32))
    c_main = cumsum_rows(lf_m.reshape(nbp, seq, LANE), c_meta[:, n_meta - 1:n_meta, :])

    cq_s, ck_s = _head_major(c_s, h_b)
    _, ck_cache = _head_major(c_cache, h_b)
    o_s = fox_attention(q_r, cq_s, kk_r, vv_r, ck_s,
                        (cache_k.reshape(nbs, past, d), cache_v.reshape(nbs, past, d), ck_cache),
                        nb=nbs, rows=dseq, row_off=0, heads=h_b, q_tile=dseq)
    cq_meta, ck_meta = _head_major(c_meta, h_b)
    o_meta = fox_attention(q_r, cq_meta, kk_r, vv_r, ck_meta, None,
                           nb=1, rows=n_meta, row_off=m_s, heads=h_b, q_tile=n_meta)
    cq_main, ck_main = _head_major(c_main, h_b)
    k_meta = kk_r[m_s:]
    v_meta = vv_r[m_s:]
    o_main = fox_attention(q_m, cq_main, kk_m, vv_m, ck_main, (k_meta[None], v_meta[None], ck_meta),
                           nb=nbp, rows=seq, row_off=0, heads=h_b, q_tile=_divisor_tile(seq, 512, 16))

    x_r3 = matmul_residual(jnp.concatenate([o_s, o_meta], axis=0), w_o, x_r2)
    state_r1 = jnp.concatenate([state_rows(state_conv[1]), jnp.zeros((1, SUBLANE, ffp), F32)], axis=0)
    x_r4, gu_r1 = conv_ffn_rest(x_r3, 1, state_r1)
    x_m3 = matmul_residual(o_main, w_o, x_m2)
    a_meta1 = gu_r1[m_rest - 2:, :ffp]
    state_m1 = jnp.broadcast_to(state_rows(a_meta1[None, :, :d_ff]), (nbp, SUBLANE, ffp))
    x_m4, gu_m1 = conv_ffn(x_m3, 1, state_m1, seq, conv_tile)

    def with_meta(meta_rows, main_rows, tail):
        mr = jnp.broadcast_to(meta_rows.reshape((1, n_meta) + tail), (nbp, n_meta) + tail)
        return jnp.concatenate([mr, main_rows.reshape((nbp, seq) + tail)], axis=1)

    y_prompt = x_m4.reshape(nbp, seq, d)
    y_sample = x_r4[:m_s].reshape(nbs, dseq, d)
    k_prompt = with_meta(k_meta, kk_m, (h_b, hd_b))
    v_prompt = with_meta(v_meta, vv_m, (h_b, hd_b))
    logf_prompt = with_meta(lf_r[m_s:, :h_b], lf_m[:, :h_b], (h_b,))
    conv_prompt = jnp.stack([gu.reshape(nbp, seq, 2 * ffp)[:, seq - 2:, :d_ff] for gu in (gu_m0, gu_m1)])
    conv_sample = jnp.stack([gu[:m_s].reshape(nbs, dseq, 2 * ffp)[:, dseq - 2:, :d_ff] for gu in (gu_r0, gu_r1)])
    return (y_prompt, y_sample, k_prompt, v_prompt, logf_prompt, ret_p[None], conv_prompt,
            kk_r[:m_s].reshape(nbs, dseq, h_b, hd_b), vv_r[:m_s].reshape(nbs, dseq, h_b, hd_b),
            lf_r[:m_s, :h_b].reshape(nbs, dseq, h_b), ret_s[None], conv_sample)
```

```python
import functools

import jax
import jax.numpy as jnp
from jax import lax
from jax.experimental import pallas as pl
from jax.experimental.pallas import tpu as pltpu

EPS = 1e-6
LOG2E = 1.4426950408889634
ROPE_BASE = 10000.0
MASK_VALUE = -1e30
LANE = 128
SUBLANE = 8
VMEM_BUDGET = 44 * 2 ** 20
F32 = jnp.float32
BF16 = jnp.bfloat16


def _params(semantics, vmem_bytes):
    return pltpu.CompilerParams(dimension_semantics=semantics,
                                vmem_limit_bytes=int(min(vmem_bytes + 8 * 2 ** 20, 60 * 2 ** 20)))


def _divisor_tile(n, cap, mult):
    if n <= cap:
        return n
    t = cap - cap % mult
    while t >= mult:
        if n % t == 0:
            return t
        t -= mult
    raise ValueError(f"no tile for {n} (cap {cap}, multiple {mult})")


def _rmsnorm_kernel(x_ref, g_ref, *o_refs):
    x = x_ref[...]
    y = x * lax.rsqrt(jnp.mean(x * x, axis=-1, keepdims=True) + EPS)
    for n, o_ref in enumerate(o_refs):
        o_ref[...] = (y * g_ref[n:n + 1, :]).astype(o_ref.dtype)


def rmsnorm_rows(x, gains):
    m, d = x.shape
    g = gains.shape[0]
    tm = _divisor_tile(m, 256, 16)
    vmem = 2 * tm * d * (4 + 2 * g)
    outs = pl.pallas_call(
        _rmsnorm_kernel,
        grid=(m // tm,),
        in_specs=[pl.BlockSpec((tm, d), lambda i: (i, 0)),
                  pl.BlockSpec((g, d), lambda i: (0, 0))],
        out_specs=[pl.BlockSpec((tm, d), lambda i: (i, 0)) for _ in range(g)],
        out_shape=[jax.ShapeDtypeStruct((m, d), BF16) for _ in range(g)],
        compiler_params=_params(("parallel",), vmem),
        name="rmsnorm",
    )(x, gains)
    return tuple(outs)


def _epi_plain(acc, o_ref):
    o_ref[...] = acc.astype(o_ref.dtype)


def _epi_residual(acc, res_ref, o_ref):
    o_ref[...] = res_ref[...] + acc


def _epi_rope(acc, cos_ref, sin_ref, o_ref, *, head_dim, scale):
    half = head_dim // 2
    cos = cos_ref[...]
    sin = sin_ref[...]
    for h in range(acc.shape[1] // head_dim):
        x1 = acc[:, h * head_dim:h * head_dim + half]
        x2 = acc[:, h * head_dim + half:(h + 1) * head_dim]
        o_ref[:, h * head_dim:h * head_dim + half] = ((x1 * cos - x2 * sin) * scale).astype(o_ref.dtype)
        o_ref[:, h * head_dim + half:(h + 1) * head_dim] = ((x1 * sin + x2 * cos) * scale).astype(o_ref.dtype)


def _epi_headnorm(acc, g_ref, o_ref, *, head_dim, scale):
    g = g_ref[...]
    for h in range(acc.shape[1] // head_dim):
        x = acc[:, h * head_dim:(h + 1) * head_dim]
        y = x * lax.rsqrt(jnp.mean(x * x, axis=-1, keepdims=True) + EPS) * g
        o_ref[:, h * head_dim:(h + 1) * head_dim] = (y * scale).astype(o_ref.dtype)


def _epi_logsigmoid(acc, b_ref, o_ref):
    z = acc + b_ref[...]
    o_ref[...] = jnp.minimum(z, 0.0) - jnp.log1p(jnp.exp(-jnp.abs(z)))


def _mm_kernel(x_ref, w_ref, *rest, epilogue, nk):
    if nk == 1:
        epilogue(jnp.dot(x_ref[...], w_ref[...], preferred_element_type=F32), *rest)
        return
    acc_ref = rest[-1]
    k = pl.program_id(2)

    @pl.when(k == 0)
    def _():
        acc_ref[...] = jnp.zeros_like(acc_ref)

    acc_ref[...] += jnp.dot(x_ref[...], w_ref[...], preferred_element_type=F32)

    @pl.when(k == nk - 1)
    def _():
        epilogue(acc_ref[...], *rest[:-1])


def _mm_tiles(m, k, n, out_bytes, extra_bytes_per_elem):
    tm = _divisor_tile(m, 1024, 16)
    for tn in (1024, 512, 256, 128):
        if n % tn:
            continue
        for nk in (1, 2, 4, 8):
            if k % nk or (k // nk) % LANE:
                continue
            tk = k // nk
            ws = 2 * (tm * tk * 2 + tk * tn * 2 + tm * tn * (out_bytes + extra_bytes_per_elem))
            ws += tm * tn * 4
            if ws <= VMEM_BUDGET:
                return tm, tn, tk, ws
    raise ValueError(f"no matmul tiling for {(m, k, n)}")


def matmul(x, w, *, col_off=0, n_out=None, epilogue=_epi_plain, extras=(), extra_specs=lambda tm, tn: [],
           out_dtype=F32, extra_bytes_per_elem=0, row_tile=None):
    m, k = x.shape
    n_out = w.shape[1] - col_off if n_out is None else n_out
    tm, tn, tk, ws = _mm_tiles(m, k, n_out, jnp.dtype(out_dtype).itemsize, extra_bytes_per_elem)
    if row_tile is not None:
        tm = row_tile
    assert col_off % tn == 0 and m % tm == 0
    nk = k // tk
    joff = col_off // tn
    kern = functools.partial(_mm_kernel, epilogue=epilogue, nk=nk)
    return pl.pallas_call(
        kern,
        grid=(m // tm, n_out // tn, nk),
        in_specs=[pl.BlockSpec((tm, tk), lambda i, j, kk: (i, kk)),
                  pl.BlockSpec((tk, tn), lambda i, j, kk: (kk, j + joff)),
                  *extra_specs(tm, tn)],
        out_specs=pl.BlockSpec((tm, tn), lambda i, j, kk: (i, j)),
        out_shape=jax.ShapeDtypeStruct((m, n_out), out_dtype),
        scratch_shapes=[pltpu.VMEM((tm, tn), F32)] if nk > 1 else [],
        compiler_params=_params(("parallel", "parallel", "arbitrary"), ws),
        name="matmul_" + getattr(epilogue, "func", epilogue).__name__.lstrip("_"),
    )(x, w, *extras)


def matmul_residual(x, w, res):
    return matmul(x, w, epilogue=_epi_residual, extras=(res,), extra_bytes_per_elem=4,
                  extra_specs=lambda tm, tn: [pl.BlockSpec((tm, tn), lambda i, j, kk: (i, j))])


def matmul_rope(x, w, cos, sin, *, col_off, n_out, head_dim, scale, period_rows):
    half = head_dim // 2

    def specs(tm, tn):
        assert period_rows % tm == 0 and tn % head_dim == 0
        nper = period_rows // tm
        return [pl.BlockSpec((tm, half), lambda i, j, kk: (i % nper, 0))] * 2

    row_tile = _divisor_tile(period_rows, _divisor_tile(x.shape[0], 1024, 16), 16)
    return matmul(x, w, col_off=col_off, n_out=n_out, out_dtype=BF16, extras=(cos, sin), extra_specs=specs,
                  row_tile=row_tile,
                  epilogue=functools.partial(_epi_rope, head_dim=head_dim, scale=scale))


def matmul_headnorm(x, w, gain, *, col_off, n_out, head_dim, scale, out_dtype):
    return matmul(x, w, col_off=col_off, n_out=n_out, out_dtype=out_dtype, extras=(gain.reshape(1, head_dim),),
                  extra_specs=lambda tm, tn: [pl.BlockSpec((1, head_dim), lambda i, j, kk: (0, 0))],
                  epilogue=functools.partial(_epi_headnorm, head_dim=head_dim, scale=scale))


def matmul_logsigmoid(x, w, bias):
    return matmul(x, w, extras=(bias,), epilogue=_epi_logsigmoid,
                  extra_specs=lambda tm, tn: [pl.BlockSpec((1, tn), lambda i, j, kk: (0, j))])


def _retention_kernel(lg_ref, q_ref, k_ref, v_ref, gate_ref, s0_ref, og_ref, sout_ref, s_scr, decay_scr, *,
                      nc, hps):
    hg = pl.program_id(1)
    c = pl.program_id(2)
    t = q_ref.shape[0]
    dk = q_ref.shape[1] // hps
    dv = v_ref.shape[1] // hps
    idx = lax.broadcasted_iota(jnp.int32, (t, 1), 0).astype(F32)

    @pl.when(c == 0)
    def _():
        rel = (lax.broadcasted_iota(jnp.int32, (t, t), 0) - lax.broadcasted_iota(jnp.int32, (t, t), 1)).astype(F32)
        causal = rel >= 0.0
        for hh in range(hps):
            s_scr[hh] = s0_ref[0, hh]
            decay_scr[hh] = jnp.where(causal, jnp.exp(jnp.where(causal, rel, 0.0) * lg_ref[hg * hps + hh]), 0.0)

    for hh in range(hps):
        lg = lg_ref[hg * hps + hh]
        q = q_ref[:, hh * dk:(hh + 1) * dk]
        k = k_ref[:, hh * dk:(hh + 1) * dk]
        v = v_ref[:, hh * dv:(hh + 1) * dv]
        scores = lax.dot_general(q, k, (((1,), (1,)), ((), ())), preferred_element_type=F32) * decay_scr[hh]
        o = jnp.dot(scores.astype(BF16), v, preferred_element_type=F32)
        s_old = s_scr[hh]
        q_dec = (q.astype(F32) * jnp.exp((idx + 1.0) * lg)).astype(BF16)
        o = o + jnp.dot(q_dec, s_old.astype(BF16), preferred_element_type=F32)
        k_dec = (k.astype(F32) * jnp.exp((t - 1.0 - idx) * lg)).astype(BF16)
        kv = lax.dot_general(k_dec, v, (((0,), (0,)), ((), ())), preferred_element_type=F32)
        s_new = jnp.exp(jnp.full((1, 1), t, F32) * lg) * s_old + kv
        s_scr[hh] = s_new
        on = o * lax.rsqrt(jnp.mean(o * o, axis=-1, keepdims=True) + EPS)
        g = gate_ref[:, hh * dv:(hh + 1) * dv].astype(F32)
        og_ref[:, hh * dv:(hh + 1) * dv] = ((g * jax.nn.sigmoid(g)) * on).astype(og_ref.dtype)

        @pl.when(c == nc - 1)
        def _():
            sout_ref[0, hh] = s_new


def retention(log_g, q, k, v, gate, s0, *, nb, rows, chunk, row_off, heads):
    dk = q.shape[1] // heads
    dv = v.shape[1] // heads
    nc = rows // chunk
    hps = 2 if heads % 2 == 0 else 1
    assert rows % chunk == 0 and row_off % chunk == 0
    roff = row_off // chunk
    shared = s0.shape[0] == 1
    row_map = lambda b, h, c: (roff + b * nc + c, h)
    vmem = hps * (2 * chunk * (2 * dk * 2 + 3 * dv * 2) + 5 * dk * dv * 4 + 8 * chunk * chunk * 4)
    og, s_out = pl.pallas_call(
        functools.partial(_retention_kernel, nc=nc, hps=hps),
        grid=(nb, heads // hps, nc),
        in_specs=[pl.BlockSpec(memory_space=pltpu.SMEM),
                  pl.BlockSpec((chunk, hps * dk), row_map),
                  pl.BlockSpec((chunk, hps * dk), row_map),
                  pl.BlockSpec((chunk, hps * dv), row_map),
                  pl.BlockSpec((chunk, hps * dv), row_map),
                  pl.BlockSpec((1, hps, dk, dv), lambda b, h, c: (0 if shared else b, h, 0, 0))],
        out_specs=[pl.BlockSpec((chunk, hps * dv), lambda b, h, c: (b * nc + c, h)),
                   pl.BlockSpec((1, hps, dk, dv), lambda b, h, c: (b, h, 0, 0))],
        out_shape=[jax.ShapeDtypeStruct((nb * rows, heads * dv), BF16),
                   jax.ShapeDtypeStruct((nb, heads, dk, dv), F32)],
        scratch_shapes=[pltpu.VMEM((hps, dk, dv), F32), pltpu.VMEM((hps, chunk, chunk), F32)],
        compiler_params=_params(("parallel", "parallel", "arbitrary"), vmem),
        name="retention",
    )(log_g, q, k, v, gate, s0)
    return og, s_out


def _conv_silu_gate(a, u, halo, cw, cb):
    r1 = pltpu.roll(a, 1, axis=0)
    r2 = pltpu.roll(a, 2, axis=0)
    row = lax.broadcasted_iota(jnp.int32, (SUBLANE, a.shape[1]), 0)
    top1 = jnp.where(row == 0, halo[7:8, :], r1[:SUBLANE])
    top2 = jnp.where(row == 0, halo[6:7, :], jnp.where(row == 1, halo[7:8, :], r2[:SUBLANE]))
    prev1 = jnp.concatenate([top1, r1[SUBLANE:]], axis=0)
    prev2 = jnp.concatenate([top2, r2[SUBLANE:]], axis=0)
    c = cb + cw[0:1, :] * prev2
    c = c + cw[1:2, :] * prev1
    c = c + cw[2:3, :] * a
    return (c * jax.nn.sigmoid(c)) * u


def _conv_gate_kernel(a_ref, u_ref, halo_ref, state_ref, cw_ref, cb_ref, o_ref, *, tiles_per_stream):
    first = (pl.program_id(0) % tiles_per_stream) == 0
    halo = jnp.where(first, state_ref[0], halo_ref[...])
    o_ref[...] = _conv_silu_gate(a_ref[...], u_ref[...], halo, cw_ref[...], cb_ref[...]).astype(o_ref.dtype)


def conv_gate(gu, state, cw, cb, *, rows_per_stream, row_tile):
    m = gu.shape[0]
    f = gu.shape[1] // 2
    tr = row_tile
    tc = _divisor_tile(f, 1024, LANE)
    assert rows_per_stream % tr == 0 and tr % SUBLANE == 0
    tps = rows_per_stream // tr
    nfb = f // tc
    hb = tr // SUBLANE
    vmem = 2 * (tr * tc * (4 + 4 + 2) + 3 * 8 * tc * 4) + 6 * tr * tc * 4
    return pl.pallas_call(
        functools.partial(_conv_gate_kernel, tiles_per_stream=tps),
        grid=(m // tr, nfb),
        in_specs=[pl.BlockSpec((tr, tc), lambda i, j: (i, j)),
                  pl.BlockSpec((tr, tc), lambda i, j: (i, j + nfb)),
                  pl.BlockSpec((SUBLANE, tc), lambda i, j: (jnp.maximum(i * hb - 1, 0), j)),
                  pl.BlockSpec((1, SUBLANE, tc), lambda i, j: (i // tps, 0, j)),
                  pl.BlockSpec((SUBLANE, tc), lambda i, j: (0, j)),
                  pl.BlockSpec((1, tc), lambda i, j: (0, j))],
        out_specs=pl.BlockSpec((tr, tc), lambda i, j: (i, j)),
        out_shape=jax.ShapeDtypeStruct((m, f), BF16),
        compiler_params=_params(("parallel", "parallel"), vmem),
        name="conv_gate",
    )(gu, gu, gu, state, cw, cb)


MXU_COLS = 256


def _gate_up_conv_kernel(x_ref, wa_ref, wu_ref, state_ref, cw_ref, cb_ref, o_ref, tail_ref, halo_scr, *,
                         tiles_per_stream):
    i = pl.program_id(0)
    j = pl.program_id(1)
    first = (i % tiles_per_stream) == 0
    tm = x_ref.shape[0]
    for s in range(o_ref.shape[1] // MXU_COLS):
        cols = slice(s * MXU_COLS, (s + 1) * MXU_COLS)
        a = jnp.dot(x_ref[...], wa_ref[:, cols], preferred_element_type=F32)
        u = jnp.dot(x_ref[...], wu_ref[:, cols], preferred_element_type=F32)
        halo = jnp.where(first, state_ref[0, :, cols], halo_scr[j, :, cols])
        o_ref[:, cols] = _conv_silu_gate(a, u, halo, cw_ref[:, cols], cb_ref[:, cols]).astype(o_ref.dtype)
        tail = a[tm - SUBLANE:, :]
        halo_scr[j, :, cols] = tail
        tail_ref[0, :, cols] = tail


def matmul_conv_gate(x, w, state, cw, cb, *, rows_per_stream):
    m, k = x.shape
    f = w.shape[1] // 2
    tm = _divisor_tile(rows_per_stream, 1024, 16)
    tn = 2 * MXU_COLS
    assert f % tn == 0 and m % rows_per_stream == 0
    tps = rows_per_stream // tm
    nfb = f // tn
    vmem = 2 * (tm * k * 2 + 2 * k * tn * 2 + tm * tn * 2) + SUBLANE * f * 4 + 8 * tm * MXU_COLS * 4
    act, tails = pl.pallas_call(
        functools.partial(_gate_up_conv_kernel, tiles_per_stream=tps),
        grid=(m // tm, nfb),
        in_specs=[pl.BlockSpec((tm, k), lambda i, j: (i, 0)),
                  pl.BlockSpec((k, tn), lambda i, j: (0, j)),
                  pl.BlockSpec((k, tn), lambda i, j: (0, j + nfb)),
                  pl.BlockSpec((1, SUBLANE, tn), lambda i, j: (i // tps, 0, j)),
                  pl.BlockSpec((SUBLANE, tn), lambda i, j: (0, j)),
                  pl.BlockSpec((1, tn), lambda i, j: (0, j))],
        out_specs=[pl.BlockSpec((tm, tn), lambda i, j: (i, j)),
                   pl.BlockSpec((1, SUBLANE, tn), lambda i, j: (i, 0, j))],
        out_shape=[jax.ShapeDtypeStruct((m, f), BF16),
                   jax.ShapeDtypeStruct((m // tm, SUBLANE, f), F32)],
        scratch_shapes=[pltpu.VMEM((nfb, SUBLANE, tn), F32)],
        compiler_params=_params(("arbitrary", "arbitrary"), vmem),
        name="matmul_gate_up_conv",
    )(x, w, w, state, cw, cb)
    return act, tails[tps - 1::tps]


def _cumsum_kernel(x_ref, c0_ref, o_ref, *, blk):
    t = x_ref.shape[1]
    r = lax.broadcasted_iota(jnp.int32, (blk, blk), 0)
    s = lax.broadcasted_iota(jnp.int32, (blk, blk), 1)
    tri = (s <= r).astype(BF16)
    carry = c0_ref[0]
    for b in range(t // blk):
        x = x_ref[0, b * blk:(b + 1) * blk, :]
        x_hi = x.astype(BF16)
        r1 = x - x_hi.astype(F32)
        x_mid = r1.astype(BF16)
        x_lo = (r1 - x_mid.astype(F32)).astype(BF16)
        c = jnp.dot(tri, x_hi, preferred_element_type=F32)
        c = c + jnp.dot(tri, x_mid, preferred_element_type=F32)
        c = c + jnp.dot(tri, x_lo, preferred_element_type=F32)
        c = c + carry
        o_ref[0, b * blk:(b + 1) * blk, :] = c
        carry = c[blk - 1:blk, :]


def cumsum_rows(x, c0):
    nb, t, l = x.shape
    blk = _divisor_tile(t, 256, 16)
    shared = c0.shape[0] == 1
    return pl.pallas_call(
        functools.partial(_cumsum_kernel, blk=blk),
        grid=(nb,),
        in_specs=[pl.BlockSpec((1, t, l), lambda b: (b, 0, 0)),
                  pl.BlockSpec((1, 1, l), lambda b: (0 if shared else b, 0, 0))],
        out_specs=pl.BlockSpec((1, t, l), lambda b: (b, 0, 0)),
        out_shape=jax.ShapeDtypeStruct((nb, t, l), F32),
        compiler_params=_params(("parallel",), 4 * t * l * 4 + 2 ** 20),
        name="cumsum",
    )(x, c0)


def _fox_kernel(*refs, tq, prefix):
    m_scr, cq_scr, alpha_scr, acc_scr, sa_scr, sb_scr, pa_scr, pb_scr, kb_scr, v1_scr = refs[-10:]
    if prefix:
        q_ref, cq_ref, km_ref, vm_ref, ckm_ref, kp_ref, vp_ref, ckp_ref, o_ref = refs[:-10]
    else:
        q_ref, cq_ref, km_ref, vm_ref, ckm_ref, o_ref = refs[:-10]
    qi = pl.program_id(2)
    q = q_ref[...]
    hd = q.shape[1]
    m_scr[...] = jnp.full(m_scr.shape, MASK_VALUE, F32)
    acc_scr[...] = jnp.zeros(acc_scr.shape, F32)
    cq_scr[...] = jnp.broadcast_to(cq_ref[0], cq_scr.shape)

    def lanes(x, width):
        return jnp.tile(x, (1, width // LANE)) if width % LANE == 0 else x[:, :1]

    def ones_columns(vb):
        return jnp.concatenate([vb.astype(BF16), jnp.ones((vb.shape[0], LANE), BF16)], axis=1)

    @pl.when(qi == 0)
    def _():
        for c in range(kb_scr.shape[0] // tq):
            rows = slice(c * tq, (c + 1) * tq)
            kb_scr[rows, :] = km_ref[rows, :].astype(BF16)
            v1_scr[rows, :] = ones_columns(vm_ref[rows, :])

    def scores(s_ref, kb):
        s_ref[:, :kb.shape[0]] = lax.dot_general(q, kb, (((1,), (1,)), ((), ())), preferred_element_type=F32)

    def softmax(s_ref, p_ref, ck, causal):
        width = ck.shape[1]
        rb = min(tq, max(16, 32768 // max(width, LANE)))
        for r in range(tq // rb):
            rows = slice(r * rb, (r + 1) * rb)
            s = s_ref[rows, :width] - ck
            if causal:
                row = r * rb + lax.broadcasted_iota(jnp.int32, (rb, width), 0)
                s = jnp.where(lax.broadcasted_iota(jnp.int32, (rb, width), 1) <= row, s, MASK_VALUE)
            cq = cq_scr[rows, :]
            m_old = m_scr[rows, :]
            m_new = jnp.maximum(m_old, jnp.max(s, axis=-1, keepdims=True) + cq)
            p_ref[rows, :width] = jnp.exp2(s - lanes(m_new - cq, width)).astype(BF16)
            m_scr[rows, :] = m_new
            alpha_scr[rows, :] = jnp.exp2(m_old - m_new)

    def accumulate(p, v1, rescale):
        pv = jnp.dot(p, v1, preferred_element_type=F32)
        if rescale:
            acc_scr[...] = (acc_scr[...] + pv) * jnp.tile(alpha_scr[...], (1, 2))
        else:
            acc_scr[...] += pv

    if prefix:
        pw = kp_ref.shape[1]
        scores(sa_scr, kp_ref[0].astype(BF16))
        softmax(sa_scr, pa_scr, ckp_ref[0], False)
        accumulate(pa_scr[:, :pw], ones_columns(vp_ref[0]), False)

    def block(j):
        return pl.ds(pl.multiple_of(j * tq, tq), tq)

    scores(sa_scr, kb_scr[block(0), :])
    pb_scr[...] = jnp.zeros(pb_scr.shape, BF16)

    def pair(t, carry):
        j = 2 * t
        scores(sb_scr, kb_scr[block(j + 1), :])
        softmax(sa_scr, pa_scr, ckm_ref[0, j], False)
        accumulate(pb_scr[:, :tq], v1_scr[block(jnp.maximum(j - 1, 0)), :], True)
        scores(sa_scr, kb_scr[block(j + 2), :])
        softmax(sb_scr, pb_scr, ckm_ref[0, j + 1], False)
        accumulate(pa_scr[:, :tq], v1_scr[block(j), :], True)
        return carry

    lax.fori_loop(0, qi // 2, pair, 0)
    last = jnp.maximum(qi - 1, 0)

    @pl.when(qi % 2 == 0)
    def _():
        softmax(sa_scr, pa_scr, ckm_ref[0, qi], True)
        accumulate(pb_scr[:, :tq], v1_scr[block(last), :], True)
        accumulate(pa_scr[:, :tq], v1_scr[block(qi), :], False)

    @pl.when(qi % 2 == 1)
    def _():
        scores(sb_scr, kb_scr[block(qi), :])
        softmax(sa_scr, pa_scr, ckm_ref[0, last], False)
        accumulate(pb_scr[:, :tq], v1_scr[block(jnp.maximum(qi - 2, 0)), :], True)
        softmax(sb_scr, pb_scr, ckm_ref[0, qi], True)
        accumulate(pa_scr[:, :tq], v1_scr[block(last), :], True)
        accumulate(pb_scr[:, :tq], v1_scr[block(qi), :], False)

    o_ref[...] = (acc_scr[:, :hd] / acc_scr[:, hd:]).astype(o_ref.dtype)


def fox_attention(q, cq, km, vm, ckm, prefix, *, nb, rows, row_off, heads, q_tile):
    hd = q.shape[1] // heads
    tq = q_tile
    nq = rows // tq
    assert rows % tq == 0 and row_off % rows == 0
    qoff = row_off // tq
    boff = row_off // rows
    in_specs = [pl.BlockSpec((tq, hd), lambda b, h, i: (qoff + b * nq + i, h)),
                pl.BlockSpec((1, tq, 1), lambda b, h, i: (b * heads + h, i, 0)),
                pl.BlockSpec((rows, hd), lambda b, h, i: (boff + b, h)),
                pl.BlockSpec((rows, hd), lambda b, h, i: (boff + b, h)),
                pl.BlockSpec((1, nq, 1, tq), lambda b, h, i: (b * heads + h, 0, 0, 0))]
    args = [q, cq, km, vm, ckm.reshape(nb * heads, nq, 1, tq)]
    assert hd == LANE
    vmem = 2 * (tq * hd * 2 + tq * LANE * 4 + 2 * rows * hd * 4 + 8 * rows * 4)
    width = tq
    if prefix is not None:
        kp, vp, ckp = prefix
        p = kp.shape[1]
        width = max(tq, p)
        shared = kp.shape[0] == 1
        in_specs += [pl.BlockSpec((1, p, hd), lambda b, h, i: (0 if shared else b, 0, h)),
                     pl.BlockSpec((1, p, hd), lambda b, h, i: (0 if shared else b, 0, h)),
                     pl.BlockSpec((1, 1, p), lambda b, h, i: (h if shared else b * heads + h, 0, 0))]
        args += [kp, vp, ckp]
        vmem += 2 * (2 * p * hd * 4 + 8 * p * 4)
    width = -(-width // LANE) * LANE
    vmem += tq * width * 12 + 5 * tq * LANE * 4 + rows * hd * 6 + 2 ** 21
    return pl.pallas_call(
        functools.partial(_fox_kernel, tq=tq, prefix=prefix is not None),
        grid=(nb, heads, nq),
        in_specs=in_specs,
        out_specs=pl.BlockSpec((tq, hd), lambda b, h, i: (b * nq + i, h)),
        out_shape=jax.ShapeDtypeStruct((nb * rows, heads * hd), BF16),
        scratch_shapes=[pltpu.VMEM((tq, LANE), F32), pltpu.VMEM((tq, LANE), F32), pltpu.VMEM((tq, LANE), F32),
                        pltpu.VMEM((tq, 2 * hd), F32),
                        pltpu.VMEM((tq, width), F32), pltpu.VMEM((tq, width), F32),
                        pltpu.VMEM((tq, width), BF16), pltpu.VMEM((tq, width), BF16),
                        pltpu.VMEM((rows, hd), BF16), pltpu.VMEM((rows, 2 * hd), BF16)],
        compiler_params=_params(("parallel", "parallel", "arbitrary"), vmem),
        name="fox_attention",
    )(*args)


def _rope_tables(pos, half):
    inv = ROPE_BASE ** (-jnp.arange(half, dtype=F32) / half)
    ang = pos.astype(F32)[:, None] * inv[None, :]
    return jnp.cos(ang), jnp.sin(ang)


def _head_major(c, heads):
    nb, t, _ = c.shape
    ct = jnp.swapaxes(c[:, :, :heads] * LOG2E, 1, 2).reshape(nb * heads, t)
    return ct[:, :, None], ct[:, None, :]


def kernel(x_prompt, x_sample, cache_k, cache_v, cache_logf, state_ret, state_conv, meta, g_attn, g_ffn,
           w_ret_in, w_ret_out, g_kv, w_kvf, b_f, g_k, w_q_b, g_q, w_o_b, w_gu, conv_w, conv_b, w_down):
    nbp, seq, d = x_prompt.shape
    nbs, dseq, _ = x_sample.shape
    n_meta = meta.shape[0]
    past = cache_k.shape[1]
    depth = g_attn.shape[0]
    n_a = w_ret_in.shape[0]
    h_a, dk_a, dv_a = state_ret.shape[2:]
    h_b, hd_b = cache_k.shape[2:]
    qk_w, v_w = h_a * dk_a, h_a * dv_a
    d_ff = w_down.shape[1]
    cw_taps = conv_w.shape[1]
    assert n_a == 1 and depth == 2 and cw_taps == 3, "kernel is written for one retention + one attention layer"
    ff_tile = 1024 if d_ff >= 4096 else 256
    ffp = -(-d_ff // ff_tile) * ff_tile
    m_main = nbp * seq
    m_s = nbs * dseq
    m_sm = m_s + n_meta
    m_rest = -(-m_sm // dseq) * dseq
    dt = x_prompt.dtype

    w_in = w_ret_in[0].astype(BF16)
    w_out = w_ret_out[0].astype(BF16)
    w_kv = w_kvf[:, :2 * d].astype(BF16)
    w_f = jnp.pad(w_kvf[:, 2 * d:], ((0, 0), (0, LANE - h_b))).astype(BF16)
    b_fp = jnp.pad(b_f, (0, LANE - h_b)).reshape(1, LANE)
    w_q = w_q_b[0].astype(BF16)
    w_o = w_o_b[0].astype(BF16)
    padc = lambda a: jnp.pad(a, ((0, 0),) * (a.ndim - 1) + ((0, ffp - d_ff),))
    w_gu_p = [jnp.concatenate([padc(w_gu[l][:, :d_ff]), padc(w_gu[l][:, d_ff:])], axis=1).astype(BF16)
              for l in range(depth)]
    w_dn_p = [jnp.pad(w_down[l], ((0, ffp - d_ff), (0, 0))).astype(BF16) for l in range(depth)]
    cw_p = [jnp.pad(padc(conv_w[l]), ((0, SUBLANE - cw_taps), (0, 0))) for l in range(depth)]
    cb_p = [padc(conv_b[l]).reshape(1, ffp) for l in range(depth)]
    log_g = jnp.log1p(-jnp.exp2(-5.0 - jnp.arange(h_a, dtype=F32)))

    x_main = x_prompt.reshape(m_main, d)
    pad_rows = lambda a: jnp.pad(a, ((0, m_rest - m_sm),) + ((0, 0),) * (a.ndim - 1))
    x_rest = pad_rows(jnp.concatenate([x_sample.reshape(m_s, d), meta.astype(dt)], axis=0))
    cos_m, sin_m = _rope_tables(n_meta + jnp.arange(seq), dk_a // 2)
    pos_rest = pad_rows(jnp.concatenate([jnp.tile(n_meta + past + jnp.arange(dseq), nbs), jnp.arange(n_meta)]))
    cos_r, sin_r = _rope_tables(pos_rest, dk_a // 2)

    def state_rows(rows2):
        return jnp.pad(padc(rows2), ((0, 0), (SUBLANE - 2, 0), (0, 0)))

    def conv_ffn(x, l, state):
        (hn,) = rmsnorm_rows(x, g_ffn[l:l + 1])
        act, tails = matmul_conv_gate(hn, w_gu_p[l], state, cw_p[l], cb_p[l], rows_per_stream=seq)
        return matmul_residual(act, w_dn_p[l], x), tails[:, SUBLANE - 2:, :d_ff]

    def ret_in(h, cos, sin, period):
        q = matmul_rope(h, w_in, cos, sin, col_off=0, n_out=qk_w, head_dim=dk_a, scale=1.0, period_rows=period)
        k = matmul_rope(h, w_in, cos, sin, col_off=qk_w, n_out=qk_w, head_dim=dk_a, scale=dk_a ** -0.5,
                        period_rows=period)
        v = matmul(h, w_in, col_off=2 * qk_w, n_out=v_w, out_dtype=BF16)
        gate = matmul(h, w_in, col_off=2 * qk_w + v_w, n_out=v_w, out_dtype=BF16)
        return q, k, v, gate

    (h_r,) = rmsnorm_rows(x_rest, g_attn[0:1])
    q_r, k_r, v_r, gate_r = ret_in(h_r, cos_r, sin_r, m_rest)
    og_s, ret_s = retention(log_g, q_r, k_r, v_r, gate_r, state_ret[0], nb=nbs, rows=dseq, chunk=dseq,
                            row_off=0, heads=h_a)
    og_m, ret_meta = retention(log_g, q_r, k_r, v_r, gate_r, jnp.zeros((1, h_a, dk_a, dv_a), F32), nb=1,
                               rows=n_meta, chunk=n_meta, row_off=m_s, heads=h_a)
    x_r1 = matmul_residual(pad_rows(jnp.concatenate([og_s, og_m], axis=0)), w_out, x_rest)
    assert n_meta <= dseq and n_meta % SUBLANE == 0
    state_r0 = jnp.concatenate([state_rows(state_conv[0]), jnp.zeros((1, SUBLANE, ffp), F32)], axis=0)

    def conv_ffn_rest(x, l, state):
        (hn,) = rmsnorm_rows(x, g_ffn[l:l + 1])
        gu = matmul(hn, w_gu_p[l])
        act = conv_gate(gu, state, cw_p[l], cb_p[l], rows_per_stream=dseq, row_tile=dseq)
        return matmul_residual(act, w_dn_p[l], x), gu

    x_r2, gu_r0 = conv_ffn_rest(x_r1, 0, state_r0)

    (h_m,) = rmsnorm_rows(x_main, g_attn[0:1])
    q_m, k_m, v_m, gate_m = ret_in(h_m, cos_m, sin_m, seq)
    chunk = _divisor_tile(seq, 256, 16)
    og_main, ret_p = retention(log_g, q_m, k_m, v_m, gate_m, ret_meta, nb=nbp, rows=seq, chunk=chunk,
                               row_off=0, heads=h_a)
    x_m1 = matmul_residual(og_main, w_out, x_main)
    state_m0 = jnp.broadcast_to(state_rows(gu_r0[None, m_sm - 2:m_sm, :d_ff]), (nbp, SUBLANE, ffp))
    x_m2, conv_p0 = conv_ffn(x_m1, 0, state_m0)

    def kv_side(x, g_row):
        h, hk = rmsnorm_rows(x, jnp.stack([g_row, g_kv]))
        kk = matmul_headnorm(hk, w_kv, g_k, col_off=0, n_out=d, head_dim=hd_b, scale=1.0, out_dtype=F32)
        vv = matmul(hk, w_kv, col_off=d, n_out=d)
        lf = matmul_logsigmoid(hk, w_f, b_fp)
        q = matmul_headnorm(h, w_q, g_q[0], col_off=0, n_out=d, head_dim=hd_b, scale=hd_b ** -0.5 * LOG2E,
                            out_dtype=BF16)
        return q, kk, vv, lf

    q_r, kk_r, vv_r, lf_r = kv_side(x_r2, g_attn[1])
    q_m, kk_m, vv_m, lf_m = kv_side(x_m2, g_attn[1])

    c_cache = cumsum_rows(jnp.pad(cache_logf, ((0, 0), (0, 0), (0, LANE - h_b))), jnp.zeros((1, 1, LANE), F32))
    c_s = cumsum_rows(lf_r[:m_s].reshape(nbs, dseq, LANE), c_cache[:, past - 1:past, :])
    c_meta = cumsum_rows(lf_r[m_s:m_sm].reshape(1, n_meta, LANE), jnp.zeros((1, 1, LANE), F32))
    c_main = cumsum_rows(lf_m.reshape(nbp, seq, LANE), c_meta[:, n_meta - 1:n_meta, :])

    cq_s, ck_s = _head_major(c_s, h_b)
    _, ck_cache = _head_major(c_cache, h_b)
    o_s = fox_attention(q_r, cq_s, kk_r, vv_r, ck_s,
                        (cache_k.reshape(nbs, past, d), cache_v.reshape(nbs, past, d), ck_cache),
                        nb=nbs, rows=dseq, row_off=0, heads=h_b, q_tile=dseq)
    cq_meta, ck_meta = _head_major(c_meta, h_b)
    o_meta = fox_attention(q_r, cq_meta, kk_r, vv_r, ck_meta, None,
                           nb=1, rows=n_meta, row_off=m_s, heads=h_b, q_tile=n_meta)
    cq_main, ck_main = _head_major(c_main, h_b)
    k_meta = kk_r[m_s:m_sm]
    v_meta = vv_r[m_s:m_sm]
    o_main = fox_attention(q_m, cq_main, kk_m, vv_m, ck_main, (k_meta[None], v_meta[None], ck_meta),
                           nb=nbp, rows=seq, row_off=0, heads=h_b, q_tile=_divisor_tile(seq, 512, 16))

    x_r3 = matmul_residual(pad_rows(jnp.concatenate([o_s, o_meta], axis=0)), w_o, x_r2)
    state_r1 = jnp.concatenate([state_rows(state_conv[1]), jnp.zeros((1, SUBLANE, ffp), F32)], axis=0)
    x_r4, gu_r1 = conv_ffn_rest(x_r3, 1, state_r1)
    x_m3 = matmul_residual(o_main, w_o, x_m2)
    state_m1 = jnp.broadcast_to(state_rows(gu_r1[None, m_sm - 2:m_sm, :d_ff]), (nbp, SUBLANE, ffp))
    x_m4, conv_p1 = conv_ffn(x_m3, 1, state_m1)

    def with_meta(meta_rows, main_rows, tail):
        mr = jnp.broadcast_to(meta_rows.reshape((1, n_meta) + tail), (nbp, n_meta) + tail)
        return jnp.concatenate([mr, main_rows.reshape((nbp, seq) + tail)], axis=1)

    y_prompt = x_m4.reshape(nbp, seq, d)
    y_sample = x_r4[:m_s].reshape(nbs, dseq, d)
    k_prompt = with_meta(k_meta, kk_m, (h_b, hd_b))
    v_prompt = with_meta(v_meta, vv_m, (h_b, hd_b))
    logf_prompt = with_meta(lf_r[m_s:m_sm, :h_b], lf_m[:, :h_b], (h_b,))
    conv_prompt = jnp.stack([conv_p0, conv_p1])
    conv_sample = jnp.stack([gu[:m_s].reshape(nbs, dseq, 2 * ffp)[:, dseq - 2:, :d_ff] for gu in (gu_r0, gu_r1)])
    return (y_prompt, y_sample, k_prompt, v_prompt, logf_prompt, ret_p[None], conv_prompt,
            kk_r[:m_s].reshape(nbs, dseq, h_b, hd_b), vv_r[:m_s].reshape(nbs, dseq, h_b, hd_b),
            lf_r[:m_s, :h_b].reshape(nbs, dseq, h_b), ret_s[None], conv_sample)
```

```python
import functools

import jax
import jax.numpy as jnp
from jax import lax
from jax.experimental import pallas as pl
from jax.experimental.pallas import tpu as pltpu

EPS = 1e-6
LOG2E = 1.4426950408889634
ROPE_BASE = 10000.0
MASK_VALUE = -1e30
LANE = 128
SUBLANE = 8
VMEM_BUDGET = 44 * 2 ** 20
VMEM_TEMPORARIES = 12 * 2 ** 20
VMEM_LIMIT_CAP = 60 * 2 ** 20
MXU_COLS = 256
PV_ROWS = 512
F32 = jnp.float32
BF16 = jnp.bfloat16


def _params(semantics, vmem_bytes):
    return pltpu.CompilerParams(dimension_semantics=semantics,
                                vmem_limit_bytes=int(min(vmem_bytes + VMEM_TEMPORARIES, VMEM_LIMIT_CAP)))


def _divisor_tile(n, cap, mult):
    if n <= cap:
        return n
    t = cap - cap % mult
    while t >= mult:
        if n % t == 0:
            return t
        t -= mult
    raise ValueError(f"no tile for {n} (cap {cap}, multiple {mult})")


def _rmsnorm_kernel(x_ref, g_ref, *o_refs):
    x = x_ref[...]
    y = x * lax.rsqrt(jnp.mean(x * x, axis=-1, keepdims=True) + EPS)
    for n, o_ref in enumerate(o_refs):
        o_ref[...] = (y * g_ref[n:n + 1, :]).astype(o_ref.dtype)


def rmsnorm_rows(x, gains):
    m, d = x.shape
    g = gains.shape[0]
    tm = _divisor_tile(m, 256, 16)
    vmem = 2 * tm * d * (4 + 2 * g)
    outs = pl.pallas_call(
        _rmsnorm_kernel,
        grid=(m // tm,),
        in_specs=[pl.BlockSpec((tm, d), lambda i: (i, 0)),
                  pl.BlockSpec((g, d), lambda i: (0, 0))],
        out_specs=[pl.BlockSpec((tm, d), lambda i: (i, 0)) for _ in range(g)],
        out_shape=[jax.ShapeDtypeStruct((m, d), BF16) for _ in range(g)],
        compiler_params=_params(("parallel",), vmem),
        name="rmsnorm",
    )(x, gains)
    return tuple(outs)


def _epi_plain(acc, o_ref):
    o_ref[...] = acc.astype(o_ref.dtype)


def _epi_residual(acc, res_ref, o_ref):
    o_ref[...] = res_ref[...] + acc


def _epi_rope(acc, cos_ref, sin_ref, o_ref, *, head_dim, scale):
    half = head_dim // 2
    cos = cos_ref[...]
    sin = sin_ref[...]
    for h in range(acc.shape[1] // head_dim):
        x1 = acc[:, h * head_dim:h * head_dim + half]
        x2 = acc[:, h * head_dim + half:(h + 1) * head_dim]
        o_ref[:, h * head_dim:h * head_dim + half] = ((x1 * cos - x2 * sin) * scale).astype(o_ref.dtype)
        o_ref[:, h * head_dim + half:(h + 1) * head_dim] = ((x1 * sin + x2 * cos) * scale).astype(o_ref.dtype)


def _epi_headnorm(acc, g_ref, o_ref, *, head_dim, scale):
    g = g_ref[...]
    for h in range(acc.shape[1] // head_dim):
        x = acc[:, h * head_dim:(h + 1) * head_dim]
        y = x * lax.rsqrt(jnp.mean(x * x, axis=-1, keepdims=True) + EPS) * g
        o_ref[:, h * head_dim:(h + 1) * head_dim] = (y * scale).astype(o_ref.dtype)


def _epi_logsigmoid(acc, b_ref, o_ref):
    z = acc + b_ref[...]
    o_ref[...] = jnp.minimum(z, 0.0) - jnp.log1p(jnp.exp(-jnp.abs(z)))


def _row_rsqrt(ssq_ref, width):
    return lax.rsqrt(jnp.sum(ssq_ref[...], axis=-1, keepdims=True) * (1.0 / width) + EPS)


def _mm_kernel(x_ref, w_ref, *rest, epilogue, nk, deferred_norm):
    if deferred_norm:
        assert nk == 1
        acc = jnp.dot(x_ref[...], w_ref[...], preferred_element_type=F32)
        epilogue(acc * _row_rsqrt(rest[0], x_ref.shape[1]), *rest[1:])
        return
    if nk == 1:
        epilogue(jnp.dot(x_ref[...], w_ref[...], preferred_element_type=F32), *rest)
        return
    acc_ref = rest[-1]
    k = pl.program_id(2)

    @pl.when(k == 0)
    def _():
        acc_ref[...] = jnp.zeros_like(acc_ref)

    acc_ref[...] += jnp.dot(x_ref[...], w_ref[...], preferred_element_type=F32)

    @pl.when(k == nk - 1)
    def _():
        epilogue(acc_ref[...], *rest[:-1])


def _mm_tiles(m, k, n, out_bytes, extra_bytes_per_elem, row_bytes):
    tm = _divisor_tile(m, 1024, 16)
    for tn in (1024, 512, 256, 128):
        if n % tn:
            continue
        for nk in ((1, 2, 4, 8) if row_bytes == 0 else (1,)):
            if k % nk or (k // nk) % LANE:
                continue
            tk = k // nk
            ws = 2 * (tm * tk * 2 + tk * tn * 2 + tm * tn * (out_bytes + extra_bytes_per_elem) + tm * row_bytes)
            ws += tm * tn * 4
            if ws <= VMEM_BUDGET:
                return tm, tn, tk, ws
    raise ValueError(f"no matmul tiling for {(m, k, n)}")


def matmul(x, w, *, col_off=0, n_out=None, epilogue=_epi_plain, extras=(), extra_specs=lambda tm, tn: [],
           out_dtype=F32, extra_bytes_per_elem=0, row_tile=None, extra_outs=lambda tm, tn: [], ssq=None):
    m, k = x.shape
    n_out = w.shape[1] - col_off if n_out is None else n_out
    tm, tn, tk, ws = _mm_tiles(m, k, n_out, jnp.dtype(out_dtype).itemsize, extra_bytes_per_elem,
                               0 if ssq is None else ssq.shape[1] * 4)
    if row_tile is not None:
        tm = row_tile
    assert col_off % tn == 0 and m % tm == 0
    nk = k // tk
    joff = col_off // tn
    kern = functools.partial(_mm_kernel, epilogue=epilogue, nk=nk, deferred_norm=ssq is not None)
    ssq_args, ssq_specs = ((), []) if ssq is None else ((ssq,), [pl.BlockSpec((tm, ssq.shape[1]),
                                                                            lambda i, j, kk: (i, 0))])
    more = extra_outs(tm, tn)
    outs = pl.pallas_call(
        kern,
        grid=(m // tm, n_out // tn, nk),
        in_specs=[pl.BlockSpec((tm, tk), lambda i, j, kk: (i, kk)),
                  pl.BlockSpec((tk, tn), lambda i, j, kk: (kk, j + joff)),
                  *ssq_specs, *extra_specs(tm, tn)],
        out_specs=[pl.BlockSpec((tm, tn), lambda i, j, kk: (i, j))] + [s for _, s in more],
        out_shape=[jax.ShapeDtypeStruct((m, n_out), out_dtype)] + [s for s, _ in more],
        scratch_shapes=[pltpu.VMEM((tm, tn), F32)] if nk > 1 else [],
        compiler_params=_params(("parallel", "arbitrary" if more else "parallel", "arbitrary"), ws),
        name="matmul_" + getattr(epilogue, "func", epilogue).__name__.lstrip("_"),
    )(x, w, *ssq_args, *extras)
    return outs if more else outs[0]


def matmul_residual(x, w, res):
    return matmul(x, w, epilogue=_epi_residual, extras=(res,), extra_bytes_per_elem=4,
                  extra_specs=lambda tm, tn: [pl.BlockSpec((tm, tn), lambda i, j, kk: (i, j))])


def _epi_residual_norm(acc, res_ref, g_ref, o_ref, ssq_ref, *xg_refs):
    y = res_ref[...] + acc
    o_ref[...] = y
    sq = y * y
    part = sq[:, :LANE]
    for t in range(1, y.shape[1] // LANE):
        part = part + sq[:, t * LANE:(t + 1) * LANE]
    first_col = pl.program_id(1) == 0

    @pl.when(first_col)
    def _():
        ssq_ref[...] = part

    @pl.when(jnp.logical_not(first_col))
    def _():
        ssq_ref[...] += part

    for n, xg_ref in enumerate(xg_refs):
        xg_ref[...] = (y * g_ref[n:n + 1, :]).astype(xg_ref.dtype)


def matmul_residual_norm(x, w, res, gains):
    m, n = res.shape
    ng = gains.shape[0]

    def outs(tm, tn):
        return ([(jax.ShapeDtypeStruct((m, LANE), F32), pl.BlockSpec((tm, LANE), lambda i, j, kk: (i, 0)))]
                + [(jax.ShapeDtypeStruct((m, n), BF16), pl.BlockSpec((tm, tn), lambda i, j, kk: (i, j)))] * ng)

    y, ssq, *xg = matmul(x, w, epilogue=_epi_residual_norm, extras=(res, gains), extra_bytes_per_elem=4 + 2 * ng,
                         extra_specs=lambda tm, tn: [pl.BlockSpec((tm, tn), lambda i, j, kk: (i, j)),
                                                     pl.BlockSpec((ng, tn), lambda i, j, kk: (0, j))],
                         extra_outs=outs)
    return y, ssq, xg


def matmul_rope(x, w, cos, sin, *, col_off, n_out, head_dim, scale, period_rows):
    half = head_dim // 2

    def specs(tm, tn):
        assert period_rows % tm == 0 and tn % head_dim == 0
        nper = period_rows // tm
        return [pl.BlockSpec((tm, half), lambda i, j, kk: (i % nper, 0))] * 2

    row_tile = _divisor_tile(period_rows, _divisor_tile(x.shape[0], 1024, 16), 16)
    return matmul(x, w, col_off=col_off, n_out=n_out, out_dtype=BF16, extras=(cos, sin), extra_specs=specs,
                  row_tile=row_tile,
                  epilogue=functools.partial(_epi_rope, head_dim=head_dim, scale=scale))


def matmul_headnorm(x, w, gain, *, col_off, n_out, head_dim, scale, out_dtype, ssq=None):
    return matmul(x, w, col_off=col_off, n_out=n_out, out_dtype=out_dtype, extras=(gain.reshape(1, head_dim),),
                  extra_specs=lambda tm, tn: [pl.BlockSpec((1, head_dim), lambda i, j, kk: (0, 0))],
                  epilogue=functools.partial(_epi_headnorm, head_dim=head_dim, scale=scale), ssq=ssq)


def matmul_logsigmoid(x, w, bias, ssq=None):
    return matmul(x, w, extras=(bias,), epilogue=_epi_logsigmoid, ssq=ssq,
                  extra_specs=lambda tm, tn: [pl.BlockSpec((1, tn), lambda i, j, kk: (0, j))])


def _retention_kernel(lg_ref, q_ref, k_ref, v_ref, gate_ref, s0_ref, og_ref, sout_ref, s_scr, decay_scr, *,
                      nc, hps):
    hg = pl.program_id(1)
    c = pl.program_id(2)
    t = q_ref.shape[0]
    dk = q_ref.shape[1] // hps
    dv = v_ref.shape[1] // hps
    idx = lax.broadcasted_iota(jnp.int32, (t, 1), 0).astype(F32)

    @pl.when(c == 0)
    def _():
        rel = (lax.broadcasted_iota(jnp.int32, (t, t), 0) - lax.broadcasted_iota(jnp.int32, (t, t), 1)).astype(F32)
        causal = rel >= 0.0
        for hh in range(hps):
            s_scr[hh] = s0_ref[0, hh]
            decay_scr[hh] = jnp.where(causal, jnp.exp(jnp.where(causal, rel, 0.0) * lg_ref[hg * hps + hh]), 0.0)

    for hh in range(hps):
        lg = lg_ref[hg * hps + hh]
        q = q_ref[:, hh * dk:(hh + 1) * dk]
        k = k_ref[:, hh * dk:(hh + 1) * dk]
        v = v_ref[:, hh * dv:(hh + 1) * dv]
        scores = lax.dot_general(q, k, (((1,), (1,)), ((), ())), preferred_element_type=F32) * decay_scr[hh]
        o = jnp.dot(scores.astype(BF16), v, preferred_element_type=F32)
        s_old = s_scr[hh]
        q_dec = (q.astype(F32) * jnp.exp((idx + 1.0) * lg)).astype(BF16)
        o = o + jnp.dot(q_dec, s_old.astype(BF16), preferred_element_type=F32)
        k_dec = (k.astype(F32) * jnp.exp((t - 1.0 - idx) * lg)).astype(BF16)
        kv = lax.dot_general(k_dec, v, (((0,), (0,)), ((), ())), preferred_element_type=F32)
        s_new = jnp.exp(jnp.full((1, 1), t, F32) * lg) * s_old + kv
        s_scr[hh] = s_new
        on = o * lax.rsqrt(jnp.mean(o * o, axis=-1, keepdims=True) + EPS)
        g = gate_ref[:, hh * dv:(hh + 1) * dv].astype(F32)
        og_ref[:, hh * dv:(hh + 1) * dv] = ((g * jax.nn.sigmoid(g)) * on).astype(og_ref.dtype)

        @pl.when(c == nc - 1)
        def _():
            sout_ref[0, hh] = s_new


def retention(log_g, q, k, v, gate, s0, *, nb, rows, chunk, row_off, heads):
    dk = q.shape[1] // heads
    dv = v.shape[1] // heads
    nc = rows // chunk
    hps = 2 if heads % 2 == 0 else 1
    assert rows % chunk == 0 and row_off % chunk == 0
    roff = row_off // chunk
    shared = s0.shape[0] == 1
    row_map = lambda b, h, c: (roff + b * nc + c, h)
    vmem = hps * (2 * chunk * (2 * dk * 2 + 3 * dv * 2) + 5 * dk * dv * 4 + 8 * chunk * chunk * 4)
    og, s_out = pl.pallas_call(
        functools.partial(_retention_kernel, nc=nc, hps=hps),
        grid=(nb, heads // hps, nc),
        in_specs=[pl.BlockSpec(memory_space=pltpu.SMEM),
                  pl.BlockSpec((chunk, hps * dk), row_map),
                  pl.BlockSpec((chunk, hps * dk), row_map),
                  pl.BlockSpec((chunk, hps * dv), row_map),
                  pl.BlockSpec((chunk, hps * dv), row_map),
                  pl.BlockSpec((1, hps, dk, dv), lambda b, h, c: (0 if shared else b, h, 0, 0))],
        out_specs=[pl.BlockSpec((chunk, hps * dv), lambda b, h, c: (b * nc + c, h)),
                   pl.BlockSpec((1, hps, dk, dv), lambda b, h, c: (b, h, 0, 0))],
        out_shape=[jax.ShapeDtypeStruct((nb * rows, heads * dv), BF16),
                   jax.ShapeDtypeStruct((nb, heads, dk, dv), F32)],
        scratch_shapes=[pltpu.VMEM((hps, dk, dv), F32), pltpu.VMEM((hps, chunk, chunk), F32)],
        compiler_params=_params(("parallel", "parallel", "arbitrary"), vmem),
        name="retention",
    )(log_g, q, k, v, gate, s0)
    return og, s_out


def _conv_silu_gate(a, u, halo, cw, cb):
    r1 = pltpu.roll(a, 1, axis=0)
    r2 = pltpu.roll(a, 2, axis=0)
    row = lax.broadcasted_iota(jnp.int32, (SUBLANE, a.shape[1]), 0)
    top1 = jnp.where(row == 0, halo[7:8, :], r1[:SUBLANE])
    top2 = jnp.where(row == 0, halo[6:7, :], jnp.where(row == 1, halo[7:8, :], r2[:SUBLANE]))
    prev1 = jnp.concatenate([top1, r1[SUBLANE:]], axis=0)
    prev2 = jnp.concatenate([top2, r2[SUBLANE:]], axis=0)
    c = cb + cw[0:1, :] * prev2
    c = c + cw[1:2, :] * prev1
    c = c + cw[2:3, :] * a
    return (c * jax.nn.sigmoid(c)) * u


def _conv_gate_kernel(a_ref, u_ref, halo_ref, state_ref, cw_ref, cb_ref, o_ref, *, tiles_per_stream):
    first = (pl.program_id(0) % tiles_per_stream) == 0
    halo = jnp.where(first, state_ref[0], halo_ref[...])
    o_ref[...] = _conv_silu_gate(a_ref[...], u_ref[...], halo, cw_ref[...], cb_ref[...]).astype(o_ref.dtype)


def conv_gate(gu, state, cw, cb, *, rows_per_stream, row_tile):
    m = gu.shape[0]
    f = gu.shape[1] // 2
    tr = row_tile
    tc = _divisor_tile(f, 1024, LANE)
    assert rows_per_stream % tr == 0 and tr % SUBLANE == 0
    tps = rows_per_stream // tr
    nfb = f // tc
    hb = tr // SUBLANE
    vmem = 2 * (tr * tc * (4 + 4 + 2) + 3 * 8 * tc * 4) + 6 * tr * tc * 4
    return pl.pallas_call(
        functools.partial(_conv_gate_kernel, tiles_per_stream=tps),
        grid=(m // tr, nfb),
        in_specs=[pl.BlockSpec((tr, tc), lambda i, j: (i, j)),
                  pl.BlockSpec((tr, tc), lambda i, j: (i, j + nfb)),
                  pl.BlockSpec((SUBLANE, tc), lambda i, j: (jnp.maximum(i * hb - 1, 0), j)),
                  pl.BlockSpec((1, SUBLANE, tc), lambda i, j: (i // tps, 0, j)),
                  pl.BlockSpec((SUBLANE, tc), lambda i, j: (0, j)),
                  pl.BlockSpec((1, tc), lambda i, j: (0, j))],
        out_specs=pl.BlockSpec((tr, tc), lambda i, j: (i, j)),
        out_shape=jax.ShapeDtypeStruct((m, f), BF16),
        compiler_params=_params(("parallel", "parallel"), vmem),
        name="conv_gate",
    )(gu, gu, gu, state, cw, cb)


def _gate_up_conv_kernel(x_ref, ssq_ref, wa_ref, wu_ref, state_ref, cw_ref, cb_ref, o_ref, tail_ref, halo_scr, *,
                         tiles_per_stream):
    i = pl.program_id(0)
    j = pl.program_id(1)
    first = (i % tiles_per_stream) == 0
    tm = x_ref.shape[0]
    r = _row_rsqrt(ssq_ref, x_ref.shape[1])
    for s in range(o_ref.shape[1] // MXU_COLS):
        cols = slice(s * MXU_COLS, (s + 1) * MXU_COLS)
        a = jnp.dot(x_ref[...], wa_ref[:, cols], preferred_element_type=F32) * r
        u = jnp.dot(x_ref[...], wu_ref[:, cols], preferred_element_type=F32) * r
        halo = jnp.where(first, state_ref[0, :, cols], halo_scr[j, :, cols])
        o_ref[:, cols] = _conv_silu_gate(a, u, halo, cw_ref[:, cols], cb_ref[:, cols]).astype(o_ref.dtype)
        tail = a[tm - SUBLANE:, :]
        halo_scr[j, :, cols] = tail
        tail_ref[0, :, cols] = tail


def matmul_conv_gate(x, ssq, w, state, cw, cb, *, rows_per_stream):
    m, k = x.shape
    f = w.shape[1] // 2
    tm = _divisor_tile(rows_per_stream, 1024, 16)
    tn = 2 * MXU_COLS
    assert f % tn == 0 and m % rows_per_stream == 0
    tps = rows_per_stream // tm
    nfb = f // tn
    vmem = 2 * (tm * k * 2 + 2 * k * tn * 2 + tm * tn * 2) + SUBLANE * f * 4 + 8 * tm * MXU_COLS * 4
    act, tails = pl.pallas_call(
        functools.partial(_gate_up_conv_kernel, tiles_per_stream=tps),
        grid=(m // tm, nfb),
        in_specs=[pl.BlockSpec((tm, k), lambda i, j: (i, 0)),
                  pl.BlockSpec((tm, ssq.shape[1]), lambda i, j: (i, 0)),
                  pl.BlockSpec((k, tn), lambda i, j: (0, j)),
                  pl.BlockSpec((k, tn), lambda i, j: (0, j + nfb)),
                  pl.BlockSpec((1, SUBLANE, tn), lambda i, j: (i // tps, 0, j)),
                  pl.BlockSpec((SUBLANE, tn), lambda i, j: (0, j)),
                  pl.BlockSpec((1, tn), lambda i, j: (0, j))],
        out_specs=[pl.BlockSpec((tm, tn), lambda i, j: (i, j)),
                   pl.BlockSpec((1, SUBLANE, tn), lambda i, j: (i, 0, j))],
        out_shape=[jax.ShapeDtypeStruct((m, f), BF16),
                   jax.ShapeDtypeStruct((m // tm, SUBLANE, f), F32)],
        scratch_shapes=[pltpu.VMEM((nfb, SUBLANE, tn), F32)],
        compiler_params=_params(("arbitrary", "arbitrary"), vmem),
        name="matmul_gate_up_conv",
    )(x, ssq, w, w, state, cw, cb)
    return act, tails[tps - 1::tps]


def _cumsum_kernel(x_ref, c0_ref, o_ref, *, blk):
    t = x_ref.shape[1]
    r = lax.broadcasted_iota(jnp.int32, (blk, blk), 0)
    s = lax.broadcasted_iota(jnp.int32, (blk, blk), 1)
    tri = (s <= r).astype(BF16)
    carry = c0_ref[0]
    for b in range(t // blk):
        x = x_ref[0, b * blk:(b + 1) * blk, :]
        x_hi = x.astype(BF16)
        r1 = x - x_hi.astype(F32)
        x_mid = r1.astype(BF16)
        x_lo = (r1 - x_mid.astype(F32)).astype(BF16)
        c = jnp.dot(tri, x_hi, preferred_element_type=F32)
        c = c + jnp.dot(tri, x_mid, preferred_element_type=F32)
        c = c + jnp.dot(tri, x_lo, preferred_element_type=F32)
        c = c + carry
        o_ref[0, b * blk:(b + 1) * blk, :] = c
        carry = c[blk - 1:blk, :]


def cumsum_rows(x, c0):
    nb, t, l = x.shape
    blk = _divisor_tile(t, 256, 16)
    shared = c0.shape[0] == 1
    return pl.pallas_call(
        functools.partial(_cumsum_kernel, blk=blk),
        grid=(nb,),
        in_specs=[pl.BlockSpec((1, t, l), lambda b: (b, 0, 0)),
                  pl.BlockSpec((1, 1, l), lambda b: (0 if shared else b, 0, 0))],
        out_specs=pl.BlockSpec((1, t, l), lambda b: (b, 0, 0)),
        out_shape=jax.ShapeDtypeStruct((nb, t, l), F32),
        compiler_params=_params(("parallel",), 4 * t * l * 4 + 2 ** 20),
        name="cumsum",
    )(x, c0)


def _fox_kernel(*refs, tq, tk, prefix):
    m_scr, cq_scr, alpha_scr, acc_scr, sa_scr, sb_scr, pa_scr, pb_scr, kb_scr, v1_scr = refs[-10:]
    if prefix:
        q_ref, cq_ref, km_ref, vm_ref, ckm_ref, kp_ref, vp_ref, ckp_ref, o_ref = refs[:-10]
    else:
        q_ref, cq_ref, km_ref, vm_ref, ckm_ref, o_ref = refs[:-10]
    qi = pl.program_id(2)
    q = q_ref[...]
    hd = q.shape[1]
    m_scr[...] = jnp.full(m_scr.shape, MASK_VALUE, F32)
    acc_scr[...] = jnp.zeros(acc_scr.shape, F32)
    cq_scr[...] = jnp.broadcast_to(cq_ref[0], cq_scr.shape)

    def lanes(x, width):
        return jnp.tile(x, (1, width // LANE)) if width % LANE == 0 else x[:, :1]

    def ones_columns(vb):
        return jnp.concatenate([vb.astype(BF16), jnp.ones((vb.shape[0], LANE), BF16)], axis=1)

    @pl.when(qi == 0)
    def _():
        for c in range(kb_scr.shape[0] // tk):
            rows = slice(c * tk, (c + 1) * tk)
            kb_scr[rows, :] = km_ref[rows, :].astype(BF16)
            v1_scr[rows, :] = ones_columns(vm_ref[rows, :])

    def scores(s_ref, kb):
        rc = min(tq, PV_ROWS)
        for c in range(tq // rc):
            rows = slice(c * rc, (c + 1) * rc)
            s_ref[rows, :kb.shape[0]] = lax.dot_general(q[rows, :], kb, (((1,), (1,)), ((), ())),
                                                        preferred_element_type=F32)

    def softmax(s_ref, p_ref, ck, first_key=None):
        width = ck.shape[1]
        rb = min(tq, max(16, 32768 // max(width, LANE)))
        for r in range(tq // rb):
            rows = slice(r * rb, (r + 1) * rb)
            s = s_ref[rows, :width] - ck
            if first_key is not None:
                row = r * rb - first_key + lax.broadcasted_iota(jnp.int32, (rb, width), 0)
                s = jnp.where(lax.broadcasted_iota(jnp.int32, (rb, width), 1) <= row, s, MASK_VALUE)
            cq = cq_scr[rows, :]
            m_old = m_scr[rows, :]
            m_new = jnp.maximum(m_old, jnp.max(s, axis=-1, keepdims=True) + cq)
            p_ref[rows, :width] = jnp.exp2(s - lanes(m_new - cq, width)).astype(BF16)
            m_scr[rows, :] = m_new
            alpha_scr[rows, :] = jnp.exp2(m_old - m_new)

    def accumulate(p, v1, rescale):
        rc = min(tq, PV_ROWS)
        for c in range(tq // rc):
            rows = slice(c * rc, (c + 1) * rc)
            pv = jnp.dot(p[rows, :], v1, preferred_element_type=F32)
            if rescale:
                acc_scr[rows, :] = (acc_scr[rows, :] + pv) * jnp.tile(alpha_scr[rows, :], (1, 2))
            else:
                acc_scr[rows, :] += pv

    if prefix:
        pw = kp_ref.shape[1]
        scores(sa_scr, kp_ref[0].astype(BF16))
        softmax(sa_scr, pa_scr, ckp_ref[0])
        accumulate(pa_scr[:, :pw], ones_columns(vp_ref[0]), False)

    def block(j):
        return pl.ds(pl.multiple_of(j * tk, tk), tk)

    if tk == tq:
        scores(sa_scr, kb_scr[...])
        softmax(sa_scr, pa_scr, ckm_ref[0, 0], 0)
        acc_scr[...] = acc_scr[...] * jnp.tile(alpha_scr[...], (1, 2))
        accumulate(pa_scr[:, :tk], v1_scr[...], False)
    else:
        scores(sa_scr, kb_scr[block(0), :])
        pb_scr[...] = jnp.zeros(pb_scr.shape, BF16)

        def pair(j, first_key):
            scores(sb_scr, kb_scr[block(j + 1), :])
            softmax(sa_scr, pa_scr, ckm_ref[0, j], first_key)
            accumulate(pb_scr[:, :tk], v1_scr[block(jnp.maximum(j - 1, 0)), :], True)
            if first_key is None:
                scores(sa_scr, kb_scr[block(j + 2), :])
            softmax(sb_scr, pb_scr, ckm_ref[0, j + 1], None if first_key is None else first_key + tk)
            accumulate(pa_scr[:, :tk], v1_scr[block(j), :], True)

        def unmasked_pair(t, carry):
            pair(2 * t, None)
            return carry

        lax.fori_loop(0, qi, unmasked_pair, 0)
        pair(2 * qi, 0)
        accumulate(pb_scr[:, :tk], v1_scr[block(2 * qi + 1), :], False)

    o_ref[...] = (acc_scr[:, :hd] / acc_scr[:, hd:]).astype(o_ref.dtype)


def fox_attention(q, cq, km, vm, ckm, prefix, *, nb, rows, row_off, heads, q_tile):
    hd = q.shape[1] // heads
    tq = q_tile
    nq = rows // tq
    tk = tq // 2 if tq >= 2 * LANE else tq
    assert rows % tq == 0 and row_off % rows == 0 and (tk < tq or nq == 1)
    qoff = row_off // tq
    boff = row_off // rows
    in_specs = [pl.BlockSpec((tq, hd), lambda b, h, i: (qoff + b * nq + i, h)),
                pl.BlockSpec((1, tq, 1), lambda b, h, i: (b * heads + h, i, 0)),
                pl.BlockSpec((rows, hd), lambda b, h, i: (boff + b, h)),
                pl.BlockSpec((rows, hd), lambda b, h, i: (boff + b, h)),
                pl.BlockSpec((1, rows // tk, 1, tk), lambda b, h, i: (b * heads + h, 0, 0, 0))]
    args = [q, cq, km, vm, ckm.reshape(nb * heads, rows // tk, 1, tk)]
    assert hd == LANE
    vmem = 2 * (tq * hd * 2 + tq * LANE * 4 + 2 * rows * hd * 4 + 8 * rows * 4)
    width = tk
    if prefix is not None:
        kp, vp, ckp = prefix
        p = kp.shape[1]
        width = max(tk, p)
        shared = kp.shape[0] == 1
        in_specs += [pl.BlockSpec((1, p, hd), lambda b, h, i: (0 if shared else b, 0, h)),
                     pl.BlockSpec((1, p, hd), lambda b, h, i: (0 if shared else b, 0, h)),
                     pl.BlockSpec((1, 1, p), lambda b, h, i: (h if shared else b * heads + h, 0, 0))]
        args += [kp, vp, ckp]
        vmem += 2 * (2 * p * hd * 4 + 8 * p * 4)
    width = -(-width // LANE) * LANE
    vmem += tq * width * 12 + 5 * tq * LANE * 4 + rows * hd * 6 + 2 ** 21
    return pl.pallas_call(
        functools.partial(_fox_kernel, tq=tq, tk=tk, prefix=prefix is not None),
        grid=(nb, heads, nq),
        in_specs=in_specs,
        out_specs=pl.BlockSpec((tq, hd), lambda b, h, i: (b * nq + i, h)),
        out_shape=jax.ShapeDtypeStruct((nb * rows, heads * hd), BF16),
        scratch_shapes=[pltpu.VMEM((tq, LANE), F32), pltpu.VMEM((tq, LANE), F32), pltpu.VMEM((tq, LANE), F32),
                        pltpu.VMEM((tq, 2 * hd), F32),
                        pltpu.VMEM((tq, width), F32), pltpu.VMEM((tq, width), F32),
                        pltpu.VMEM((tq, width), BF16), pltpu.VMEM((tq, width), BF16),
                        pltpu.VMEM((rows, hd), BF16), pltpu.VMEM((rows, 2 * hd), BF16)],
        compiler_params=_params(("parallel", "parallel", "arbitrary"), vmem),
        name="fox_attention",
    )(*args)


def _rope_tables(pos, half):
    inv = ROPE_BASE ** (-jnp.arange(half, dtype=F32) / half)
    ang = pos.astype(F32)[:, None] * inv[None, :]
    return jnp.cos(ang), jnp.sin(ang)


def _head_major(c, heads):
    nb, t, _ = c.shape
    ct = jnp.swapaxes(c[:, :, :heads] * LOG2E, 1, 2).reshape(nb * heads, t)
    return ct[:, :, None], ct[:, None, :]


def kernel(x_prompt, x_sample, cache_k, cache_v, cache_logf, state_ret, state_conv, meta, g_attn, g_ffn,
           w_ret_in, w_ret_out, g_kv, w_kvf, b_f, g_k, w_q_b, g_q, w_o_b, w_gu, conv_w, conv_b, w_down):
    nbp, seq, d = x_prompt.shape
    nbs, dseq, _ = x_sample.shape
    n_meta = meta.shape[0]
    past = cache_k.shape[1]
    depth = g_attn.shape[0]
    n_a = w_ret_in.shape[0]
    h_a, dk_a, dv_a = state_ret.shape[2:]
    h_b, hd_b = cache_k.shape[2:]
    qk_w, v_w = h_a * dk_a, h_a * dv_a
    d_ff = w_down.shape[1]
    cw_taps = conv_w.shape[1]
    assert n_a == 1 and depth == 2 and cw_taps == 3, "kernel is written for one retention + one attention layer"
    ff_tile = 1024 if d_ff >= 4096 else 256
    ffp = -(-d_ff // ff_tile) * ff_tile
    m_main = nbp * seq
    m_s = nbs * dseq
    m_sm = m_s + n_meta
    m_rest = -(-m_sm // dseq) * dseq
    dt = x_prompt.dtype

    w_in = w_ret_in[0].astype(BF16)
    w_out = w_ret_out[0].astype(BF16)
    w_kv = w_kvf[:, :2 * d].astype(BF16)
    w_f = jnp.pad(w_kvf[:, 2 * d:], ((0, 0), (0, LANE - h_b))).astype(BF16)
    b_fp = jnp.pad(b_f, (0, LANE - h_b)).reshape(1, LANE)
    w_q = w_q_b[0].astype(BF16)
    w_o = w_o_b[0].astype(BF16)
    padc = lambda a: jnp.pad(a, ((0, 0),) * (a.ndim - 1) + ((0, ffp - d_ff),))
    w_gu_p = [jnp.concatenate([padc(w_gu[l][:, :d_ff]), padc(w_gu[l][:, d_ff:])], axis=1).astype(BF16)
              for l in range(depth)]
    w_dn_p = [jnp.pad(w_down[l], ((0, ffp - d_ff), (0, 0))).astype(BF16) for l in range(depth)]
    cw_p = [jnp.pad(padc(conv_w[l]), ((0, SUBLANE - cw_taps), (0, 0))) for l in range(depth)]
    cb_p = [padc(conv_b[l]).reshape(1, ffp) for l in range(depth)]
    log_g = jnp.log1p(-jnp.exp2(-5.0 - jnp.arange(h_a, dtype=F32)))

    x_main = x_prompt.reshape(m_main, d)
    pad_rows = lambda a: jnp.pad(a, ((0, m_rest - m_sm),) + ((0, 0),) * (a.ndim - 1))
    x_rest = pad_rows(jnp.concatenate([x_sample.reshape(m_s, d), meta.astype(dt)], axis=0))
    cos_m, sin_m = _rope_tables(n_meta + jnp.arange(seq), dk_a // 2)
    pos_rest = pad_rows(jnp.concatenate([jnp.tile(n_meta + past + jnp.arange(dseq), nbs), jnp.arange(n_meta)]))
    cos_r, sin_r = _rope_tables(pos_rest, dk_a // 2)

    def state_rows(rows2):
        return jnp.pad(padc(rows2), ((0, 0), (SUBLANE - 2, 0), (0, 0)))

    def conv_ffn(x, xg, ssq, l, state, main, next_gains):
        if main:
            act, tails = matmul_conv_gate(xg, ssq, w_gu_p[l], state, cw_p[l], cb_p[l], rows_per_stream=seq)
            conv_rows = tails[:, SUBLANE - 2:, :d_ff]
        else:
            conv_rows = matmul(xg, w_gu_p[l], ssq=ssq)
            act = conv_gate(conv_rows, state, cw_p[l], cb_p[l], rows_per_stream=dseq, row_tile=dseq)
        if next_gains is None:
            return matmul_residual(act, w_dn_p[l], x), None, None, conv_rows
        return (*matmul_residual_norm(act, w_dn_p[l], x, next_gains), conv_rows)

    def ret_in(h, cos, sin, period):
        q = matmul_rope(h, w_in, cos, sin, col_off=0, n_out=qk_w, head_dim=dk_a, scale=1.0, period_rows=period)
        k = matmul_rope(h, w_in, cos, sin, col_off=qk_w, n_out=qk_w, head_dim=dk_a, scale=dk_a ** -0.5,
                        period_rows=period)
        v = matmul(h, w_in, col_off=2 * qk_w, n_out=v_w, out_dtype=BF16)
        gate = matmul(h, w_in, col_off=2 * qk_w + v_w, n_out=v_w, out_dtype=BF16)
        return q, k, v, gate

    (h_r,) = rmsnorm_rows(x_rest, g_attn[0:1])
    q_r, k_r, v_r, gate_r = ret_in(h_r, cos_r, sin_r, m_rest)
    og_s, ret_s = retention(log_g, q_r, k_r, v_r, gate_r, state_ret[0], nb=nbs, rows=dseq, chunk=dseq,
                            row_off=0, heads=h_a)
    og_m, ret_meta = retention(log_g, q_r, k_r, v_r, gate_r, jnp.zeros((1, h_a, dk_a, dv_a), F32), nb=1,
                               rows=n_meta, chunk=n_meta, row_off=m_s, heads=h_a)
    x_r1, ssq_r1, (hn_r1,) = matmul_residual_norm(pad_rows(jnp.concatenate([og_s, og_m], axis=0)), w_out, x_rest,
                                                  g_ffn[0:1])
    assert n_meta <= dseq and n_meta % SUBLANE == 0
    state_r0 = jnp.concatenate([state_rows(state_conv[0]), jnp.zeros((1, SUBLANE, ffp), F32)], axis=0)
    g_l1 = jnp.stack([g_attn[1], g_kv])
    x_r2, ssq_r2, (h_r2, hk_r2), gu_r0 = conv_ffn(x_r1, hn_r1, ssq_r1, 0, state_r0, False, g_l1)

    (h_m,) = rmsnorm_rows(x_main, g_attn[0:1])
    q_m, k_m, v_m, gate_m = ret_in(h_m, cos_m, sin_m, seq)
    chunk = _divisor_tile(seq, 256, 16)
    og_main, ret_p = retention(log_g, q_m, k_m, v_m, gate_m, ret_meta, nb=nbp, rows=seq, chunk=chunk,
                               row_off=0, heads=h_a)
    x_m1, ssq_m1, (hn_m1,) = matmul_residual_norm(og_main, w_out, x_main, g_ffn[0:1])
    state_m0 = jnp.broadcast_to(state_rows(gu_r0[None, m_sm - 2:m_sm, :d_ff]), (nbp, SUBLANE, ffp))
    x_m2, ssq_m2, (h_m2, hk_m2), conv_p0 = conv_ffn(x_m1, hn_m1, ssq_m1, 0, state_m0, True, g_l1)

    def kv_side(h, hk, ssq):
        kk = matmul_headnorm(hk, w_kv, g_k, col_off=0, n_out=d, head_dim=hd_b, scale=1.0, out_dtype=F32, ssq=ssq)
        vv = matmul(hk, w_kv, col_off=d, n_out=d, ssq=ssq)
        lf = matmul_logsigmoid(hk, w_f, b_fp, ssq=ssq)
        q = matmul_headnorm(h, w_q, g_q[0], col_off=0, n_out=d, head_dim=hd_b, scale=hd_b ** -0.5 * LOG2E,
                            out_dtype=BF16, ssq=ssq)
        return q, kk, vv, lf

    q_r, kk_r, vv_r, lf_r = kv_side(h_r2, hk_r2, ssq_r2)
    q_m, kk_m, vv_m, lf_m = kv_side(h_m2, hk_m2, ssq_m2)

    c_cache = cumsum_rows(jnp.pad(cache_logf, ((0, 0), (0, 0), (0, LANE - h_b))), jnp.zeros((1, 1, LANE), F32))
    c_s = cumsum_rows(lf_r[:m_s].reshape(nbs, dseq, LANE), c_cache[:, past - 1:past, :])
    c_meta = cumsum_rows(lf_r[m_s:m_sm].reshape(1, n_meta, LANE), jnp.zeros((1, 1, LANE), F32))
    c_main = cumsum_rows(lf_m.reshape(nbp, seq, LANE), c_meta[:, n_meta - 1:n_meta, :])

    cq_s, ck_s = _head_major(c_s, h_b)
    _, ck_cache = _head_major(c_cache, h_b)
    o_s = fox_attention(q_r, cq_s, kk_r, vv_r, ck_s,
                        (cache_k.reshape(nbs, past, d), cache_v.reshape(nbs, past, d), ck_cache),
                        nb=nbs, rows=dseq, row_off=0, heads=h_b, q_tile=dseq)
    cq_meta, ck_meta = _head_major(c_meta, h_b)
    o_meta = fox_attention(q_r, cq_meta, kk_r, vv_r, ck_meta, None,
                           nb=1, rows=n_meta, row_off=m_s, heads=h_b, q_tile=n_meta)
    cq_main, ck_main = _head_major(c_main, h_b)
    k_meta = kk_r[m_s:m_sm]
    v_meta = vv_r[m_s:m_sm]
    o_main = fox_attention(q_m, cq_main, kk_m, vv_m, ck_main, (k_meta[None], v_meta[None], ck_meta),
                           nb=nbp, rows=seq, row_off=0, heads=h_b, q_tile=_divisor_tile(seq, 512, 2 * LANE))

    x_r3, ssq_r3, (hn_r3,) = matmul_residual_norm(pad_rows(jnp.concatenate([o_s, o_meta], axis=0)), w_o, x_r2,
                                                  g_ffn[1:2])
    state_r1 = jnp.concatenate([state_rows(state_conv[1]), jnp.zeros((1, SUBLANE, ffp), F32)], axis=0)
    x_r4, _, _, gu_r1 = conv_ffn(x_r3, hn_r3, ssq_r3, 1, state_r1, False, None)
    x_m3, ssq_m3, (hn_m3,) = matmul_residual_norm(o_main, w_o, x_m2, g_ffn[1:2])
    state_m1 = jnp.broadcast_to(state_rows(gu_r1[None, m_sm - 2:m_sm, :d_ff]), (nbp, SUBLANE, ffp))
    x_m4, _, _, conv_p1 = conv_ffn(x_m3, hn_m3, ssq_m3, 1, state_m1, True, None)

    def with_meta(meta_rows, main_rows, tail):
        mr = jnp.broadcast_to(meta_rows.reshape((1, n_meta) + tail), (nbp, n_meta) + tail)
        return jnp.concatenate([mr, main_rows.reshape((nbp, seq) + tail)], axis=1)

    y_prompt = x_m4.reshape(nbp, seq, d)
    y_sample = x_r4[:m_s].reshape(nbs, dseq, d)
    k_prompt = with_meta(k_meta, kk_m, (h_b, hd_b))
    v_prompt = with_meta(v_meta, vv_m, (h_b, hd_b))
    logf_prompt = with_meta(lf_r[m_s:m_sm, :h_b], lf_m[:, :h_b], (h_b,))
    conv_prompt = jnp.stack([conv_p0, conv_p1])
    conv_sample = jnp.stack([gu[:m_s].reshape(nbs, dseq, 2 * ffp)[:, dseq - 2:, :d_ff] for gu in (gu_r0, gu_r1)])
    return (y_prompt, y_sample, k_prompt, v_prompt, logf_prompt, ret_p[None], conv_prompt,
            kk_r[:m_s].reshape(nbs, dseq, h_b, hd_b), vv_r[:m_s].reshape(nbs, dseq, h_b, hd_b),
            lf_r[:m_s, :h_b].reshape(nbs, dseq, h_b), ret_s[None], conv_sample)
```

```python
import functools

import jax
import jax.numpy as jnp
from jax import lax
from jax.experimental import pallas as pl
from jax.experimental.pallas import tpu as pltpu

EPS = 1e-6
LOG2E = 1.4426950408889634
ROPE_BASE = 10000.0
MASK_VALUE = -1e30
LANE = 128
SUBLANE = 8
MXU_COLS = 256
VMEM_BUDGET = 44 * 2 ** 20
F32 = jnp.float32
BF16 = jnp.bfloat16


def _params(semantics, vmem_bytes):
    return pltpu.CompilerParams(dimension_semantics=semantics,
                                vmem_limit_bytes=int(min(vmem_bytes + 8 * 2 ** 20, 60 * 2 ** 20)))


def _divisor_tile(n, cap, mult):
    if n <= cap:
        return n
    t = cap - cap % mult
    while t >= mult:
        if n % t == 0:
            return t
        t -= mult
    raise ValueError(f"no tile for {n} (cap {cap}, multiple {mult})")


def _rmsnorm_kernel(x_ref, g_ref, *o_refs):
    x = x_ref[...]
    y = x * lax.rsqrt(jnp.mean(x * x, axis=-1, keepdims=True) + EPS)
    for n, o_ref in enumerate(o_refs):
        o_ref[...] = (y * g_ref[n:n + 1, :]).astype(o_ref.dtype)


def rmsnorm_rows(x, gains):
    m, d = x.shape
    g = gains.shape[0]
    tm = _divisor_tile(m, 256, 16)
    vmem = 2 * tm * d * (4 + 2 * g)
    outs = pl.pallas_call(
        _rmsnorm_kernel,
        grid=(m // tm,),
        in_specs=[pl.BlockSpec((tm, d), lambda i: (i, 0)),
                  pl.BlockSpec((g, d), lambda i: (0, 0))],
        out_specs=[pl.BlockSpec((tm, d), lambda i: (i, 0)) for _ in range(g)],
        out_shape=[jax.ShapeDtypeStruct((m, d), BF16) for _ in range(g)],
        compiler_params=_params(("parallel",), vmem),
        name="rmsnorm",
    )(x, gains)
    return tuple(outs)


def _epi_plain(acc, o_ref):
    o_ref[...] = acc.astype(o_ref.dtype)


def _epi_residual(acc, res_ref, o_ref):
    o_ref[...] = res_ref[...] + acc


def _epi_rope(acc, cos_ref, sin_ref, o_ref, *, head_dim, scale):
    half = head_dim // 2
    cos = cos_ref[...]
    sin = sin_ref[...]
    for h in range(acc.shape[1] // head_dim):
        x1 = acc[:, h * head_dim:h * head_dim + half]
        x2 = acc[:, h * head_dim + half:(h + 1) * head_dim]
        o_ref[:, h * head_dim:h * head_dim + half] = ((x1 * cos - x2 * sin) * scale).astype(o_ref.dtype)
        o_ref[:, h * head_dim + half:(h + 1) * head_dim] = ((x1 * sin + x2 * cos) * scale).astype(o_ref.dtype)


def _epi_headnorm(acc, g_ref, o_ref, *, head_dim, scale):
    g = g_ref[...]
    for h in range(acc.shape[1] // head_dim):
        x = acc[:, h * head_dim:(h + 1) * head_dim]
        y = x * lax.rsqrt(jnp.mean(x * x, axis=-1, keepdims=True) + EPS) * g
        o_ref[:, h * head_dim:(h + 1) * head_dim] = (y * scale).astype(o_ref.dtype)


def _epi_logsigmoid(acc, b_ref, o_ref):
    z = acc + b_ref[...]
    o_ref[...] = jnp.minimum(z, 0.0) - jnp.log1p(jnp.exp(-jnp.abs(z)))


def _mm_kernel(x_ref, w_ref, *rest, epilogue, nk):
    if nk == 1:
        epilogue(jnp.dot(x_ref[...], w_ref[...], preferred_element_type=F32), *rest)
        return
    acc_ref = rest[-1]
    k = pl.program_id(2)

    @pl.when(k == 0)
    def _():
        acc_ref[...] = jnp.zeros_like(acc_ref)

    acc_ref[...] += jnp.dot(x_ref[...], w_ref[...], preferred_element_type=F32)

    @pl.when(k == nk - 1)
    def _():
        epilogue(acc_ref[...], *rest[:-1])


def _mm_tiles(m, k, n, out_bytes, extra_bytes_per_elem):
    tm = _divisor_tile(m, 1024, 16)
    for tn in (1024, 512, 256, 128):
        if n % tn:
            continue
        for nk in (1, 2, 4, 8):
            if k % nk or (k // nk) % LANE:
                continue
            tk = k // nk
            ws = 2 * (tm * tk * 2 + tk * tn * 2 + tm * tn * (out_bytes + extra_bytes_per_elem))
            ws += tm * tn * 4
            if ws <= VMEM_BUDGET:
                return tm, tn, tk, ws
    raise ValueError(f"no matmul tiling for {(m, k, n)}")


def matmul(x, w, *, col_off=0, n_out=None, epilogue=_epi_plain, extras=(), extra_specs=lambda tm, tn: [],
           out_dtype=F32, extra_bytes_per_elem=0, row_tile=None):
    m, k = x.shape
    n_out = w.shape[1] - col_off if n_out is None else n_out
    tm, tn, tk, ws = _mm_tiles(m, k, n_out, jnp.dtype(out_dtype).itemsize, extra_bytes_per_elem)
    if row_tile is not None:
        tm = row_tile
    assert col_off % tn == 0 and m % tm == 0
    nk = k // tk
    joff = col_off // tn
    kern = functools.partial(_mm_kernel, epilogue=epilogue, nk=nk)
    return pl.pallas_call(
        kern,
        grid=(m // tm, n_out // tn, nk),
        in_specs=[pl.BlockSpec((tm, tk), lambda i, j, kk: (i, kk)),
                  pl.BlockSpec((tk, tn), lambda i, j, kk: (kk, j + joff)),
                  *extra_specs(tm, tn)],
        out_specs=pl.BlockSpec((tm, tn), lambda i, j, kk: (i, j)),
        out_shape=jax.ShapeDtypeStruct((m, n_out), out_dtype),
        scratch_shapes=[pltpu.VMEM((tm, tn), F32)] if nk > 1 else [],
        compiler_params=_params(("parallel", "parallel", "arbitrary"), ws),
        name="matmul_" + getattr(epilogue, "func", epilogue).__name__.lstrip("_"),
    )(x, w, *extras)


def matmul_residual(x, w, res):
    return matmul(x, w, epilogue=_epi_residual, extras=(res,), extra_bytes_per_elem=4,
                  extra_specs=lambda tm, tn: [pl.BlockSpec((tm, tn), lambda i, j, kk: (i, j))])


def matmul_rope(x, w, cos, sin, *, col_off, n_out, head_dim, scale, period_rows):
    half = head_dim // 2

    def specs(tm, tn):
        assert period_rows % tm == 0 and tn % head_dim == 0
        nper = period_rows // tm
        return [pl.BlockSpec((tm, half), lambda i, j, kk: (i % nper, 0))] * 2

    row_tile = _divisor_tile(period_rows, _divisor_tile(x.shape[0], 1024, 16), 16)
    return matmul(x, w, col_off=col_off, n_out=n_out, out_dtype=BF16, extras=(cos, sin), extra_specs=specs,
                  row_tile=row_tile,
                  epilogue=functools.partial(_epi_rope, head_dim=head_dim, scale=scale))


def matmul_headnorm(x, w, gain, *, col_off, n_out, head_dim, scale, out_dtype):
    return matmul(x, w, col_off=col_off, n_out=n_out, out_dtype=out_dtype, extras=(gain.reshape(1, head_dim),),
                  extra_specs=lambda tm, tn: [pl.BlockSpec((1, head_dim), lambda i, j, kk: (0, 0))],
                  epilogue=functools.partial(_epi_headnorm, head_dim=head_dim, scale=scale))


def matmul_logsigmoid(x, w, bias):
    return matmul(x, w, extras=(bias,), epilogue=_epi_logsigmoid,
                  extra_specs=lambda tm, tn: [pl.BlockSpec((1, tn), lambda i, j, kk: (0, j))])


def _retention_kernel(lg_ref, q_ref, k_ref, v_ref, gate_ref, s0_ref, og_ref, sout_ref, s_scr, decay_scr, *,
                      nc, hps):
    hg = pl.program_id(1)
    c = pl.program_id(2)
    t = q_ref.shape[0]
    dk = q_ref.shape[1] // hps
    dv = v_ref.shape[1] // hps
    idx = lax.broadcasted_iota(jnp.int32, (t, 1), 0).astype(F32)

    @pl.when(c == 0)
    def _():
        rel = (lax.broadcasted_iota(jnp.int32, (t, t), 0) - lax.broadcasted_iota(jnp.int32, (t, t), 1)).astype(F32)
        causal = rel >= 0.0
        for hh in range(hps):
            s_scr[hh] = s0_ref[0, hh]
            decay_scr[hh] = jnp.where(causal, jnp.exp(jnp.where(causal, rel, 0.0) * lg_ref[hg * hps + hh]), 0.0)

    for hh in range(hps):
        lg = lg_ref[hg * hps + hh]
        q = q_ref[:, hh * dk:(hh + 1) * dk]
        k = k_ref[:, hh * dk:(hh + 1) * dk]
        v = v_ref[:, hh * dv:(hh + 1) * dv]
        scores = lax.dot_general(q, k, (((1,), (1,)), ((), ())), preferred_element_type=F32) * decay_scr[hh]
        o = jnp.dot(scores.astype(BF16), v, preferred_element_type=F32)
        s_old = s_scr[hh]
        q_dec = (q.astype(F32) * jnp.exp((idx + 1.0) * lg)).astype(BF16)
        o = o + jnp.dot(q_dec, s_old.astype(BF16), preferred_element_type=F32)
        k_dec = (k.astype(F32) * jnp.exp((t - 1.0 - idx) * lg)).astype(BF16)
        kv = lax.dot_general(k_dec, v, (((0,), (0,)), ((), ())), preferred_element_type=F32)
        s_new = jnp.exp(jnp.full((1, 1), t, F32) * lg) * s_old + kv
        s_scr[hh] = s_new
        on = o * lax.rsqrt(jnp.mean(o * o, axis=-1, keepdims=True) + EPS)
        g = gate_ref[:, hh * dv:(hh + 1) * dv].astype(F32)
        og_ref[:, hh * dv:(hh + 1) * dv] = ((g * jax.nn.sigmoid(g)) * on).astype(og_ref.dtype)

        @pl.when(c == nc - 1)
        def _():
            sout_ref[0, hh] = s_new


def retention(log_g, q, k, v, gate, s0, *, nb, rows, chunk, row_off, heads):
    dk = q.shape[1] // heads
    dv = v.shape[1] // heads
    nc = rows // chunk
    hps = 2 if heads % 2 == 0 else 1
    assert rows % chunk == 0 and row_off % chunk == 0
    roff = row_off // chunk
    shared = s0.shape[0] == 1
    row_map = lambda b, h, c: (roff + b * nc + c, h)
    vmem = hps * (2 * chunk * (2 * dk * 2 + 3 * dv * 2) + 5 * dk * dv * 4 + 8 * chunk * chunk * 4)
    og, s_out = pl.pallas_call(
        functools.partial(_retention_kernel, nc=nc, hps=hps),
        grid=(nb, heads // hps, nc),
        in_specs=[pl.BlockSpec(memory_space=pltpu.SMEM),
                  pl.BlockSpec((chunk, hps * dk), row_map),
                  pl.BlockSpec((chunk, hps * dk), row_map),
                  pl.BlockSpec((chunk, hps * dv), row_map),
                  pl.BlockSpec((chunk, hps * dv), row_map),
                  pl.BlockSpec((1, hps, dk, dv), lambda b, h, c: (0 if shared else b, h, 0, 0))],
        out_specs=[pl.BlockSpec((chunk, hps * dv), lambda b, h, c: (b * nc + c, h)),
                   pl.BlockSpec((1, hps, dk, dv), lambda b, h, c: (b, h, 0, 0))],
        out_shape=[jax.ShapeDtypeStruct((nb * rows, heads * dv), BF16),
                   jax.ShapeDtypeStruct((nb, heads, dk, dv), F32)],
        scratch_shapes=[pltpu.VMEM((hps, dk, dv), F32), pltpu.VMEM((hps, chunk, chunk), F32)],
        compiler_params=_params(("parallel", "parallel", "arbitrary"), vmem),
        name="retention",
    )(log_g, q, k, v, gate, s0)
    return og, s_out


def _conv_silu_gate(a, u, halo, cw, cb):
    r1 = pltpu.roll(a, 1, axis=0)
    r2 = pltpu.roll(a, 2, axis=0)
    row = lax.broadcasted_iota(jnp.int32, (SUBLANE, a.shape[1]), 0)
    top1 = jnp.where(row == 0, halo[7:8, :], r1[:SUBLANE])
    top2 = jnp.where(row == 0, halo[6:7, :], jnp.where(row == 1, halo[7:8, :], r2[:SUBLANE]))
    prev1 = jnp.concatenate([top1, r1[SUBLANE:]], axis=0)
    prev2 = jnp.concatenate([top2, r2[SUBLANE:]], axis=0)
    c = cb + cw[0:1, :] * prev2
    c = c + cw[1:2, :] * prev1
    c = c + cw[2:3, :] * a
    return (c * jax.nn.sigmoid(c)) * u


def _conv_gate_kernel(a_ref, u_ref, halo_ref, state_ref, cw_ref, cb_ref, o_ref, *, tiles_per_stream):
    first = (pl.program_id(0) % tiles_per_stream) == 0
    halo = jnp.where(first, state_ref[0], halo_ref[...])
    o_ref[...] = _conv_silu_gate(a_ref[...], u_ref[...], halo, cw_ref[...], cb_ref[...]).astype(o_ref.dtype)


def conv_gate(gu, state, cw, cb, *, rows_per_stream, row_tile):
    m = gu.shape[0]
    f = gu.shape[1] // 2
    tr = row_tile
    tc = _divisor_tile(f, 1024, LANE)
    assert rows_per_stream % tr == 0 and tr % SUBLANE == 0
    tps = rows_per_stream // tr
    nfb = f // tc
    hb = tr // SUBLANE
    vmem = 2 * (tr * tc * (4 + 4 + 2) + 3 * 8 * tc * 4) + 6 * tr * tc * 4
    return pl.pallas_call(
        functools.partial(_conv_gate_kernel, tiles_per_stream=tps),
        grid=(m // tr, nfb),
        in_specs=[pl.BlockSpec((tr, tc), lambda i, j: (i, j)),
                  pl.BlockSpec((tr, tc), lambda i, j: (i, j + nfb)),
                  pl.BlockSpec((SUBLANE, tc), lambda i, j: (jnp.maximum(i * hb - 1, 0), j)),
                  pl.BlockSpec((1, SUBLANE, tc), lambda i, j: (i // tps, 0, j)),
                  pl.BlockSpec((SUBLANE, tc), lambda i, j: (0, j)),
                  pl.BlockSpec((1, tc), lambda i, j: (0, j))],
        out_specs=pl.BlockSpec((tr, tc), lambda i, j: (i, j)),
        out_shape=jax.ShapeDtypeStruct((m, f), BF16),
        compiler_params=_params(("parallel", "parallel"), vmem),
        name="conv_gate",
    )(gu, gu, gu, state, cw, cb)


def _gate_up_conv_kernel(x_ref, wa_ref, wu_ref, state_ref, cw_ref, cb_ref, o_ref, tail_ref, halo_scr, *,
                         tiles_per_stream):
    i = pl.program_id(0)
    j = pl.program_id(1)
    first = (i % tiles_per_stream) == 0
    tm = x_ref.shape[0]
    for s in range(o_ref.shape[1] // MXU_COLS):
        cols = slice(s * MXU_COLS, (s + 1) * MXU_COLS)
        a = jnp.dot(x_ref[...], wa_ref[:, cols], preferred_element_type=F32)
        u = jnp.dot(x_ref[...], wu_ref[:, cols], preferred_element_type=F32)
        halo = jnp.where(first, state_ref[0, :, cols], halo_scr[j, :, cols])
        o_ref[:, cols] = _conv_silu_gate(a, u, halo, cw_ref[:, cols], cb_ref[:, cols]).astype(o_ref.dtype)
        tail = a[tm - SUBLANE:, :]
        halo_scr[j, :, cols] = tail
        tail_ref[0, :, cols] = tail


def matmul_conv_gate(x, w, state, cw, cb, *, rows_per_stream):
    m, k = x.shape
    f = w.shape[1] // 2
    tm = _divisor_tile(rows_per_stream, 1024, 16)
    tn = 2 * MXU_COLS
    assert f % tn == 0 and m % rows_per_stream == 0
    tps = rows_per_stream // tm
    nfb = f // tn
    vmem = 2 * (tm * k * 2 + 2 * k * tn * 2 + tm * tn * 2) + SUBLANE * f * 4 + 8 * tm * MXU_COLS * 4
    act, tails = pl.pallas_call(
        functools.partial(_gate_up_conv_kernel, tiles_per_stream=tps),
        grid=(m // tm, nfb),
        in_specs=[pl.BlockSpec((tm, k), lambda i, j: (i, 0)),
                  pl.BlockSpec((k, tn), lambda i, j: (0, j)),
                  pl.BlockSpec((k, tn), lambda i, j: (0, j + nfb)),
                  pl.BlockSpec((1, SUBLANE, tn), lambda i, j: (i // tps, 0, j)),
                  pl.BlockSpec((SUBLANE, tn), lambda i, j: (0, j)),
                  pl.BlockSpec((1, tn), lambda i, j: (0, j))],
        out_specs=[pl.BlockSpec((tm, tn), lambda i, j: (i, j)),
                   pl.BlockSpec((1, SUBLANE, tn), lambda i, j: (i, 0, j))],
        out_shape=[jax.ShapeDtypeStruct((m, f), BF16),
                   jax.ShapeDtypeStruct((m // tm, SUBLANE, f), F32)],
        scratch_shapes=[pltpu.VMEM((nfb, SUBLANE, tn), F32)],
        compiler_params=_params(("arbitrary", "arbitrary"), vmem),
        name="matmul_gate_up_conv",
    )(x, w, w, state, cw, cb)
    return act, tails[tps - 1::tps]


def _cumsum_kernel(x_ref, c0_ref, o_ref, *, blk):
    t = x_ref.shape[1]
    r = lax.broadcasted_iota(jnp.int32, (blk, blk), 0)
    s = lax.broadcasted_iota(jnp.int32, (blk, blk), 1)
    tri = (s <= r).astype(BF16)
    carry = c0_ref[0]
    for b in range(t // blk):
        x = x_ref[0, b * blk:(b + 1) * blk, :]
        x_hi = x.astype(BF16)
        r1 = x - x_hi.astype(F32)
        x_mid = r1.astype(BF16)
        x_lo = (r1 - x_mid.astype(F32)).astype(BF16)
        c = jnp.dot(tri, x_hi, preferred_element_type=F32)
        c = c + jnp.dot(tri, x_mid, preferred_element_type=F32)
        c = c + jnp.dot(tri, x_lo, preferred_element_type=F32)
        c = c + carry
        o_ref[0, b * blk:(b + 1) * blk, :] = c
        carry = c[blk - 1:blk, :]


def cumsum_rows(x, c0):
    nb, t, l = x.shape
    blk = _divisor_tile(t, 256, 16)
    shared = c0.shape[0] == 1
    return pl.pallas_call(
        functools.partial(_cumsum_kernel, blk=blk),
        grid=(nb,),
        in_specs=[pl.BlockSpec((1, t, l), lambda b: (b, 0, 0)),
                  pl.BlockSpec((1, 1, l), lambda b: (0 if shared else b, 0, 0))],
        out_specs=pl.BlockSpec((1, t, l), lambda b: (b, 0, 0)),
        out_shape=jax.ShapeDtypeStruct((nb, t, l), F32),
        compiler_params=_params(("parallel",), 4 * t * l * 4 + 2 ** 20),
        name="cumsum",
    )(x, c0)


def _fox_kernel(*refs, tq, prefix):
    m_scr, cq_scr, alpha_scr, acc_scr, sa_scr, sb_scr, pa_scr, pb_scr, kb_scr, v1_scr = refs[-10:]
    if prefix:
        q_ref, cq_ref, km_ref, vm_ref, ckm_ref, kp_ref, vp_ref, ckp_ref, o_ref = refs[:-10]
    else:
        q_ref, cq_ref, km_ref, vm_ref, ckm_ref, o_ref = refs[:-10]
    hd = q_ref.shape[1]
    nq = q_ref.shape[0] // tq

    def lanes(x, width):
        return jnp.tile(x, (1, width // LANE)) if width % LANE == 0 else x[:, :1]

    def ones_columns(vb):
        return jnp.concatenate([vb.astype(BF16), jnp.ones((vb.shape[0], LANE), BF16)], axis=1)

    for c in range(nq):
        rows = slice(c * tq, (c + 1) * tq)
        kb_scr[rows, :] = km_ref[rows, :].astype(BF16)
        v1_scr[rows, :] = ones_columns(vm_ref[rows, :])

    def block(j):
        return pl.ds(pl.multiple_of(j * tq, tq), tq)

    def query_block(qi, carry):
        q = q_ref[block(qi), :]
        m_scr[...] = jnp.full(m_scr.shape, MASK_VALUE, F32)
        acc_scr[...] = jnp.zeros(acc_scr.shape, F32)
        cq_scr[...] = jnp.broadcast_to(cq_ref[0, block(qi), :], cq_scr.shape)

        def scores(s_ref, kb):
            s_ref[:, :kb.shape[0]] = lax.dot_general(q, kb, (((1,), (1,)), ((), ())), preferred_element_type=F32)

        def softmax(s_ref, p_ref, ck, causal):
            width = ck.shape[1]
            rb = min(tq, max(16, 32768 // max(width, LANE)))
            for r in range(tq // rb):
                rows = slice(r * rb, (r + 1) * rb)
                s = s_ref[rows, :width] - ck
                if causal:
                    row = r * rb + lax.broadcasted_iota(jnp.int32, (rb, width), 0)
                    s = jnp.where(lax.broadcasted_iota(jnp.int32, (rb, width), 1) <= row, s, MASK_VALUE)
                cq = cq_scr[rows, :]
                m_old = m_scr[rows, :]
                m_new = jnp.maximum(m_old, jnp.max(s, axis=-1, keepdims=True) + cq)
                p_ref[rows, :width] = jnp.exp2(s - lanes(m_new - cq, width)).astype(BF16)
                m_scr[rows, :] = m_new
                alpha_scr[rows, :] = jnp.exp2(m_old - m_new)

        def accumulate(p, v1, rescale):
            pv = jnp.dot(p, v1, preferred_element_type=F32)
            if rescale:
                acc_scr[...] = (acc_scr[...] + pv) * jnp.tile(alpha_scr[...], (1, 2))
            else:
                acc_scr[...] += pv

        if prefix:
            pw = kp_ref.shape[1]
            scores(sa_scr, kp_ref[0].astype(BF16))
            softmax(sa_scr, pa_scr, ckp_ref[0], False)
            accumulate(pa_scr[:, :pw], ones_columns(vp_ref[0]), False)

        scores(sa_scr, kb_scr[block(0), :])
        pb_scr[...] = jnp.zeros(pb_scr.shape, BF16)

        def pair(t, carry):
            j = 2 * t
            scores(sb_scr, kb_scr[block(j + 1), :])
            softmax(sa_scr, pa_scr, ckm_ref[0, j], False)
            accumulate(pb_scr[:, :tq], v1_scr[block(jnp.maximum(j - 1, 0)), :], True)
            scores(sa_scr, kb_scr[block(j + 2), :])
            softmax(sb_scr, pb_scr, ckm_ref[0, j + 1], False)
            accumulate(pa_scr[:, :tq], v1_scr[block(j), :], True)
            return carry

        lax.fori_loop(0, qi // 2, pair, 0)
        last = jnp.maximum(qi - 1, 0)

        @pl.when(qi % 2 == 0)
        def _():
            softmax(sa_scr, pa_scr, ckm_ref[0, qi], True)
            accumulate(pb_scr[:, :tq], v1_scr[block(last), :], True)
            accumulate(pa_scr[:, :tq], v1_scr[block(qi), :], False)

        @pl.when(qi % 2 == 1)
        def _():
            scores(sb_scr, kb_scr[block(qi), :])
            softmax(sa_scr, pa_scr, ckm_ref[0, last], False)
            accumulate(pb_scr[:, :tq], v1_scr[block(jnp.maximum(qi - 2, 0)), :], True)
            softmax(sb_scr, pb_scr, ckm_ref[0, qi], True)
            accumulate(pa_scr[:, :tq], v1_scr[block(last), :], True)
            accumulate(pb_scr[:, :tq], v1_scr[block(qi), :], False)

        o_ref[block(qi), :] = (acc_scr[:, :hd] / acc_scr[:, hd:]).astype(o_ref.dtype)
        return carry

    lax.fori_loop(0, nq, query_block, 0)


def fox_attention(q, cq, km, vm, ckm, prefix, *, nb, rows, row_off, heads, q_tile):
    hd = q.shape[1] // heads
    tq = q_tile
    nq = rows // tq
    assert rows % tq == 0 and row_off % rows == 0 and hd == LANE
    boff = row_off // rows
    in_specs = [pl.BlockSpec((rows, hd), lambda b, h: (boff + b, h)),
                pl.BlockSpec((1, rows, 1), lambda b, h: (b * heads + h, 0, 0)),
                pl.BlockSpec((rows, hd), lambda b, h: (boff + b, h)),
                pl.BlockSpec((rows, hd), lambda b, h: (boff + b, h)),
                pl.BlockSpec((1, nq, 1, tq), lambda b, h: (b * heads + h, 0, 0, 0))]
    args = [q, cq, km, vm, ckm.reshape(nb * heads, nq, 1, tq)]
    vmem = 2 * (2 * rows * hd * 2 + rows * LANE * 4 + 2 * rows * hd * 4 + 8 * rows * 4)
    width = tq
    if prefix is not None:
        kp, vp, ckp = prefix
        p = kp.shape[1]
        width = max(tq, p)
        shared = kp.shape[0] == 1
        in_specs += [pl.BlockSpec((1, p, hd), lambda b, h: (0 if shared else b, 0, h)),
                     pl.BlockSpec((1, p, hd), lambda b, h: (0 if shared else b, 0, h)),
                     pl.BlockSpec((1, 1, p), lambda b, h: (h if shared else b * heads + h, 0, 0))]
        args += [kp, vp, ckp]
        vmem += 2 * (2 * p * hd * 4 + 8 * p * 4)
    width = -(-width // LANE) * LANE
    vmem += tq * width * 12 + 5 * tq * LANE * 4 + rows * hd * 6 + 2 ** 21
    return pl.pallas_call(
        functools.partial(_fox_kernel, tq=tq, prefix=prefix is not None),
        grid=(nb, heads),
        in_specs=in_specs,
        out_specs=pl.BlockSpec((rows, hd), lambda b, h: (b, h)),
        out_shape=jax.ShapeDtypeStruct((nb * rows, heads * hd), BF16),
        scratch_shapes=[pltpu.VMEM((tq, LANE), F32), pltpu.VMEM((tq, LANE), F32), pltpu.VMEM((tq, LANE), F32),
                        pltpu.VMEM((tq, 2 * hd), F32),
                        pltpu.VMEM((tq, width), F32), pltpu.VMEM((tq, width), F32),
                        pltpu.VMEM((tq, width), BF16), pltpu.VMEM((tq, width), BF16),
                        pltpu.VMEM((rows, hd), BF16), pltpu.VMEM((rows, 2 * hd), BF16)],
        compiler_params=_params(("parallel", "parallel"), vmem),
        name="fox_attention",
    )(*args)


def _rope_tables(pos, half):
    inv = ROPE_BASE ** (-jnp.arange(half, dtype=F32) / half)
    ang = pos.astype(F32)[:, None] * inv[None, :]
    return jnp.cos(ang), jnp.sin(ang)


def _head_major(c, heads):
    nb, t, _ = c.shape
    ct = jnp.swapaxes(c[:, :, :heads] * LOG2E, 1, 2).reshape(nb * heads, t)
    return ct[:, :, None], ct[:, None, :]


def kernel(x_prompt, x_sample, cache_k, cache_v, cache_logf, state_ret, state_conv, meta, g_attn, g_ffn,
           w_ret_in, w_ret_out, g_kv, w_kvf, b_f, g_k, w_q_b, g_q, w_o_b, w_gu, conv_w, conv_b, w_down):
    nbp, seq, d = x_prompt.shape
    nbs, dseq, _ = x_sample.shape
    n_meta = meta.shape[0]
    past = cache_k.shape[1]
    depth = g_attn.shape[0]
    n_a = w_ret_in.shape[0]
    h_a, dk_a, dv_a = state_ret.shape[2:]
    h_b, hd_b = cache_k.shape[2:]
    qk_w, v_w = h_a * dk_a, h_a * dv_a
    d_ff = w_down.shape[1]
    cw_taps = conv_w.shape[1]
    assert n_a == 1 and depth == 2 and cw_taps == 3, "kernel is written for one retention + one attention layer"
    ff_tile = 1024 if d_ff >= 4096 else 256
    ffp = -(-d_ff // ff_tile) * ff_tile
    m_main = nbp * seq
    m_s = nbs * dseq
    m_sm = m_s + n_meta
    m_rest = -(-m_sm // dseq) * dseq
    dt = x_prompt.dtype

    w_in = w_ret_in[0].astype(BF16)
    w_out = w_ret_out[0].astype(BF16)
    w_kv = w_kvf[:, :2 * d].astype(BF16)
    w_f = jnp.pad(w_kvf[:, 2 * d:], ((0, 0), (0, LANE - h_b))).astype(BF16)
    b_fp = jnp.pad(b_f, (0, LANE - h_b)).reshape(1, LANE)
    w_q = w_q_b[0].astype(BF16)
    w_o = w_o_b[0].astype(BF16)
    padc = lambda a: jnp.pad(a, ((0, 0),) * (a.ndim - 1) + ((0, ffp - d_ff),))
    w_gu_p = [jnp.concatenate([padc(w_gu[l][:, :d_ff]), padc(w_gu[l][:, d_ff:])], axis=1).astype(BF16)
              for l in range(depth)]
    w_dn_p = [jnp.pad(w_down[l], ((0, ffp - d_ff), (0, 0))).astype(BF16) for l in range(depth)]
    cw_p = [jnp.pad(padc(conv_w[l]), ((0, SUBLANE - cw_taps), (0, 0))) for l in range(depth)]
    cb_p = [padc(conv_b[l]).reshape(1, ffp) for l in range(depth)]
    log_g = jnp.log1p(-jnp.exp2(-5.0 - jnp.arange(h_a, dtype=F32)))

    x_main = x_prompt.reshape(m_main, d)
    pad_rows = lambda a: jnp.pad(a, ((0, m_rest - m_sm),) + ((0, 0),) * (a.ndim - 1))
    x_rest = pad_rows(jnp.concatenate([x_sample.reshape(m_s, d), meta.astype(dt)], axis=0))
    cos_m, sin_m = _rope_tables(n_meta + jnp.arange(seq), dk_a // 2)
    pos_rest = pad_rows(jnp.concatenate([jnp.tile(n_meta + past + jnp.arange(dseq), nbs), jnp.arange(n_meta)]))
    cos_r, sin_r = _rope_tables(pos_rest, dk_a // 2)

    def state_rows(rows2):
        return jnp.pad(padc(rows2), ((0, 0), (SUBLANE - 2, 0), (0, 0)))

    def conv_ffn(x, l, state):
        (hn,) = rmsnorm_rows(x, g_ffn[l:l + 1])
        act, tails = matmul_conv_gate(hn, w_gu_p[l], state, cw_p[l], cb_p[l], rows_per_stream=seq)
        return matmul_residual(act, w_dn_p[l], x), tails[:, SUBLANE - 2:, :d_ff]

    def ret_in(h, cos, sin, period):
        q = matmul_rope(h, w_in, cos, sin, col_off=0, n_out=qk_w, head_dim=dk_a, scale=1.0, period_rows=period)
        k = matmul_rope(h, w_in, cos, sin, col_off=qk_w, n_out=qk_w, head_dim=dk_a, scale=dk_a ** -0.5,
                        period_rows=period)
        v = matmul(h, w_in, col_off=2 * qk_w, n_out=v_w, out_dtype=BF16)
        gate = matmul(h, w_in, col_off=2 * qk_w + v_w, n_out=v_w, out_dtype=BF16)
        return q, k, v, gate

    (h_r,) = rmsnorm_rows(x_rest, g_attn[0:1])
    q_r, k_r, v_r, gate_r = ret_in(h_r, cos_r, sin_r, m_rest)
    og_s, ret_s = retention(log_g, q_r, k_r, v_r, gate_r, state_ret[0], nb=nbs, rows=dseq, chunk=dseq,
                            row_off=0, heads=h_a)
    og_m, ret_meta = retention(log_g, q_r, k_r, v_r, gate_r, jnp.zeros((1, h_a, dk_a, dv_a), F32), nb=1,
                               rows=n_meta, chunk=n_meta, row_off=m_s, heads=h_a)
    x_r1 = matmul_residual(pad_rows(jnp.concatenate([og_s, og_m], axis=0)), w_out, x_rest)
    assert n_meta <= dseq and n_meta % SUBLANE == 0
    state_r0 = jnp.concatenate([state_rows(state_conv[0]), jnp.zeros((1, SUBLANE, ffp), F32)], axis=0)

    def conv_ffn_rest(x, l, state):
        (hn,) = rmsnorm_rows(x, g_ffn[l:l + 1])
        gu = matmul(hn, w_gu_p[l])
        act = conv_gate(gu, state, cw_p[l], cb_p[l], rows_per_stream=dseq, row_tile=dseq)
        return matmul_residual(act, w_dn_p[l], x), gu

    x_r2, gu_r0 = conv_ffn_rest(x_r1, 0, state_r0)

    (h_m,) = rmsnorm_rows(x_main, g_attn[0:1])
    q_m, k_m, v_m, gate_m = ret_in(h_m, cos_m, sin_m, seq)
    chunk = _divisor_tile(seq, 256, 16)
    og_main, ret_p = retention(log_g, q_m, k_m, v_m, gate_m, ret_meta, nb=nbp, rows=seq, chunk=chunk,
                               row_off=0, heads=h_a)
    x_m1 = matmul_residual(og_main, w_out, x_main)
    state_m0 = jnp.broadcast_to(state_rows(gu_r0[None, m_sm - 2:m_sm, :d_ff]), (nbp, SUBLANE, ffp))
    x_m2, conv_p0 = conv_ffn(x_m1, 0, state_m0)

    def kv_side(x, g_row):
        h, hk = rmsnorm_rows(x, jnp.stack([g_row, g_kv]))
        kk = matmul_headnorm(hk, w_kv, g_k, col_off=0, n_out=d, head_dim=hd_b, scale=1.0, out_dtype=F32)
        vv = matmul(hk, w_kv, col_off=d, n_out=d)
        lf = matmul_logsigmoid(hk, w_f, b_fp)
        q = matmul_headnorm(h, w_q, g_q[0], col_off=0, n_out=d, head_dim=hd_b, scale=hd_b ** -0.5 * LOG2E,
                            out_dtype=BF16)
        return q, kk, vv, lf

    q_r, kk_r, vv_r, lf_r = kv_side(x_r2, g_attn[1])
    q_m, kk_m, vv_m, lf_m = kv_side(x_m2, g_attn[1])

    c_cache = cumsum_rows(jnp.pad(cache_logf, ((0, 0), (0, 0), (0, LANE - h_b))), jnp.zeros((1, 1, LANE), F32))
    c_s = cumsum_rows(lf_r[:m_s].reshape(nbs, dseq, LANE), c_cache[:, past - 1:past, :])
    c_meta = cumsum_rows(lf_r[m_s:m_sm].reshape(1, n_meta, LANE), jnp.zeros((1, 1, LANE), F32))
    c_main = cumsum_rows(lf_m.reshape(nbp, seq, LANE), c_meta[:, n_meta - 1:n_meta, :])

    cq_s, ck_s = _head_major(c_s, h_b)
    _, ck_cache = _head_major(c_cache, h_b)
    o_s = fox_attention(q_r, cq_s, kk_r, vv_r, ck_s,
                        (cache_k.reshape(nbs, past, d), cache_v.reshape(nbs, past, d), ck_cache),
                        nb=nbs, rows=dseq, row_off=0, heads=h_b, q_tile=dseq)
    cq_meta, ck_meta = _head_major(c_meta, h_b)
    o_meta = fox_attention(q_r, cq_meta, kk_r, vv_r, ck_meta, None,
                           nb=1, rows=n_meta, row_off=m_s, heads=h_b, q_tile=n_meta)
    cq_main, ck_main = _head_major(c_main, h_b)
    k_meta = kk_r[m_s:m_sm]
    v_meta = vv_r[m_s:m_sm]
    o_main = fox_attention(q_m, cq_main, kk_m, vv_m, ck_main, (k_meta[None], v_meta[None], ck_meta),
                           nb=nbp, rows=seq, row_off=0, heads=h_b, q_tile=_divisor_tile(seq, 512, 16))

    x_r3 = matmul_residual(pad_rows(jnp.concatenate([o_s, o_meta], axis=0)), w_o, x_r2)
    state_r1 = jnp.concatenate([state_rows(state_conv[1]), jnp.zeros((1, SUBLANE, ffp), F32)], axis=0)
    x_r4, gu_r1 = conv_ffn_rest(x_r3, 1, state_r1)
    x_m3 = matmul_residual(o_main, w_o, x_m2)
    state_m1 = jnp.broadcast_to(state_rows(gu_r1[None, m_sm - 2:m_sm, :d_ff]), (nbp, SUBLANE, ffp))
    x_m4, conv_p1 = conv_ffn(x_m3, 1, state_m1)

    def with_meta(meta_rows, main_rows, tail):
        mr = jnp.broadcast_to(meta_rows.reshape((1, n_meta) + tail), (nbp, n_meta) + tail)
        return jnp.concatenate([mr, main_rows.reshape((nbp, seq) + tail)], axis=1)

    y_prompt = x_m4.reshape(nbp, seq, d)
    y_sample = x_r4[:m_s].reshape(nbs, dseq, d)
    k_prompt = with_meta(k_meta, kk_m, (h_b, hd_b))
    v_prompt = with_meta(v_meta, vv_m, (h_b, hd_b))
    logf_prompt = with_meta(lf_r[m_s:m_sm, :h_b], lf_m[:, :h_b], (h_b,))
    conv_prompt = jnp.stack([conv_p0, conv_p1])
    conv_sample = jnp.stack([gu[:m_s].reshape(nbs, dseq, 2 * ffp)[:, dseq - 2:, :d_ff] for gu in (gu_r0, gu_r1)])
    return (y_prompt, y_sample, k_prompt, v_prompt, logf_prompt, ret_p[None], conv_prompt,
            kk_r[:m_s].reshape(nbs, dseq, h_b, hd_b), vv_r[:m_s].reshape(nbs, dseq, h_b, hd_b),
            lf_r[:m_s, :h_b].reshape(nbs, dseq, h_b), ret_s[None], conv_sample)
```

```python
import functools

import jax
import jax.numpy as jnp
from jax import lax
from jax.experimental import pallas as pl
from jax.experimental.pallas import tpu as pltpu

EPS = 1e-6
LOG2E = 1.4426950408889634
ROPE_BASE = 10000.0
MASK_VALUE = -1e30
LANE = 128
SUBLANE = 8
MXU_COLS = 256
VMEM_BUDGET = 44 * 2 ** 20
F32 = jnp.float32
BF16 = jnp.bfloat16


def _params(semantics, vmem_bytes):
    return pltpu.CompilerParams(dimension_semantics=semantics,
                                vmem_limit_bytes=int(min(vmem_bytes + 8 * 2 ** 20, 60 * 2 ** 20)))


def _divisor_tile(n, cap, mult):
    if n <= cap:
        return n
    t = cap - cap % mult
    while t >= mult:
        if n % t == 0:
            return t
        t -= mult
    raise ValueError(f"no tile for {n} (cap {cap}, multiple {mult})")


def _rmsnorm_kernel(x_ref, g_ref, *o_refs):
    x = x_ref[...]
    y = x * lax.rsqrt(jnp.mean(x * x, axis=-1, keepdims=True) + EPS)
    for n, o_ref in enumerate(o_refs):
        o_ref[...] = (y * g_ref[n:n + 1, :]).astype(o_ref.dtype)


def rmsnorm_rows(x, gains):
    m, d = x.shape
    g = gains.shape[0]
    tm = _divisor_tile(m, 256, 16)
    vmem = 2 * tm * d * (4 + 2 * g)
    outs = pl.pallas_call(
        _rmsnorm_kernel,
        grid=(m // tm,),
        in_specs=[pl.BlockSpec((tm, d), lambda i: (i, 0)),
                  pl.BlockSpec((g, d), lambda i: (0, 0))],
        out_specs=[pl.BlockSpec((tm, d), lambda i: (i, 0)) for _ in range(g)],
        out_shape=[jax.ShapeDtypeStruct((m, d), BF16) for _ in range(g)],
        compiler_params=_params(("parallel",), vmem),
        name="rmsnorm",
    )(x, gains)
    return tuple(outs)


def _epi_plain(acc, o_ref):
    o_ref[...] = acc.astype(o_ref.dtype)


def _epi_residual(acc, res_ref, o_ref):
    o_ref[...] = res_ref[...] + acc


def _epi_rope(acc, cos_ref, sin_ref, o_ref, *, head_dim, scale):
    half = head_dim // 2
    cos = cos_ref[...]
    sin = sin_ref[...]
    for h in range(acc.shape[1] // head_dim):
        x1 = acc[:, h * head_dim:h * head_dim + half]
        x2 = acc[:, h * head_dim + half:(h + 1) * head_dim]
        o_ref[:, h * head_dim:h * head_dim + half] = ((x1 * cos - x2 * sin) * scale).astype(o_ref.dtype)
        o_ref[:, h * head_dim + half:(h + 1) * head_dim] = ((x1 * sin + x2 * cos) * scale).astype(o_ref.dtype)


def _epi_headnorm(acc, g_ref, o_ref, *, head_dim, scale):
    g = g_ref[...]
    for h in range(acc.shape[1] // head_dim):
        x = acc[:, h * head_dim:(h + 1) * head_dim]
        y = x * lax.rsqrt(jnp.mean(x * x, axis=-1, keepdims=True) + EPS) * g
        o_ref[:, h * head_dim:(h + 1) * head_dim] = (y * scale).astype(o_ref.dtype)


def _epi_logsigmoid(acc, b_ref, o_ref):
    z = acc + b_ref[...]
    o_ref[...] = jnp.minimum(z, 0.0) - jnp.log1p(jnp.exp(-jnp.abs(z)))


def _mm_kernel(x_ref, w_ref, *rest, epilogue, nk):
    if nk == 1:
        epilogue(jnp.dot(x_ref[...], w_ref[...], preferred_element_type=F32), *rest)
        return
    acc_ref = rest[-1]
    k = pl.program_id(2)

    @pl.when(k == 0)
    def _():
        acc_ref[...] = jnp.zeros_like(acc_ref)

    acc_ref[...] += jnp.dot(x_ref[...], w_ref[...], preferred_element_type=F32)

    @pl.when(k == nk - 1)
    def _():
        epilogue(acc_ref[...], *rest[:-1])


def _mm_tiles(m, k, n, out_bytes, extra_bytes_per_elem):
    tm = _divisor_tile(m, 1024, 16)
    for tn in (1024, 512, 256, 128):
        if n % tn:
            continue
        for nk in (1, 2, 4, 8):
            if k % nk or (k // nk) % LANE:
                continue
            tk = k // nk
            ws = 2 * (tm * tk * 2 + tk * tn * 2 + tm * tn * (out_bytes + extra_bytes_per_elem))
            ws += tm * tn * 4
            if ws <= VMEM_BUDGET:
                return tm, tn, tk, ws
    raise ValueError(f"no matmul tiling for {(m, k, n)}")


def matmul(x, w, *, col_off=0, n_out=None, epilogue=_epi_plain, extras=(), extra_specs=lambda tm, tn: [],
           out_dtype=F32, extra_bytes_per_elem=0, row_tile=None):
    m, k = x.shape
    n_out = w.shape[1] - col_off if n_out is None else n_out
    tm, tn, tk, ws = _mm_tiles(m, k, n_out, jnp.dtype(out_dtype).itemsize, extra_bytes_per_elem)
    if row_tile is not None:
        tm = row_tile
    assert col_off % tn == 0 and m % tm == 0
    nk = k // tk
    joff = col_off // tn
    kern = functools.partial(_mm_kernel, epilogue=epilogue, nk=nk)
    return pl.pallas_call(
        kern,
        grid=(m // tm, n_out // tn, nk),
        in_specs=[pl.BlockSpec((tm, tk), lambda i, j, kk: (i, kk)),
                  pl.BlockSpec((tk, tn), lambda i, j, kk: (kk, j + joff)),
                  *extra_specs(tm, tn)],
        out_specs=pl.BlockSpec((tm, tn), lambda i, j, kk: (i, j)),
        out_shape=jax.ShapeDtypeStruct((m, n_out), out_dtype),
        scratch_shapes=[pltpu.VMEM((tm, tn), F32)] if nk > 1 else [],
        compiler_params=_params(("parallel", "parallel", "arbitrary"), ws),
        name="matmul_" + getattr(epilogue, "func", epilogue).__name__.lstrip("_"),
    )(x, w, *extras)


def matmul_residual(x, w, res):
    return matmul(x, w, epilogue=_epi_residual, extras=(res,), extra_bytes_per_elem=4,
                  extra_specs=lambda tm, tn: [pl.BlockSpec((tm, tn), lambda i, j, kk: (i, j))])


def matmul_rope(x, w, cos, sin, *, col_off, n_out, head_dim, scale, period_rows):
    half = head_dim // 2

    def specs(tm, tn):
        assert period_rows % tm == 0 and tn % head_dim == 0
        nper = period_rows // tm
        return [pl.BlockSpec((tm, half), lambda i, j, kk: (i % nper, 0))] * 2

    row_tile = _divisor_tile(period_rows, _divisor_tile(x.shape[0], 1024, 16), 16)
    return matmul(x, w, col_off=col_off, n_out=n_out, out_dtype=BF16, extras=(cos, sin), extra_specs=specs,
                  row_tile=row_tile,
                  epilogue=functools.partial(_epi_rope, head_dim=head_dim, scale=scale))


def matmul_headnorm(x, w, gain, *, col_off, n_out, head_dim, scale, out_dtype):
    return matmul(x, w, col_off=col_off, n_out=n_out, out_dtype=out_dtype, extras=(gain.reshape(1, head_dim),),
                  extra_specs=lambda tm, tn: [pl.BlockSpec((1, head_dim), lambda i, j, kk: (0, 0))],
                  epilogue=functools.partial(_epi_headnorm, head_dim=head_dim, scale=scale))


def matmul_logsigmoid(x, w, bias):
    return matmul(x, w, extras=(bias,), epilogue=_epi_logsigmoid,
                  extra_specs=lambda tm, tn: [pl.BlockSpec((1, tn), lambda i, j, kk: (0, j))])


def _retention_kernel(lg_ref, q_ref, k_ref, v_ref, gate_ref, s0_ref, og_ref, sout_ref, s_scr, decay_scr, *,
                      nc, hps):
    hg = pl.program_id(1)
    c = pl.program_id(2)
    t = q_ref.shape[0]
    dk = q_ref.shape[1] // hps
    dv = v_ref.shape[1] // hps
    idx = lax.broadcasted_iota(jnp.int32, (t, 1), 0).astype(F32)

    @pl.when(c == 0)
    def _():
        rel = (lax.broadcasted_iota(jnp.int32, (t, t), 0) - lax.broadcasted_iota(jnp.int32, (t, t), 1)).astype(F32)
        causal = rel >= 0.0
        for hh in range(hps):
            s_scr[hh] = s0_ref[0, hh]
            decay_scr[hh] = jnp.where(causal, jnp.exp(jnp.where(causal, rel, 0.0) * lg_ref[hg * hps + hh]), 0.0)

    for hh in range(hps):
        lg = lg_ref[hg * hps + hh]
        q = q_ref[:, hh * dk:(hh + 1) * dk]
        k = k_ref[:, hh * dk:(hh + 1) * dk]
        v = v_ref[:, hh * dv:(hh + 1) * dv]
        scores = lax.dot_general(q, k, (((1,), (1,)), ((), ())), preferred_element_type=F32) * decay_scr[hh]
        o = jnp.dot(scores.astype(BF16), v, preferred_element_type=F32)
        s_old = s_scr[hh]
        q_dec = (q.astype(F32) * jnp.exp((idx + 1.0) * lg)).astype(BF16)
        o = o + jnp.dot(q_dec, s_old.astype(BF16), preferred_element_type=F32)
        k_dec = (k.astype(F32) * jnp.exp((t - 1.0 - idx) * lg)).astype(BF16)
        kv = lax.dot_general(k_dec, v, (((0,), (0,)), ((), ())), preferred_element_type=F32)
        s_new = jnp.exp(jnp.full((1, 1), t, F32) * lg) * s_old + kv
        s_scr[hh] = s_new
        on = o * lax.rsqrt(jnp.mean(o * o, axis=-1, keepdims=True) + EPS)
        g = gate_ref[:, hh * dv:(hh + 1) * dv].astype(F32)
        og_ref[:, hh * dv:(hh + 1) * dv] = ((g * jax.nn.sigmoid(g)) * on).astype(og_ref.dtype)

        @pl.when(c == nc - 1)
        def _():
            sout_ref[0, hh] = s_new


def retention(log_g, q, k, v, gate, s0, *, nb, rows, chunk, row_off, heads):
    dk = q.shape[1] // heads
    dv = v.shape[1] // heads
    nc = rows // chunk
    hps = 2 if heads % 2 == 0 else 1
    assert rows % chunk == 0 and row_off % chunk == 0
    roff = row_off // chunk
    shared = s0.shape[0] == 1
    row_map = lambda b, h, c: (roff + b * nc + c, h)
    vmem = hps * (2 * chunk * (2 * dk * 2 + 3 * dv * 2) + 5 * dk * dv * 4 + 8 * chunk * chunk * 4)
    og, s_out = pl.pallas_call(
        functools.partial(_retention_kernel, nc=nc, hps=hps),
        grid=(nb, heads // hps, nc),
        in_specs=[pl.BlockSpec(memory_space=pltpu.SMEM),
                  pl.BlockSpec((chunk, hps * dk), row_map),
                  pl.BlockSpec((chunk, hps * dk), row_map),
                  pl.BlockSpec((chunk, hps * dv), row_map),
                  pl.BlockSpec((chunk, hps * dv), row_map),
                  pl.BlockSpec((1, hps, dk, dv), lambda b, h, c: (0 if shared else b, h, 0, 0))],
        out_specs=[pl.BlockSpec((chunk, hps * dv), lambda b, h, c: (b * nc + c, h)),
                   pl.BlockSpec((1, hps, dk, dv), lambda b, h, c: (b, h, 0, 0))],
        out_shape=[jax.ShapeDtypeStruct((nb * rows, heads * dv), BF16),
                   jax.ShapeDtypeStruct((nb, heads, dk, dv), F32)],
        scratch_shapes=[pltpu.VMEM((hps, dk, dv), F32), pltpu.VMEM((hps, chunk, chunk), F32)],
        compiler_params=_params(("parallel", "parallel", "arbitrary"), vmem),
        name="retention",
    )(log_g, q, k, v, gate, s0)
    return og, s_out


def _conv_silu_gate(a, u, halo, cw, cb):
    r1 = pltpu.roll(a, 1, axis=0)
    r2 = pltpu.roll(a, 2, axis=0)
    row = lax.broadcasted_iota(jnp.int32, (SUBLANE, a.shape[1]), 0)
    top1 = jnp.where(row == 0, halo[7:8, :], r1[:SUBLANE])
    top2 = jnp.where(row == 0, halo[6:7, :], jnp.where(row == 1, halo[7:8, :], r2[:SUBLANE]))
    prev1 = jnp.concatenate([top1, r1[SUBLANE:]], axis=0)
    prev2 = jnp.concatenate([top2, r2[SUBLANE:]], axis=0)
    c = cb + cw[0:1, :] * prev2
    c = c + cw[1:2, :] * prev1
    c = c + cw[2:3, :] * a
    return (c * jax.nn.sigmoid(c)) * u


def _conv_gate_kernel(a_ref, u_ref, halo_ref, state_ref, cw_ref, cb_ref, o_ref, *, tiles_per_stream):
    first = (pl.program_id(0) % tiles_per_stream) == 0
    halo = jnp.where(first, state_ref[0], halo_ref[...])
    o_ref[...] = _conv_silu_gate(a_ref[...], u_ref[...], halo, cw_ref[...], cb_ref[...]).astype(o_ref.dtype)


def conv_gate(a, u, state, cw, cb, *, rows_per_stream, row_tile):
    m, f = a.shape
    tr = row_tile
    tc = _divisor_tile(f, 1024, LANE)
    assert rows_per_stream % tr == 0 and tr % SUBLANE == 0
    tps = rows_per_stream // tr
    nfb = f // tc
    hb = tr // SUBLANE
    vmem = 2 * (tr * tc * (4 + 4 + 2) + 3 * 8 * tc * 4) + 6 * tr * tc * 4
    return pl.pallas_call(
        functools.partial(_conv_gate_kernel, tiles_per_stream=tps),
        grid=(m // tr, nfb),
        in_specs=[pl.BlockSpec((tr, tc), lambda i, j: (i, j)),
                  pl.BlockSpec((tr, tc), lambda i, j: (i, j)),
                  pl.BlockSpec((SUBLANE, tc), lambda i, j: (jnp.maximum(i * hb - 1, 0), j)),
                  pl.BlockSpec((1, SUBLANE, tc), lambda i, j: (i // tps, 0, j)),
                  pl.BlockSpec((SUBLANE, tc), lambda i, j: (0, j)),
                  pl.BlockSpec((1, tc), lambda i, j: (0, j))],
        out_specs=pl.BlockSpec((tr, tc), lambda i, j: (i, j)),
        out_shape=jax.ShapeDtypeStruct((m, f), BF16),
        compiler_params=_params(("parallel", "parallel"), vmem),
        name="conv_gate",
    )(a, u, a, state, cw, cb)


def _gate_up_conv_kernel(x_ref, wa_ref, wu_ref, state_ref, cw_ref, cb_ref, o_ref, tail_ref, halo_scr, *,
                         tiles_per_stream):
    i = pl.program_id(0)
    j = pl.program_id(1)
    first = (i % tiles_per_stream) == 0
    tm = x_ref.shape[0]
    for s in range(o_ref.shape[1] // MXU_COLS):
        cols = slice(s * MXU_COLS, (s + 1) * MXU_COLS)
        a = jnp.dot(x_ref[...], wa_ref[:, cols], preferred_element_type=F32)
        u = jnp.dot(x_ref[...], wu_ref[:, cols], preferred_element_type=F32)
        halo = jnp.where(first, state_ref[0, :, cols], halo_scr[j, :, cols])
        o_ref[:, cols] = _conv_silu_gate(a, u, halo, cw_ref[:, cols], cb_ref[:, cols]).astype(o_ref.dtype)
        tail = a[tm - SUBLANE:, :]
        halo_scr[j, :, cols] = tail
        tail_ref[0, :, cols] = tail


def matmul_conv_gate(x, wa, wu, state, cw, cb, *, rows_per_stream):
    m, k = x.shape
    f = wa.shape[1]
    tm = _divisor_tile(rows_per_stream, 1024, 16)
    tn = 2 * MXU_COLS
    assert f % tn == 0 and m % rows_per_stream == 0
    tps = rows_per_stream // tm
    nfb = f // tn
    vmem = 2 * (tm * k * 2 + 2 * k * tn * 2 + tm * tn * 2) + SUBLANE * f * 4 + 8 * tm * MXU_COLS * 4
    act, tails = pl.pallas_call(
        functools.partial(_gate_up_conv_kernel, tiles_per_stream=tps),
        grid=(m // tm, nfb),
        in_specs=[pl.BlockSpec((tm, k), lambda i, j: (i, 0)),
                  pl.BlockSpec((k, tn), lambda i, j: (0, j)),
                  pl.BlockSpec((k, tn), lambda i, j: (0, j)),
                  pl.BlockSpec((1, SUBLANE, tn), lambda i, j: (i // tps, 0, j)),
                  pl.BlockSpec((SUBLANE, tn), lambda i, j: (0, j)),
                  pl.BlockSpec((1, tn), lambda i, j: (0, j))],
        out_specs=[pl.BlockSpec((tm, tn), lambda i, j: (i, j)),
                   pl.BlockSpec((1, SUBLANE, tn), lambda i, j: (i, 0, j))],
        out_shape=[jax.ShapeDtypeStruct((m, f), BF16),
                   jax.ShapeDtypeStruct((m // tm, SUBLANE, f), F32)],
        scratch_shapes=[pltpu.VMEM((nfb, SUBLANE, tn), F32)],
        compiler_params=_params(("arbitrary", "arbitrary"), vmem),
        name="matmul_gate_up_conv",
    )(x, wa, wu, state, cw, cb)
    return act, tails[tps - 1::tps]


def _cumsum_kernel(x_ref, c0_ref, o_ref, *, blk):
    t = x_ref.shape[1]
    r = lax.broadcasted_iota(jnp.int32, (blk, blk), 0)
    s = lax.broadcasted_iota(jnp.int32, (blk, blk), 1)
    tri = (s <= r).astype(BF16)
    carry = c0_ref[0]
    for b in range(t // blk):
        x = x_ref[0, b * blk:(b + 1) * blk, :]
        x_hi = x.astype(BF16)
        r1 = x - x_hi.astype(F32)
        x_mid = r1.astype(BF16)
        x_lo = (r1 - x_mid.astype(F32)).astype(BF16)
        c = jnp.dot(tri, x_hi, preferred_element_type=F32)
        c = c + jnp.dot(tri, x_mid, preferred_element_type=F32)
        c = c + jnp.dot(tri, x_lo, preferred_element_type=F32)
        c = c + carry
        o_ref[0, b * blk:(b + 1) * blk, :] = c
        carry = c[blk - 1:blk, :]


def cumsum_rows(x, c0):
    nb, t, l = x.shape
    blk = _divisor_tile(t, 256, 16)
    shared = c0.shape[0] == 1
    return pl.pallas_call(
        functools.partial(_cumsum_kernel, blk=blk),
        grid=(nb,),
        in_specs=[pl.BlockSpec((1, t, l), lambda b: (b, 0, 0)),
                  pl.BlockSpec((1, 1, l), lambda b: (0 if shared else b, 0, 0))],
        out_specs=pl.BlockSpec((1, t, l), lambda b: (b, 0, 0)),
        out_shape=jax.ShapeDtypeStruct((nb, t, l), F32),
        compiler_params=_params(("parallel",), 4 * t * l * 4 + 2 ** 20),
        name="cumsum",
    )(x, c0)


def _fox_kernel(*refs, tq, prefix):
    m_scr, cq_scr, alpha_scr, acc_scr, sa_scr, sb_scr, pa_scr, pb_scr, kb_scr, v1_scr = refs[-10:]
    if prefix:
        q_ref, cq_ref, km_ref, vm_ref, ckm_ref, kp_ref, vp_ref, ckp_ref, o_ref = refs[:-10]
    else:
        q_ref, cq_ref, km_ref, vm_ref, ckm_ref, o_ref = refs[:-10]
    hd = q_ref.shape[1]
    nq = q_ref.shape[0] // tq

    def lanes(x, width):
        return jnp.tile(x, (1, width // LANE)) if width % LANE == 0 else x[:, :1]

    def ones_columns(vb):
        return jnp.concatenate([vb.astype(BF16), jnp.ones((vb.shape[0], LANE), BF16)], axis=1)

    for c in range(nq):
        rows = slice(c * tq, (c + 1) * tq)
        kb_scr[rows, :] = km_ref[rows, :].astype(BF16)
        v1_scr[rows, :] = ones_columns(vm_ref[rows, :])

    def block(j):
        return pl.ds(pl.multiple_of(j * tq, tq), tq)

    head_lane = lax.broadcasted_iota(jnp.int32, (tq, LANE), 1) == pl.program_id(1)

    def query_block(qi, carry):
        q = q_ref[block(qi), :]
        m_scr[...] = jnp.full(m_scr.shape, MASK_VALUE, F32)
        acc_scr[...] = jnp.zeros(acc_scr.shape, F32)
        cq = jnp.sum(jnp.where(head_lane, cq_ref[0, block(qi), :], 0.0), axis=-1, keepdims=True)
        cq_scr[...] = jnp.broadcast_to(cq, cq_scr.shape)

        def scores(s_ref, kb):
            s_ref[:, :kb.shape[0]] = lax.dot_general(q, kb, (((1,), (1,)), ((), ())), preferred_element_type=F32)

        def softmax(s_ref, p_ref, ck, causal):
            width = ck.shape[1]
            rb = min(tq, max(16, 32768 // max(width, LANE)))
            for r in range(tq // rb):
                rows = slice(r * rb, (r + 1) * rb)
                s = s_ref[rows, :width] - ck
                if causal:
                    row = r * rb + lax.broadcasted_iota(jnp.int32, (rb, width), 0)
                    s = jnp.where(lax.broadcasted_iota(jnp.int32, (rb, width), 1) <= row, s, MASK_VALUE)
                cq = cq_scr[rows, :]
                m_old = m_scr[rows, :]
                m_new = jnp.maximum(m_old, jnp.max(s, axis=-1, keepdims=True) + cq)
                p_ref[rows, :width] = jnp.exp2(s - lanes(m_new - cq, width)).astype(BF16)
                m_scr[rows, :] = m_new
                alpha_scr[rows, :] = jnp.exp2(m_old - m_new)

        def accumulate(p, v1, rescale):
            pv = jnp.dot(p, v1, preferred_element_type=F32)
            if rescale:
                acc_scr[...] = (acc_scr[...] + pv) * jnp.tile(alpha_scr[...], (1, 2))
            else:
                acc_scr[...] += pv

        if prefix:
            pw = kp_ref.shape[1]
            scores(sa_scr, kp_ref[0].astype(BF16))
            softmax(sa_scr, pa_scr, ckp_ref[0], False)
            accumulate(pa_scr[:, :pw], ones_columns(vp_ref[0]), False)

        scores(sa_scr, kb_scr[block(0), :])
        pb_scr[...] = jnp.zeros(pb_scr.shape, BF16)

        def pair(t, carry):
            j = 2 * t
            scores(sb_scr, kb_scr[block(j + 1), :])
            softmax(sa_scr, pa_scr, ckm_ref[0, j], False)
            accumulate(pb_scr[:, :tq], v1_scr[block(jnp.maximum(j - 1, 0)), :], True)
            scores(sa_scr, kb_scr[block(j + 2), :])
            softmax(sb_scr, pb_scr, ckm_ref[0, j + 1], False)
            accumulate(pa_scr[:, :tq], v1_scr[block(j), :], True)
            return carry

        lax.fori_loop(0, qi // 2, pair, 0)
        last = jnp.maximum(qi - 1, 0)

        @pl.when(qi % 2 == 0)
        def _():
            softmax(sa_scr, pa_scr, ckm_ref[0, qi], True)
            accumulate(pb_scr[:, :tq], v1_scr[block(last), :], True)
            accumulate(pa_scr[:, :tq], v1_scr[block(qi), :], False)

        @pl.when(qi % 2 == 1)
        def _():
            scores(sb_scr, kb_scr[block(qi), :])
            softmax(sa_scr, pa_scr, ckm_ref[0, last], False)
            accumulate(pb_scr[:, :tq], v1_scr[block(jnp.maximum(qi - 2, 0)), :], True)
            softmax(sb_scr, pb_scr, ckm_ref[0, qi], True)
            accumulate(pa_scr[:, :tq], v1_scr[block(last), :], True)
            accumulate(pb_scr[:, :tq], v1_scr[block(qi), :], False)

        o_ref[block(qi), :] = (acc_scr[:, :hd] / acc_scr[:, hd:]).astype(o_ref.dtype)
        return carry

    lax.fori_loop(0, nq, query_block, 0)


def fox_attention(q, cq, km, vm, ckm, prefix, *, nb, rows, row_off, heads, q_tile):
    hd = q.shape[1] // heads
    tq = q_tile
    nq = rows // tq
    assert rows % tq == 0 and row_off % rows == 0 and hd == LANE and heads <= LANE
    boff = row_off // rows
    in_specs = [pl.BlockSpec((rows, hd), lambda b, h: (boff + b, h)),
                pl.BlockSpec((1, rows, LANE), lambda b, h: (b, 0, 0)),
                pl.BlockSpec((rows, hd), lambda b, h: (boff + b, h)),
                pl.BlockSpec((rows, hd), lambda b, h: (boff + b, h)),
                pl.BlockSpec((1, nq, 1, tq), lambda b, h: (b * heads + h, 0, 0, 0))]
    args = [q, cq, km, vm, ckm.reshape(nb * heads, nq, 1, tq)]
    vmem = 2 * (2 * rows * hd * 2 + rows * LANE * 4 + 2 * rows * hd * 4 + 8 * rows * 4)
    width = tq
    if prefix is not None:
        kp, vp, ckp = prefix
        p = kp.shape[1]
        width = max(tq, p)
        shared = kp.shape[0] == 1
        in_specs += [pl.BlockSpec((1, p, hd), lambda b, h: (0 if shared else b, 0, h)),
                     pl.BlockSpec((1, p, hd), lambda b, h: (0 if shared else b, 0, h)),
                     pl.BlockSpec((1, 1, p), lambda b, h: (h if shared else b * heads + h, 0, 0))]
        args += [kp, vp, ckp]
        vmem += 2 * (2 * p * hd * 4 + 8 * p * 4)
    width = -(-width // LANE) * LANE
    vmem += tq * width * 12 + 5 * tq * LANE * 4 + rows * hd * 6 + 2 ** 21
    return pl.pallas_call(
        functools.partial(_fox_kernel, tq=tq, prefix=prefix is not None),
        grid=(nb, heads),
        in_specs=in_specs,
        out_specs=pl.BlockSpec((rows, hd), lambda b, h: (b, h)),
        out_shape=jax.ShapeDtypeStruct((nb * rows, heads * hd), BF16),
        scratch_shapes=[pltpu.VMEM((tq, LANE), F32), pltpu.VMEM((tq, LANE), F32), pltpu.VMEM((tq, LANE), F32),
                        pltpu.VMEM((tq, 2 * hd), F32),
                        pltpu.VMEM((tq, width), F32), pltpu.VMEM((tq, width), F32),
                        pltpu.VMEM((tq, width), BF16), pltpu.VMEM((tq, width), BF16),
                        pltpu.VMEM((rows, hd), BF16), pltpu.VMEM((rows, 2 * hd), BF16)],
        compiler_params=_params(("parallel", "parallel"), vmem),
        name="fox_attention",
    )(*args)


def _rope_tables(pos, half):
    inv = ROPE_BASE ** (-jnp.arange(half, dtype=F32) / half)
    ang = pos.astype(F32)[:, None] * inv[None, :]
    return jnp.cos(ang), jnp.sin(ang)


def _head_major(c, heads):
    nb, t, _ = c.shape
    c2 = c * LOG2E
    return c2, jnp.swapaxes(c2[:, :, :heads], 1, 2).reshape(nb * heads, 1, t)


def kernel(x_prompt, x_sample, cache_k, cache_v, cache_logf, state_ret, state_conv, meta, g_attn, g_ffn,
           w_ret_in, w_ret_out, g_kv, w_kvf, b_f, g_k, w_q_b, g_q, w_o_b, w_gu, conv_w, conv_b, w_down):
    nbp, seq, d = x_prompt.shape
    nbs, dseq, _ = x_sample.shape
    n_meta = meta.shape[0]
    past = cache_k.shape[1]
    depth = g_attn.shape[0]
    n_a = w_ret_in.shape[0]
    h_a, dk_a, dv_a = state_ret.shape[2:]
    h_b, hd_b = cache_k.shape[2:]
    qk_w, v_w = h_a * dk_a, h_a * dv_a
    d_ff = w_down.shape[1]
    cw_taps = conv_w.shape[1]
    assert n_a == 1 and depth == 2 and cw_taps == 3, "kernel is written for one retention + one attention layer"
    ff_tile = 1024 if d_ff >= 4096 else 256
    ffp = -(-d_ff // ff_tile) * ff_tile
    m_main = nbp * seq
    m_s = nbs * dseq
    m_sm = m_s + n_meta
    m_rest = -(-m_sm // dseq) * dseq
    dt = x_prompt.dtype

    w_in = w_ret_in[0].astype(BF16)
    w_out = w_ret_out[0].astype(BF16)
    w_kv = w_kvf[:, :2 * d].astype(BF16)
    w_f = jnp.pad(w_kvf[:, 2 * d:], ((0, 0), (0, LANE - h_b))).astype(BF16)
    b_fp = jnp.pad(b_f, (0, LANE - h_b)).reshape(1, LANE)
    w_q = w_q_b[0].astype(BF16)
    w_o = w_o_b[0].astype(BF16)
    padc = lambda a: jnp.pad(a, ((0, 0),) * (a.ndim - 1) + ((0, ffp - d_ff),))
    w_ga_p = [padc(w_gu[l, :, :d_ff]).astype(BF16) for l in range(depth)]
    w_up_p = [padc(w_gu[l, :, d_ff:]).astype(BF16) for l in range(depth)]
    w_dn_p = [jnp.pad(w_down[l], ((0, ffp - d_ff), (0, 0))).astype(BF16) for l in range(depth)]
    cw_p = [jnp.pad(padc(conv_w[l]), ((0, SUBLANE - cw_taps), (0, 0))) for l in range(depth)]
    cb_p = [padc(conv_b[l]).reshape(1, ffp) for l in range(depth)]
    log_g = jnp.log1p(-jnp.exp2(-5.0 - jnp.arange(h_a, dtype=F32)))

    x_main = x_prompt.reshape(m_main, d)
    pad_rows = lambda a: jnp.pad(a, ((0, m_rest - m_sm),) + ((0, 0),) * (a.ndim - 1))
    x_rest = pad_rows(jnp.concatenate([x_sample.reshape(m_s, d), meta.astype(dt)], axis=0))
    cos_m, sin_m = _rope_tables(n_meta + jnp.arange(seq), dk_a // 2)
    pos_rest = pad_rows(jnp.concatenate([jnp.tile(n_meta + past + jnp.arange(dseq), nbs), jnp.arange(n_meta)]))
    cos_r, sin_r = _rope_tables(pos_rest, dk_a // 2)

    def state_rows(rows2):
        return jnp.pad(padc(rows2), ((0, 0), (SUBLANE - 2, 0), (0, 0)))

    def conv_ffn(x, l, state):
        (hn,) = rmsnorm_rows(x, g_ffn[l:l + 1])
        act, tails = matmul_conv_gate(hn, w_ga_p[l], w_up_p[l], state, cw_p[l], cb_p[l], rows_per_stream=seq)
        return matmul_residual(act, w_dn_p[l], x), tails[:, SUBLANE - 2:, :d_ff]

    def ret_in(h, cos, sin, period):
        q = matmul_rope(h, w_in, cos, sin, col_off=0, n_out=qk_w, head_dim=dk_a, scale=1.0, period_rows=period)
        k = matmul_rope(h, w_in, cos, sin, col_off=qk_w, n_out=qk_w, head_dim=dk_a, scale=dk_a ** -0.5,
                        period_rows=period)
        v = matmul(h, w_in, col_off=2 * qk_w, n_out=v_w, out_dtype=BF16)
        gate = matmul(h, w_in, col_off=2 * qk_w + v_w, n_out=v_w, out_dtype=BF16)
        return q, k, v, gate

    (h_r,) = rmsnorm_rows(x_rest, g_attn[0:1])
    q_r, k_r, v_r, gate_r = ret_in(h_r, cos_r, sin_r, m_rest)
    og_s, ret_s = retention(log_g, q_r, k_r, v_r, gate_r, state_ret[0], nb=nbs, rows=dseq, chunk=dseq,
                            row_off=0, heads=h_a)
    og_m, ret_meta = retention(log_g, q_r, k_r, v_r, gate_r, jnp.zeros((1, h_a, dk_a, dv_a), F32), nb=1,
                               rows=n_meta, chunk=n_meta, row_off=m_s, heads=h_a)
    x_r1 = matmul_residual(pad_rows(jnp.concatenate([og_s, og_m], axis=0)), w_out, x_rest)
    assert n_meta <= dseq and n_meta % SUBLANE == 0
    state_r0 = jnp.concatenate([state_rows(state_conv[0]), jnp.zeros((1, SUBLANE, ffp), F32)], axis=0)

    def conv_ffn_rest(x, l, state):
        (hn,) = rmsnorm_rows(x, g_ffn[l:l + 1])
        a = matmul(hn, w_ga_p[l])
        act = conv_gate(a, matmul(hn, w_up_p[l]), state, cw_p[l], cb_p[l], rows_per_stream=dseq, row_tile=dseq)
        return matmul_residual(act, w_dn_p[l], x), a

    x_r2, a_r0 = conv_ffn_rest(x_r1, 0, state_r0)

    (h_m,) = rmsnorm_rows(x_main, g_attn[0:1])
    q_m, k_m, v_m, gate_m = ret_in(h_m, cos_m, sin_m, seq)
    chunk = _divisor_tile(seq, 256, 16)
    og_main, ret_p = retention(log_g, q_m, k_m, v_m, gate_m, ret_meta, nb=nbp, rows=seq, chunk=chunk,
                               row_off=0, heads=h_a)
    x_m1 = matmul_residual(og_main, w_out, x_main)
    state_m0 = jnp.broadcast_to(state_rows(a_r0[None, m_sm - 2:m_sm, :d_ff]), (nbp, SUBLANE, ffp))
    x_m2, conv_p0 = conv_ffn(x_m1, 0, state_m0)

    def kv_side(x, g_row):
        h, hk = rmsnorm_rows(x, jnp.stack([g_row, g_kv]))
        kk = matmul_headnorm(hk, w_kv, g_k, col_off=0, n_out=d, head_dim=hd_b, scale=1.0, out_dtype=F32)
        vv = matmul(hk, w_kv, col_off=d, n_out=d)
        lf = matmul_logsigmoid(hk, w_f, b_fp)
        q = matmul_headnorm(h, w_q, g_q[0], col_off=0, n_out=d, head_dim=hd_b, scale=hd_b ** -0.5 * LOG2E,
                            out_dtype=BF16)
        return q, kk, vv, lf

    q_r, kk_r, vv_r, lf_r = kv_side(x_r2, g_attn[1])
    q_m, kk_m, vv_m, lf_m = kv_side(x_m2, g_attn[1])

    c_cache = cumsum_rows(jnp.pad(cache_logf, ((0, 0), (0, 0), (0, LANE - h_b))), jnp.zeros((1, 1, LANE), F32))
    c_s = cumsum_rows(lf_r[:m_s].reshape(nbs, dseq, LANE), c_cache[:, past - 1:past, :])
    c_meta = cumsum_rows(lf_r[m_s:m_sm].reshape(1, n_meta, LANE), jnp.zeros((1, 1, LANE), F32))
    c_main = cumsum_rows(lf_m.reshape(nbp, seq, LANE), c_meta[:, n_meta - 1:n_meta, :])

    cq_s, ck_s = _head_major(c_s, h_b)
    _, ck_cache = _head_major(c_cache, h_b)
    o_s = fox_attention(q_r, cq_s, kk_r, vv_r, ck_s,
                        (cache_k.reshape(nbs, past, d), cache_v.reshape(nbs, past, d), ck_cache),
                        nb=nbs, rows=dseq, row_off=0, heads=h_b, q_tile=dseq)
    cq_meta, ck_meta = _head_major(c_meta, h_b)
    o_meta = fox_attention(q_r, cq_meta, kk_r, vv_r, ck_meta, None,
                           nb=1, rows=n_meta, row_off=m_s, heads=h_b, q_tile=n_meta)
    cq_main, ck_main = _head_major(c_main, h_b)
    k_meta = kk_r[m_s:m_sm]
    v_meta = vv_r[m_s:m_sm]
    o_main = fox_attention(q_m, cq_main, kk_m, vv_m, ck_main, (k_meta[None], v_meta[None], ck_meta),
                           nb=nbp, rows=seq, row_off=0, heads=h_b, q_tile=_divisor_tile(seq, 512, 16))

    x_r3 = matmul_residual(pad_rows(jnp.concatenate([o_s, o_meta], axis=0)), w_o, x_r2)
    state_r1 = jnp.concatenate([state_rows(state_conv[1]), jnp.zeros((1, SUBLANE, ffp), F32)], axis=0)
    x_r4, a_r1 = conv_ffn_rest(x_r3, 1, state_r1)
    x_m3 = matmul_residual(o_main, w_o, x_m2)
    state_m1 = jnp.broadcast_to(state_rows(a_r1[None, m_sm - 2:m_sm, :d_ff]), (nbp, SUBLANE, ffp))
    x_m4, conv_p1 = conv_ffn(x_m3, 1, state_m1)

    def with_meta(meta_rows, main_rows, tail):
        mr = jnp.broadcast_to(meta_rows.reshape((1, n_meta) + tail), (nbp, n_meta) + tail)
        return jnp.concatenate([mr, main_rows.reshape((nbp, seq) + tail)], axis=1)

    y_prompt = x_m4.reshape(nbp, seq, d)
    y_sample = x_r4[:m_s].reshape(nbs, dseq, d)
    k_prompt = with_meta(k_meta, kk_m, (h_b, hd_b))
    v_prompt = with_meta(v_meta, vv_m, (h_b, hd_b))
    logf_prompt = with_meta(lf_r[m_s:m_sm, :h_b], lf_m[:, :h_b], (h_b,))
    conv_prompt = jnp.stack([conv_p0, conv_p1])
    conv_sample = jnp.stack([a[:m_s].reshape(nbs, dseq, ffp)[:, dseq - 2:, :d_ff] for a in (a_r0, a_r1)])
    return (y_prompt, y_sample, k_prompt, v_prompt, logf_prompt, ret_p[None], conv_prompt,
            kk_r[:m_s].reshape(nbs, dseq, h_b, hd_b), vv_r[:m_s].reshape(nbs, dseq, h_b, hd_b),
            lf_r[:m_s, :h_b].reshape(nbs, dseq, h_b), ret_s[None], conv_sample)
```

```python
import functools

import jax
import jax.numpy as jnp
from jax import lax
from jax.experimental import pallas as pl
from jax.experimental.pallas import tpu as pltpu

EPS = 1e-6
LOG2E = 1.4426950408889634
ROPE_BASE = 10000.0
MASK_VALUE = -1e30
LANE = 128
SUBLANE = 8
MXU_COLS = 256
VMEM_BUDGET = 44 * 2 ** 20
F32 = jnp.float32
BF16 = jnp.bfloat16


def _params(semantics, vmem_bytes):
    return pltpu.CompilerParams(dimension_semantics=semantics,
                                vmem_limit_bytes=int(min(vmem_bytes + 8 * 2 ** 20, 60 * 2 ** 20)))


def _divisor_tile(n, cap, mult):
    if n <= cap:
        return n
    t = cap - cap % mult
    while t >= mult:
        if n % t == 0:
            return t
        t -= mult
    raise ValueError(f"no tile for {n} (cap {cap}, multiple {mult})")


def _rmsnorm_kernel(x_ref, g_ref, *o_refs):
    x = x_ref[...]
    y = x * lax.rsqrt(jnp.mean(x * x, axis=-1, keepdims=True) + EPS)
    for n, o_ref in enumerate(o_refs):
        o_ref[...] = (y * g_ref[n:n + 1, :]).astype(o_ref.dtype)


def rmsnorm_rows(x, gains):
    m, d = x.shape
    g = gains.shape[0]
    tm = _divisor_tile(m, 256, 16)
    vmem = 2 * tm * d * (4 + 2 * g)
    outs = pl.pallas_call(
        _rmsnorm_kernel,
        grid=(m // tm,),
        in_specs=[pl.BlockSpec((tm, d), lambda i: (i, 0)),
                  pl.BlockSpec((g, d), lambda i: (0, 0))],
        out_specs=[pl.BlockSpec((tm, d), lambda i: (i, 0)) for _ in range(g)],
        out_shape=[jax.ShapeDtypeStruct((m, d), BF16) for _ in range(g)],
        compiler_params=_params(("parallel",), vmem),
        name="rmsnorm",
    )(x, gains)
    return tuple(outs)


def _epi_plain(acc, o_ref):
    o_ref[...] = acc.astype(o_ref.dtype)


def _epi_residual(acc, res_ref, o_ref):
    o_ref[...] = res_ref[...] + acc


def _epi_rope(acc, cos_ref, sin_ref, o_ref, *, head_dim, scale):
    half = head_dim // 2
    cos = cos_ref[...]
    sin = sin_ref[...]
    for h in range(acc.shape[1] // head_dim):
        x1 = acc[:, h * head_dim:h * head_dim + half]
        x2 = acc[:, h * head_dim + half:(h + 1) * head_dim]
        o_ref[:, h * head_dim:h * head_dim + half] = ((x1 * cos - x2 * sin) * scale).astype(o_ref.dtype)
        o_ref[:, h * head_dim + half:(h + 1) * head_dim] = ((x1 * sin + x2 * cos) * scale).astype(o_ref.dtype)


def _epi_headnorm(acc, g_ref, o_ref, *, head_dim, scale):
    g = g_ref[...]
    for h in range(acc.shape[1] // head_dim):
        x = acc[:, h * head_dim:(h + 1) * head_dim]
        y = x * lax.rsqrt(jnp.mean(x * x, axis=-1, keepdims=True) + EPS) * g
        o_ref[:, h * head_dim:(h + 1) * head_dim] = (y * scale).astype(o_ref.dtype)


def _epi_logsigmoid(acc, b_ref, o_ref):
    z = acc + b_ref[...]
    o_ref[...] = jnp.minimum(z, 0.0) - jnp.log1p(jnp.exp(-jnp.abs(z)))


def _mm_kernel(x_ref, w_ref, *rest, epilogue, nk):
    if nk == 1:
        epilogue(jnp.dot(x_ref[...], w_ref[...], preferred_element_type=F32), *rest)
        return
    acc_ref = rest[-1]
    k = pl.program_id(2)

    @pl.when(k == 0)
    def _():
        acc_ref[...] = jnp.zeros_like(acc_ref)

    acc_ref[...] += jnp.dot(x_ref[...], w_ref[...], preferred_element_type=F32)

    @pl.when(k == nk - 1)
    def _():
        epilogue(acc_ref[...], *rest[:-1])


def _mm_tiles(m, k, n, out_bytes, extra_bytes_per_elem):
    tm = _divisor_tile(m, 1024, 16)
    for tn in (1024, 512, 256, 128):
        if n % tn:
            continue
        for nk in (1, 2, 4, 8):
            if k % nk or (k // nk) % LANE:
                continue
            tk = k // nk
            ws = 2 * (tm * tk * 2 + tk * tn * 2 + tm * tn * (out_bytes + extra_bytes_per_elem))
            ws += tm * tn * 4
            if ws <= VMEM_BUDGET:
                return tm, tn, tk, ws
    raise ValueError(f"no matmul tiling for {(m, k, n)}")


def matmul(x, w, *, col_off=0, n_out=None, epilogue=_epi_plain, extras=(), extra_specs=lambda tm, tn: [],
           out_dtype=F32, extra_bytes_per_elem=0, row_tile=None):
    m, k = x.shape
    n_out = w.shape[1] - col_off if n_out is None else n_out
    tm, tn, tk, ws = _mm_tiles(m, k, n_out, jnp.dtype(out_dtype).itemsize, extra_bytes_per_elem)
    if row_tile is not None:
        tm = row_tile
    assert col_off % tn == 0 and m % tm == 0
    nk = k // tk
    joff = col_off // tn
    kern = functools.partial(_mm_kernel, epilogue=epilogue, nk=nk)
    return pl.pallas_call(
        kern,
        grid=(m // tm, n_out // tn, nk),
        in_specs=[pl.BlockSpec((tm, tk), lambda i, j, kk: (i, kk)),
                  pl.BlockSpec((tk, tn), lambda i, j, kk: (kk, j + joff)),
                  *extra_specs(tm, tn)],
        out_specs=pl.BlockSpec((tm, tn), lambda i, j, kk: (i, j)),
        out_shape=jax.ShapeDtypeStruct((m, n_out), out_dtype),
        scratch_shapes=[pltpu.VMEM((tm, tn), F32)] if nk > 1 else [],
        compiler_params=_params(("parallel", "parallel", "arbitrary"), ws),
        name="matmul_" + getattr(epilogue, "func", epilogue).__name__.lstrip("_"),
    )(x, w, *extras)


def matmul_residual(x, w, res):
    return matmul(x, w, epilogue=_epi_residual, extras=(res,), extra_bytes_per_elem=4,
                  extra_specs=lambda tm, tn: [pl.BlockSpec((tm, tn), lambda i, j, kk: (i, j))])


def matmul_rope(x, w, cos, sin, *, col_off, n_out, head_dim, scale, period_rows):
    half = head_dim // 2

    def specs(tm, tn):
        assert period_rows % tm == 0 and tn % head_dim == 0
        nper = period_rows // tm
        return [pl.BlockSpec((tm, half), lambda i, j, kk: (i % nper, 0))] * 2

    row_tile = _divisor_tile(period_rows, _divisor_tile(x.shape[0], 1024, 16), 16)
    return matmul(x, w, col_off=col_off, n_out=n_out, out_dtype=BF16, extras=(cos, sin), extra_specs=specs,
                  row_tile=row_tile,
                  epilogue=functools.partial(_epi_rope, head_dim=head_dim, scale=scale))


def matmul_headnorm(x, w, gain, *, col_off, n_out, head_dim, scale, out_dtype):
    return matmul(x, w, col_off=col_off, n_out=n_out, out_dtype=out_dtype, extras=(gain.reshape(1, head_dim),),
                  extra_specs=lambda tm, tn: [pl.BlockSpec((1, head_dim), lambda i, j, kk: (0, 0))],
                  epilogue=functools.partial(_epi_headnorm, head_dim=head_dim, scale=scale))


def matmul_logsigmoid(x, w, bias):
    return matmul(x, w, extras=(bias,), epilogue=_epi_logsigmoid,
                  extra_specs=lambda tm, tn: [pl.BlockSpec((1, tn), lambda i, j, kk: (0, j))])


def _retention_kernel(lg_ref, q_ref, k_ref, v_ref, gate_ref, s0_ref, og_ref, sout_ref, s_scr, decay_scr, *,
                      nc, hps):
    hg = pl.program_id(1)
    c = pl.program_id(2)
    t = q_ref.shape[0]
    dk = q_ref.shape[1] // hps
    dv = v_ref.shape[1] // hps
    idx = lax.broadcasted_iota(jnp.int32, (t, 1), 0).astype(F32)

    @pl.when(c == 0)
    def _():
        rel = (lax.broadcasted_iota(jnp.int32, (t, t), 0) - lax.broadcasted_iota(jnp.int32, (t, t), 1)).astype(F32)
        causal = rel >= 0.0
        for hh in range(hps):
            s_scr[hh] = s0_ref[0, hh]
            decay_scr[hh] = jnp.where(causal, jnp.exp(jnp.where(causal, rel, 0.0) * lg_ref[hg * hps + hh]), 0.0)

    for hh in range(hps):
        lg = lg_ref[hg * hps + hh]
        q = q_ref[:, hh * dk:(hh + 1) * dk]
        k = k_ref[:, hh * dk:(hh + 1) * dk]
        v = v_ref[:, hh * dv:(hh + 1) * dv]
        scores = lax.dot_general(q, k, (((1,), (1,)), ((), ())), preferred_element_type=F32) * decay_scr[hh]
        o = jnp.dot(scores.astype(BF16), v, preferred_element_type=F32)
        s_old = s_scr[hh]
        q_dec = (q.astype(F32) * jnp.exp((idx + 1.0) * lg)).astype(BF16)
        o = o + jnp.dot(q_dec, s_old.astype(BF16), preferred_element_type=F32)
        k_dec = (k.astype(F32) * jnp.exp((t - 1.0 - idx) * lg)).astype(BF16)
        kv = lax.dot_general(k_dec, v, (((0,), (0,)), ((), ())), preferred_element_type=F32)
        s_new = jnp.exp(jnp.full((1, 1), t, F32) * lg) * s_old + kv
        s_scr[hh] = s_new
        on = o * lax.rsqrt(jnp.mean(o * o, axis=-1, keepdims=True) + EPS)
        g = gate_ref[:, hh * dv:(hh + 1) * dv].astype(F32)
        og_ref[:, hh * dv:(hh + 1) * dv] = ((g * jax.nn.sigmoid(g)) * on).astype(og_ref.dtype)

        @pl.when(c == nc - 1)
        def _():
            sout_ref[0, hh] = s_new


def retention(log_g, q, k, v, gate, s0, *, nb, rows, chunk, row_off, heads):
    dk = q.shape[1] // heads
    dv = v.shape[1] // heads
    nc = rows // chunk
    hps = 2 if heads % 2 == 0 else 1
    assert rows % chunk == 0 and row_off % chunk == 0
    roff = row_off // chunk
    shared = s0.shape[0] == 1
    row_map = lambda b, h, c: (roff + b * nc + c, h)
    vmem = hps * (2 * chunk * (2 * dk * 2 + 3 * dv * 2) + 5 * dk * dv * 4 + 8 * chunk * chunk * 4)
    og, s_out = pl.pallas_call(
        functools.partial(_retention_kernel, nc=nc, hps=hps),
        grid=(nb, heads // hps, nc),
        in_specs=[pl.BlockSpec(memory_space=pltpu.SMEM),
                  pl.BlockSpec((chunk, hps * dk), row_map),
                  pl.BlockSpec((chunk, hps * dk), row_map),
                  pl.BlockSpec((chunk, hps * dv), row_map),
                  pl.BlockSpec((chunk, hps * dv), row_map),
                  pl.BlockSpec((1, hps, dk, dv), lambda b, h, c: (0 if shared else b, h, 0, 0))],
        out_specs=[pl.BlockSpec((chunk, hps * dv), lambda b, h, c: (b * nc + c, h)),
                   pl.BlockSpec((1, hps, dk, dv), lambda b, h, c: (b, h, 0, 0))],
        out_shape=[jax.ShapeDtypeStruct((nb * rows, heads * dv), BF16),
                   jax.ShapeDtypeStruct((nb, heads, dk, dv), F32)],
        scratch_shapes=[pltpu.VMEM((hps, dk, dv), F32), pltpu.VMEM((hps, chunk, chunk), F32)],
        compiler_params=_params(("parallel", "parallel", "arbitrary"), vmem),
        name="retention",
    )(log_g, q, k, v, gate, s0)
    return og, s_out


def _conv_silu_gate(a, u, halo, cw, cb):
    r1 = pltpu.roll(a, 1, axis=0)
    r2 = pltpu.roll(a, 2, axis=0)
    row = lax.broadcasted_iota(jnp.int32, (SUBLANE, a.shape[1]), 0)
    top1 = jnp.where(row == 0, halo[7:8, :], r1[:SUBLANE])
    top2 = jnp.where(row == 0, halo[6:7, :], jnp.where(row == 1, halo[7:8, :], r2[:SUBLANE]))
    prev1 = jnp.concatenate([top1, r1[SUBLANE:]], axis=0)
    prev2 = jnp.concatenate([top2, r2[SUBLANE:]], axis=0)
    c = cb + cw[0:1, :] * prev2
    c = c + cw[1:2, :] * prev1
    c = c + cw[2:3, :] * a
    return (c * jax.nn.sigmoid(c)) * u


def _conv_gate_kernel(a_ref, u_ref, halo_ref, state_ref, cw_ref, cb_ref, o_ref, *, tiles_per_stream):
    first = (pl.program_id(0) % tiles_per_stream) == 0
    halo = jnp.where(first, state_ref[0], halo_ref[...])
    o_ref[...] = _conv_silu_gate(a_ref[...], u_ref[...], halo, cw_ref[...], cb_ref[...]).astype(o_ref.dtype)


def conv_gate(a, u, state, cw, cb, *, rows_per_stream, row_tile):
    m, f = a.shape
    tr = row_tile
    tc = _divisor_tile(f, 1024, LANE)
    assert rows_per_stream % tr == 0 and tr % SUBLANE == 0
    tps = rows_per_stream // tr
    nfb = f // tc
    hb = tr // SUBLANE
    vmem = 2 * (tr * tc * (4 + 4 + 2) + 3 * 8 * tc * 4) + 6 * tr * tc * 4
    return pl.pallas_call(
        functools.partial(_conv_gate_kernel, tiles_per_stream=tps),
        grid=(m // tr, nfb),
        in_specs=[pl.BlockSpec((tr, tc), lambda i, j: (i, j)),
                  pl.BlockSpec((tr, tc), lambda i, j: (i, j)),
                  pl.BlockSpec((SUBLANE, tc), lambda i, j: (jnp.maximum(i * hb - 1, 0), j)),
                  pl.BlockSpec((1, SUBLANE, tc), lambda i, j: (i // tps, 0, j)),
                  pl.BlockSpec((SUBLANE, tc), lambda i, j: (0, j)),
                  pl.BlockSpec((1, tc), lambda i, j: (0, j))],
        out_specs=pl.BlockSpec((tr, tc), lambda i, j: (i, j)),
        out_shape=jax.ShapeDtypeStruct((m, f), BF16),
        compiler_params=_params(("parallel", "parallel"), vmem),
        name="conv_gate",
    )(a, u, a, state, cw, cb)


def _gate_up_conv_kernel(x_ref, *refs, tiles_per_stream, nsub, blocks):
    wa_refs, wu_refs = refs[:nsub], refs[nsub:2 * nsub]
    state_ref, cw_ref, cb_ref, o_ref, tail_ref, halo_scr = refs[2 * nsub:]
    i = pl.program_id(0)
    j = pl.program_id(1)
    first = (i % tiles_per_stream) == 0
    tm = x_ref.shape[0]
    for s in range(nsub):
        cols = slice(s * MXU_COLS, (s + 1) * MXU_COLS)
        a = jnp.dot(x_ref[...], wa_refs[s][...], preferred_element_type=F32)
        u = jnp.dot(x_ref[...], wu_refs[s][...], preferred_element_type=F32)
        halo = jnp.where(first, state_ref[0, :, cols], halo_scr[j, :, cols])
        act = _conv_silu_gate(a, u, halo, cw_ref[:, cols], cb_ref[:, cols])
        o_ref[:, cols] = jnp.where(j * nsub + s < blocks, act, 0.0).astype(o_ref.dtype)
        tail = a[tm - SUBLANE:, :]
        halo_scr[j, :, cols] = tail
        tail_ref[0, :, cols] = tail


def matmul_conv_gate(x, w, state, cw, cb, *, rows_per_stream):
    m, k = x.shape
    f = w.shape[1] // 2
    fp = cw.shape[1]
    tm = _divisor_tile(rows_per_stream, 1024, 16)
    nsub = 2
    tn = nsub * MXU_COLS
    assert f % MXU_COLS == 0 and fp % tn == 0 and m % rows_per_stream == 0
    blocks = f // MXU_COLS
    tps = rows_per_stream // tm
    nfb = fp // tn
    vmem = 2 * (tm * k * 2 + 2 * k * tn * 2 + tm * tn * 2) + SUBLANE * fp * 4 + 8 * tm * MXU_COLS * 4

    def weight_spec(branch, s):
        return pl.BlockSpec((k, MXU_COLS), lambda i, j: (0, branch * blocks + jnp.minimum(j * nsub + s, blocks - 1)))

    act, tails = pl.pallas_call(
        functools.partial(_gate_up_conv_kernel, tiles_per_stream=tps, nsub=nsub, blocks=blocks),
        grid=(m // tm, nfb),
        in_specs=[pl.BlockSpec((tm, k), lambda i, j: (i, 0)),
                  *[weight_spec(0, s) for s in range(nsub)],
                  *[weight_spec(1, s) for s in range(nsub)],
                  pl.BlockSpec((1, SUBLANE, tn), lambda i, j: (i // tps, 0, j)),
                  pl.BlockSpec((SUBLANE, tn), lambda i, j: (0, j)),
                  pl.BlockSpec((1, tn), lambda i, j: (0, j))],
        out_specs=[pl.BlockSpec((tm, tn), lambda i, j: (i, j)),
                   pl.BlockSpec((1, SUBLANE, tn), lambda i, j: (i, 0, j))],
        out_shape=[jax.ShapeDtypeStruct((m, fp), BF16),
                   jax.ShapeDtypeStruct((m // tm, SUBLANE, fp), F32)],
        scratch_shapes=[pltpu.VMEM((nfb, SUBLANE, tn), F32)],
        compiler_params=_params(("arbitrary", "arbitrary"), vmem),
        name="matmul_gate_up_conv",
    )(x, *([w] * (2 * nsub)), state, cw, cb)
    return act, tails[tps - 1::tps]


def _cumsum_kernel(x_ref, c0_ref, o_ref, *, blk):
    t = x_ref.shape[1]
    r = lax.broadcasted_iota(jnp.int32, (blk, blk), 0)
    s = lax.broadcasted_iota(jnp.int32, (blk, blk), 1)
    tri = (s <= r).astype(BF16)
    carry = c0_ref[0]
    for b in range(t // blk):
        x = x_ref[0, b * blk:(b + 1) * blk, :]
        x_hi = x.astype(BF16)
        r1 = x - x_hi.astype(F32)
        x_mid = r1.astype(BF16)
        x_lo = (r1 - x_mid.astype(F32)).astype(BF16)
        c = jnp.dot(tri, x_hi, preferred_element_type=F32)
        c = c + jnp.dot(tri, x_mid, preferred_element_type=F32)
        c = c + jnp.dot(tri, x_lo, preferred_element_type=F32)
        c = c + carry
        o_ref[0, b * blk:(b + 1) * blk, :] = c
        carry = c[blk - 1:blk, :]


def cumsum_rows(x, c0):
    nb, t, l = x.shape
    blk = _divisor_tile(t, 256, 16)
    shared = c0.shape[0] == 1
    return pl.pallas_call(
        functools.partial(_cumsum_kernel, blk=blk),
        grid=(nb,),
        in_specs=[pl.BlockSpec((1, t, l), lambda b: (b, 0, 0)),
                  pl.BlockSpec((1, 1, l), lambda b: (0 if shared else b, 0, 0))],
        out_specs=pl.BlockSpec((1, t, l), lambda b: (b, 0, 0)),
        out_shape=jax.ShapeDtypeStruct((nb, t, l), F32),
        compiler_params=_params(("parallel",), 4 * t * l * 4 + 2 ** 20),
        name="cumsum",
    )(x, c0)


def _fox_kernel(*refs, tq, prefix):
    m_scr, cq_scr, alpha_scr, acc_scr, sa_scr, sb_scr, pa_scr, pb_scr, kb_scr, v1_scr = refs[-10:]
    if prefix:
        q_ref, cq_ref, km_ref, vm_ref, ckm_ref, kp_ref, vp_ref, ckp_ref, o_ref = refs[:-10]
    else:
        q_ref, cq_ref, km_ref, vm_ref, ckm_ref, o_ref = refs[:-10]
    hd = q_ref.shape[1]
    nq = q_ref.shape[0] // tq

    def lanes(x, width):
        return jnp.tile(x, (1, width // LANE)) if width % LANE == 0 else x[:, :1]

    def ones_columns(vb):
        return jnp.concatenate([vb.astype(BF16), jnp.ones((vb.shape[0], LANE), BF16)], axis=1)

    for c in range(nq):
        rows = slice(c * tq, (c + 1) * tq)
        kb_scr[rows, :] = km_ref[rows, :].astype(BF16)
        v1_scr[rows, :] = ones_columns(vm_ref[rows, :])

    def block(j):
        return pl.ds(pl.multiple_of(j * tq, tq), tq)

    head_lane = lax.broadcasted_iota(jnp.int32, (tq, LANE), 1) == pl.program_id(1)

    def query_block(qi, carry):
        q = q_ref[block(qi), :]
        m_scr[...] = jnp.full(m_scr.shape, MASK_VALUE, F32)
        acc_scr[...] = jnp.zeros(acc_scr.shape, F32)
        cq = jnp.sum(jnp.where(head_lane, cq_ref[0, block(qi), :], 0.0), axis=-1, keepdims=True)
        cq_scr[...] = jnp.broadcast_to(cq, cq_scr.shape)

        def scores(s_ref, kb):
            s_ref[:, :kb.shape[0]] = lax.dot_general(q, kb, (((1,), (1,)), ((), ())), preferred_element_type=F32)

        def softmax(s_ref, p_ref, ck, causal):
            width = ck.shape[1]
            rb = min(tq, max(16, 32768 // max(width, LANE)))
            for r in range(tq // rb):
                rows = slice(r * rb, (r + 1) * rb)
                s = s_ref[rows, :width] - ck
                if causal:
                    row = r * rb + lax.broadcasted_iota(jnp.int32, (rb, width), 0)
                    s = jnp.where(lax.broadcasted_iota(jnp.int32, (rb, width), 1) <= row, s, MASK_VALUE)
                cq = cq_scr[rows, :]
                m_old = m_scr[rows, :]
                m_new = jnp.maximum(m_old, jnp.max(s, axis=-1, keepdims=True) + cq)
                p_ref[rows, :width] = jnp.exp2(s - lanes(m_new - cq, width)).astype(BF16)
                m_scr[rows, :] = m_new
                alpha_scr[rows, :] = jnp.exp2(m_old - m_new)

        def accumulate(p, v1, rescale):
            pv = jnp.dot(p, v1, preferred_element_type=F32)
            if rescale:
                acc_scr[...] = (acc_scr[...] + pv) * jnp.tile(alpha_scr[...], (1, 2))
            else:
                acc_scr[...] += pv

        if prefix:
            pw = kp_ref.shape[1]
            scores(sa_scr, kp_ref[0].astype(BF16))
            softmax(sa_scr, pa_scr, ckp_ref[0], False)
            accumulate(pa_scr[:, :pw], ones_columns(vp_ref[0]), False)

        scores(sa_scr, kb_scr[block(0), :])
        pb_scr[...] = jnp.zeros(pb_scr.shape, BF16)

        def pair(t, carry):
            j = 2 * t
            scores(sb_scr, kb_scr[block(j + 1), :])
            softmax(sa_scr, pa_scr, ckm_ref[0, j], False)
            accumulate(pb_scr[:, :tq], v1_scr[block(jnp.maximum(j - 1, 0)), :], True)
            scores(sa_scr, kb_scr[block(j + 2), :])
            softmax(sb_scr, pb_scr, ckm_ref[0, j + 1], False)
            accumulate(pa_scr[:, :tq], v1_scr[block(j), :], True)
            return carry

        lax.fori_loop(0, qi // 2, pair, 0)
        last = jnp.maximum(qi - 1, 0)

        @pl.when(qi % 2 == 0)
        def _():
            softmax(sa_scr, pa_scr, ckm_ref[0, qi], True)
            accumulate(pb_scr[:, :tq], v1_scr[block(last), :], True)
            accumulate(pa_scr[:, :tq], v1_scr[block(qi), :], False)

        @pl.when(qi % 2 == 1)
        def _():
            scores(sb_scr, kb_scr[block(qi), :])
            softmax(sa_scr, pa_scr, ckm_ref[0, last], False)
            accumulate(pb_scr[:, :tq], v1_scr[block(jnp.maximum(qi - 2, 0)), :], True)
            softmax(sb_scr, pb_scr, ckm_ref[0, qi], True)
            accumulate(pa_scr[:, :tq], v1_scr[block(last), :], True)
            accumulate(pb_scr[:, :tq], v1_scr[block(qi), :], False)

        o_ref[block(qi), :] = (acc_scr[:, :hd] / acc_scr[:, hd:]).astype(o_ref.dtype)
        return carry

    lax.fori_loop(0, nq, query_block, 0)


def fox_attention(q, cq, km, vm, ckm, prefix, *, nb, rows, row_off, heads, q_tile):
    hd = q.shape[1] // heads
    tq = q_tile
    nq = rows // tq
    assert rows % tq == 0 and row_off % rows == 0 and hd == LANE and heads <= LANE
    boff = row_off // rows
    in_specs = [pl.BlockSpec((rows, hd), lambda b, h: (boff + b, h)),
                pl.BlockSpec((1, rows, LANE), lambda b, h: (b, 0, 0)),
                pl.BlockSpec((rows, hd), lambda b, h: (boff + b, h)),
                pl.BlockSpec((rows, hd), lambda b, h: (boff + b, h)),
                pl.BlockSpec((1, nq, 1, tq), lambda b, h: (b * heads + h, 0, 0, 0))]
    args = [q, cq, km, vm, ckm.reshape(nb * heads, nq, 1, tq)]
    vmem = 2 * (2 * rows * hd * 2 + rows * LANE * 4 + 2 * rows * hd * 4 + 8 * rows * 4)
    width = tq
    if prefix is not None:
        kp, vp, ckp = prefix
        p = kp.shape[1]
        width = max(tq, p)
        shared = kp.shape[0] == 1
        in_specs += [pl.BlockSpec((1, p, hd), lambda b, h: (0 if shared else b, 0, h)),
                     pl.BlockSpec((1, p, hd), lambda b, h: (0 if shared else b, 0, h)),
                     pl.BlockSpec((1, 1, p), lambda b, h: (h if shared else b * heads + h, 0, 0))]
        args += [kp, vp, ckp]
        vmem += 2 * (2 * p * hd * 4 + 8 * p * 4)
    width = -(-width // LANE) * LANE
    vmem += tq * width * 12 + 5 * tq * LANE * 4 + rows * hd * 6 + 2 ** 21
    return pl.pallas_call(
        functools.partial(_fox_kernel, tq=tq, prefix=prefix is not None),
        grid=(nb, heads),
        in_specs=in_specs,
        out_specs=pl.BlockSpec((rows, hd), lambda b, h: (b, h)),
        out_shape=jax.ShapeDtypeStruct((nb * rows, heads * hd), BF16),
        scratch_shapes=[pltpu.VMEM((tq, LANE), F32), pltpu.VMEM((tq, LANE), F32), pltpu.VMEM((tq, LANE), F32),
                        pltpu.VMEM((tq, 2 * hd), F32),
                        pltpu.VMEM((tq, width), F32), pltpu.VMEM((tq, width), F32),
                        pltpu.VMEM((tq, width), BF16), pltpu.VMEM((tq, width), BF16),
                        pltpu.VMEM((rows, hd), BF16), pltpu.VMEM((rows, 2 * hd), BF16)],
        compiler_params=_params(("parallel", "parallel"), vmem),
        name="fox_attention",
    )(*args)


def _rope_tables(pos, half):
    inv = ROPE_BASE ** (-jnp.arange(half, dtype=F32) / half)
    ang = pos.astype(F32)[:, None] * inv[None, :]
    return jnp.cos(ang), jnp.sin(ang)


def _head_major(c, heads):
    nb, t, _ = c.shape
    c2 = c * LOG2E
    return c2, jnp.swapaxes(c2[:, :, :heads], 1, 2).reshape(nb * heads, 1, t)


def kernel(x_prompt, x_sample, cache_k, cache_v, cache_logf, state_ret, state_conv, meta, g_attn, g_ffn,
           w_ret_in, w_ret_out, g_kv, w_kvf, b_f, g_k, w_q_b, g_q, w_o_b, w_gu, conv_w, conv_b, w_down):
    nbp, seq, d = x_prompt.shape
    nbs, dseq, _ = x_sample.shape
    n_meta = meta.shape[0]
    past = cache_k.shape[1]
    depth = g_attn.shape[0]
    n_a = w_ret_in.shape[0]
    h_a, dk_a, dv_a = state_ret.shape[2:]
    h_b, hd_b = cache_k.shape[2:]
    qk_w, v_w = h_a * dk_a, h_a * dv_a
    d_ff = w_down.shape[1]
    cw_taps = conv_w.shape[1]
    assert n_a == 1 and depth == 2 and cw_taps == 3, "kernel is written for one retention + one attention layer"
    assert d_ff % MXU_COLS == 0
    ff_tile = 1024 if d_ff >= 4096 else 2 * MXU_COLS
    ffp = -(-d_ff // ff_tile) * ff_tile
    m_main = nbp * seq
    m_s = nbs * dseq
    m_sm = m_s + n_meta
    m_rest = -(-m_sm // dseq) * dseq
    dt = x_prompt.dtype

    w_in = w_ret_in[0].astype(BF16)
    w_out = w_ret_out[0].astype(BF16)
    w_kv = w_kvf[:, :2 * d].astype(BF16)
    w_f = jnp.pad(w_kvf[:, 2 * d:], ((0, 0), (0, LANE - h_b))).astype(BF16)
    b_fp = jnp.pad(b_f, (0, LANE - h_b)).reshape(1, LANE)
    w_q = w_q_b[0].astype(BF16)
    w_o = w_o_b[0].astype(BF16)
    padc = lambda a: jnp.pad(a, ((0, 0),) * (a.ndim - 1) + ((0, ffp - d_ff),))
    w_gu_b = [w_gu[l].astype(BF16) for l in range(depth)]
    w_dn_p = [jnp.pad(w_down[l], ((0, ffp - d_ff), (0, 0))).astype(BF16) for l in range(depth)]
    cw_p = [jnp.pad(padc(conv_w[l]), ((0, SUBLANE - cw_taps), (0, 0))) for l in range(depth)]
    cb_p = [padc(conv_b[l]).reshape(1, ffp) for l in range(depth)]
    log_g = jnp.log1p(-jnp.exp2(-5.0 - jnp.arange(h_a, dtype=F32)))

    x_main = x_prompt.reshape(m_main, d)
    pad_rows = lambda a: jnp.pad(a, ((0, m_rest - m_sm),) + ((0, 0),) * (a.ndim - 1))
    x_rest = pad_rows(jnp.concatenate([x_sample.reshape(m_s, d), meta.astype(dt)], axis=0))
    cos_m, sin_m = _rope_tables(n_meta + jnp.arange(seq), dk_a // 2)
    pos_rest = pad_rows(jnp.concatenate([jnp.tile(n_meta + past + jnp.arange(dseq), nbs), jnp.arange(n_meta)]))
    cos_r, sin_r = _rope_tables(pos_rest, dk_a // 2)

    def state_rows(rows2):
        return jnp.pad(padc(rows2), ((0, 0), (SUBLANE - 2, 0), (0, 0)))

    def conv_ffn(x, l, state):
        (hn,) = rmsnorm_rows(x, g_ffn[l:l + 1])
        act, tails = matmul_conv_gate(hn, w_gu_b[l], state, cw_p[l], cb_p[l], rows_per_stream=seq)
        return matmul_residual(act, w_dn_p[l], x), tails[:, SUBLANE - 2:, :d_ff]

    def ret_in(h, cos, sin, period):
        q = matmul_rope(h, w_in, cos, sin, col_off=0, n_out=qk_w, head_dim=dk_a, scale=1.0, period_rows=period)
        k = matmul_rope(h, w_in, cos, sin, col_off=qk_w, n_out=qk_w, head_dim=dk_a, scale=dk_a ** -0.5,
                        period_rows=period)
        v = matmul(h, w_in, col_off=2 * qk_w, n_out=v_w, out_dtype=BF16)
        gate = matmul(h, w_in, col_off=2 * qk_w + v_w, n_out=v_w, out_dtype=BF16)
        return q, k, v, gate

    (h_r,) = rmsnorm_rows(x_rest, g_attn[0:1])
    q_r, k_r, v_r, gate_r = ret_in(h_r, cos_r, sin_r, m_rest)
    og_s, ret_s = retention(log_g, q_r, k_r, v_r, gate_r, state_ret[0], nb=nbs, rows=dseq, chunk=dseq,
                            row_off=0, heads=h_a)
    og_m, ret_meta = retention(log_g, q_r, k_r, v_r, gate_r, jnp.zeros((1, h_a, dk_a, dv_a), F32), nb=1,
                               rows=n_meta, chunk=n_meta, row_off=m_s, heads=h_a)
    x_r1 = matmul_residual(pad_rows(jnp.concatenate([og_s, og_m], axis=0)), w_out, x_rest)
    assert n_meta <= dseq and n_meta % SUBLANE == 0
    state_r0 = jnp.concatenate([state_rows(state_conv[0]), jnp.zeros((1, SUBLANE, ffp), F32)], axis=0)

    def conv_ffn_rest(x, l, state):
        (hn,) = rmsnorm_rows(x, g_ffn[l:l + 1])
        a = matmul(hn, w_gu_b[l], col_off=0, n_out=d_ff)
        u = matmul(hn, w_gu_b[l], col_off=d_ff, n_out=d_ff)
        act = conv_gate(a, u, state[:, :, :d_ff], cw_p[l][:, :d_ff], cb_p[l][:, :d_ff], rows_per_stream=dseq,
                        row_tile=dseq)
        return matmul_residual(padc(act), w_dn_p[l], x), a

    x_r2, a_r0 = conv_ffn_rest(x_r1, 0, state_r0)

    (h_m,) = rmsnorm_rows(x_main, g_attn[0:1])
    q_m, k_m, v_m, gate_m = ret_in(h_m, cos_m, sin_m, seq)
    chunk = _divisor_tile(seq, 256, 16)
    og_main, ret_p = retention(log_g, q_m, k_m, v_m, gate_m, ret_meta, nb=nbp, rows=seq, chunk=chunk,
                               row_off=0, heads=h_a)
    x_m1 = matmul_residual(og_main, w_out, x_main)
    state_m0 = jnp.broadcast_to(state_rows(a_r0[None, m_sm - 2:m_sm, :d_ff]), (nbp, SUBLANE, ffp))
    x_m2, conv_p0 = conv_ffn(x_m1, 0, state_m0)

    def kv_side(x, g_row):
        h, hk = rmsnorm_rows(x, jnp.stack([g_row, g_kv]))
        kk = matmul_headnorm(hk, w_kv, g_k, col_off=0, n_out=d, head_dim=hd_b, scale=1.0, out_dtype=F32)
        vv = matmul(hk, w_kv, col_off=d, n_out=d)
        lf = matmul_logsigmoid(hk, w_f, b_fp)
        q = matmul_headnorm(h, w_q, g_q[0], col_off=0, n_out=d, head_dim=hd_b, scale=hd_b ** -0.5 * LOG2E,
                            out_dtype=BF16)
        return q, kk, vv, lf

    q_r, kk_r, vv_r, lf_r = kv_side(x_r2, g_attn[1])
    q_m, kk_m, vv_m, lf_m = kv_side(x_m2, g_attn[1])

    c_cache = cumsum_rows(jnp.pad(cache_logf, ((0, 0), (0, 0), (0, LANE - h_b))), jnp.zeros((1, 1, LANE), F32))
    c_s = cumsum_rows(lf_r[:m_s].reshape(nbs, dseq, LANE), c_cache[:, past - 1:past, :])
    c_meta = cumsum_rows(lf_r[m_s:m_sm].reshape(1, n_meta, LANE), jnp.zeros((1, 1, LANE), F32))
    c_main = cumsum_rows(lf_m.reshape(nbp, seq, LANE), c_meta[:, n_meta - 1:n_meta, :])

    cq_s, ck_s = _head_major(c_s, h_b)
    _, ck_cache = _head_major(c_cache, h_b)
    o_s = fox_attention(q_r, cq_s, kk_r, vv_r, ck_s,
                        (cache_k.reshape(nbs, past, d), cache_v.reshape(nbs, past, d), ck_cache),
                        nb=nbs, rows=dseq, row_off=0, heads=h_b, q_tile=dseq)
    cq_meta, ck_meta = _head_major(c_meta, h_b)
    o_meta = fox_attention(q_r, cq_meta, kk_r, vv_r, ck_meta, None,
                           nb=1, rows=n_meta, row_off=m_s, heads=h_b, q_tile=n_meta)
    cq_main, ck_main = _head_major(c_main, h_b)
    k_meta = kk_r[m_s:m_sm]
    v_meta = vv_r[m_s:m_sm]
    o_main = fox_attention(q_m, cq_main, kk_m, vv_m, ck_main, (k_meta[None], v_meta[None], ck_meta),
                           nb=nbp, rows=seq, row_off=0, heads=h_b, q_tile=_divisor_tile(seq, 512, 16))

    x_r3 = matmul_residual(pad_rows(jnp.concatenate([o_s, o_meta], axis=0)), w_o, x_r2)
    state_r1 = jnp.concatenate([state_rows(state_conv[1]), jnp.zeros((1, SUBLANE, ffp), F32)], axis=0)
    x_r4, a_r1 = conv_ffn_rest(x_r3, 1, state_r1)
    x_m3 = matmul_residual(o_main, w_o, x_m2)
    state_m1 = jnp.broadcast_to(state_rows(a_r1[None, m_sm - 2:m_sm, :d_ff]), (nbp, SUBLANE, ffp))
    x_m4, conv_p1 = conv_ffn(x_m3, 1, state_m1)

    def with_meta(meta_rows, main_rows, tail):
        mr = jnp.broadcast_to(meta_rows.reshape((1, n_meta) + tail), (nbp, n_meta) + tail)
        return jnp.concatenate([mr, main_rows.reshape((nbp, seq) + tail)], axis=1)

    y_prompt = x_m4.reshape(nbp, seq, d)
    y_sample = x_r4[:m_s].reshape(nbs, dseq, d)
    k_prompt = with_meta(k_meta, kk_m, (h_b, hd_b))
    v_prompt = with_meta(v_meta, vv_m, (h_b, hd_b))
    logf_prompt = with_meta(lf_r[m_s:m_sm, :h_b], lf_m[:, :h_b], (h_b,))
    conv_prompt = jnp.stack([conv_p0, conv_p1])
    conv_sample = jnp.stack([a[:m_s].reshape(nbs, dseq, d_ff)[:, dseq - 2:] for a in (a_r0, a_r1)])
    return (y_prompt, y_sample, k_prompt, v_prompt, logf_prompt, ret_p[None], conv_prompt,
            kk_r[:m_s].reshape(nbs, dseq, h_b, hd_b), vv_r[:m_s].reshape(nbs, dseq, h_b, hd_b),
            lf_r[:m_s, :h_b].reshape(nbs, dseq, h_b), ret_s[None], conv_sample)
```

```python
import functools

import jax
import jax.numpy as jnp
from jax import lax
from jax.experimental import pallas as pl
from jax.experimental.pallas import tpu as pltpu

EPS = 1e-6
LOG2E = 1.4426950408889634
ROPE_BASE = 10000.0
MASK_VALUE = -1e30
LANE = 128
SUBLANE = 8
MXU_COLS = 256
VMEM_BUDGET = 44 * 2 ** 20
F32 = jnp.float32
BF16 = jnp.bfloat16


def _params(semantics, vmem_bytes):
    return pltpu.CompilerParams(dimension_semantics=semantics,
                                vmem_limit_bytes=int(min(vmem_bytes + 8 * 2 ** 20, 60 * 2 ** 20)))


def _divisor_tile(n, cap, mult):
    if n <= cap:
        return n
    t = cap - cap % mult
    while t >= mult:
        if n % t == 0:
            return t
        t -= mult
    raise ValueError(f"no tile for {n} (cap {cap}, multiple {mult})")


def _rmsnorm_kernel(x_ref, g_ref, *o_refs):
    x = x_ref[...]
    y = x * lax.rsqrt(jnp.mean(x * x, axis=-1, keepdims=True) + EPS)
    for n, o_ref in enumerate(o_refs):
        o_ref[...] = (y * g_ref[n:n + 1, :]).astype(o_ref.dtype)


def rmsnorm_rows(x, gains):
    m, d = x.shape
    g = gains.shape[0]
    tm = _divisor_tile(m, 256, 16)
    vmem = 2 * tm * d * (4 + 2 * g)
    outs = pl.pallas_call(
        _rmsnorm_kernel,
        grid=(m // tm,),
        in_specs=[pl.BlockSpec((tm, d), lambda i: (i, 0)),
                  pl.BlockSpec((g, d), lambda i: (0, 0))],
        out_specs=[pl.BlockSpec((tm, d), lambda i: (i, 0)) for _ in range(g)],
        out_shape=[jax.ShapeDtypeStruct((m, d), BF16) for _ in range(g)],
        compiler_params=_params(("parallel",), vmem),
        name="rmsnorm",
    )(x, gains)
    return tuple(outs)


def _epi_plain(acc, o_ref):
    o_ref[...] = acc.astype(o_ref.dtype)


def _epi_residual(acc, res_ref, o_ref):
    o_ref[...] = res_ref[...] + acc


def _epi_rope(acc, cos_ref, sin_ref, o_ref, *, head_dim, scale):
    half = head_dim // 2
    cos = cos_ref[...]
    sin = sin_ref[...]
    for h in range(acc.shape[1] // head_dim):
        x1 = acc[:, h * head_dim:h * head_dim + half]
        x2 = acc[:, h * head_dim + half:(h + 1) * head_dim]
        o_ref[:, h * head_dim:h * head_dim + half] = ((x1 * cos - x2 * sin) * scale).astype(o_ref.dtype)
        o_ref[:, h * head_dim + half:(h + 1) * head_dim] = ((x1 * sin + x2 * cos) * scale).astype(o_ref.dtype)


def _epi_headnorm(acc, g_ref, o_ref, *, head_dim, scale):
    g = g_ref[...]
    for h in range(acc.shape[1] // head_dim):
        x = acc[:, h * head_dim:(h + 1) * head_dim]
        y = x * lax.rsqrt(jnp.mean(x * x, axis=-1, keepdims=True) + EPS) * g
        o_ref[:, h * head_dim:(h + 1) * head_dim] = (y * scale).astype(o_ref.dtype)


def _epi_logsigmoid(acc, b_ref, o_ref):
    z = acc + b_ref[...]
    o_ref[...] = jnp.minimum(z, 0.0) - jnp.log1p(jnp.exp(-jnp.abs(z)))


def _mm_kernel(x_ref, w_ref, *rest, epilogue, nk):
    if nk == 1:
        epilogue(jnp.dot(x_ref[...], w_ref[...], preferred_element_type=F32), *rest)
        return
    acc_ref = rest[-1]
    k = pl.program_id(2)

    @pl.when(k == 0)
    def _():
        acc_ref[...] = jnp.zeros_like(acc_ref)

    acc_ref[...] += jnp.dot(x_ref[...], w_ref[...], preferred_element_type=F32)

    @pl.when(k == nk - 1)
    def _():
        epilogue(acc_ref[...], *rest[:-1])


def _mm_tiles(m, k, n, out_bytes, extra_bytes_per_elem):
    tm = _divisor_tile(m, 1024, 16)
    for tn in (1024, 512, 256, 128):
        if n % tn:
            continue
        for nk in (1, 2, 4, 8):
            if k % nk or (k // nk) % LANE:
                continue
            tk = k // nk
            ws = 2 * (tm * tk * 2 + tk * tn * 2 + tm * tn * (out_bytes + extra_bytes_per_elem))
            ws += tm * tn * 4
            if ws <= VMEM_BUDGET:
                return tm, tn, tk, ws
    raise ValueError(f"no matmul tiling for {(m, k, n)}")


def matmul(x, w, *, col_off=0, n_out=None, epilogue=_epi_plain, extras=(), extra_specs=lambda tm, tn: [],
           out_dtype=F32, extra_bytes_per_elem=0, row_tile=None, layer=None):
    m, k = x.shape
    n_out = w.shape[-1] - col_off if n_out is None else n_out
    tm, tn, tk, ws = _mm_tiles(m, k, n_out, jnp.dtype(out_dtype).itemsize, extra_bytes_per_elem)
    if row_tile is not None:
        tm = row_tile
    assert col_off % tn == 0 and m % tm == 0
    nk = k // tk
    joff = col_off // tn
    kern = functools.partial(_mm_kernel, epilogue=epilogue, nk=nk)
    if layer is None:
        w_spec = pl.BlockSpec((tk, tn), lambda i, j, kk: (kk, j + joff))
    else:
        w_spec = pl.BlockSpec((None, tk, tn), lambda i, j, kk: (layer, kk, j + joff))
    return pl.pallas_call(
        kern,
        grid=(m // tm, n_out // tn, nk),
        in_specs=[pl.BlockSpec((tm, tk), lambda i, j, kk: (i, kk)), w_spec, *extra_specs(tm, tn)],
        out_specs=pl.BlockSpec((tm, tn), lambda i, j, kk: (i, j)),
        out_shape=jax.ShapeDtypeStruct((m, n_out), out_dtype),
        scratch_shapes=[pltpu.VMEM((tm, tn), F32)] if nk > 1 else [],
        compiler_params=_params(("parallel", "parallel", "arbitrary"), ws),
        name="matmul_" + getattr(epilogue, "func", epilogue).__name__.lstrip("_"),
    )(x, w, *extras)


def matmul_residual(x, w, res, layer=None):
    return matmul(x, w, epilogue=_epi_residual, extras=(res,), extra_bytes_per_elem=4, layer=layer,
                  extra_specs=lambda tm, tn: [pl.BlockSpec((tm, tn), lambda i, j, kk: (i, j))])


def matmul_rope(x, w, cos, sin, *, col_off, n_out, head_dim, scale, period_rows):
    half = head_dim // 2

    def specs(tm, tn):
        assert period_rows % tm == 0 and tn % head_dim == 0
        nper = period_rows // tm
        return [pl.BlockSpec((tm, half), lambda i, j, kk: (i % nper, 0))] * 2

    row_tile = _divisor_tile(period_rows, _divisor_tile(x.shape[0], 1024, 16), 16)
    return matmul(x, w, col_off=col_off, n_out=n_out, out_dtype=BF16, extras=(cos, sin), extra_specs=specs,
                  row_tile=row_tile,
                  epilogue=functools.partial(_epi_rope, head_dim=head_dim, scale=scale))


def matmul_headnorm(x, w, gain, *, col_off, n_out, head_dim, scale, out_dtype):
    return matmul(x, w, col_off=col_off, n_out=n_out, out_dtype=out_dtype, extras=(gain.reshape(1, head_dim),),
                  extra_specs=lambda tm, tn: [pl.BlockSpec((1, head_dim), lambda i, j, kk: (0, 0))],
                  epilogue=functools.partial(_epi_headnorm, head_dim=head_dim, scale=scale))


def matmul_logsigmoid(x, w, bias):
    return matmul(x, w, extras=(bias,), epilogue=_epi_logsigmoid,
                  extra_specs=lambda tm, tn: [pl.BlockSpec((1, tn), lambda i, j, kk: (0, j))])


def _retention_kernel(lg_ref, q_ref, k_ref, v_ref, gate_ref, s0_ref, og_ref, sout_ref, s_scr, decay_scr, *,
                      nc, hps):
    hg = pl.program_id(1)
    c = pl.program_id(2)
    t = q_ref.shape[0]
    dk = q_ref.shape[1] // hps
    dv = v_ref.shape[1] // hps
    idx = lax.broadcasted_iota(jnp.int32, (t, 1), 0).astype(F32)

    @pl.when(c == 0)
    def _():
        rel = (lax.broadcasted_iota(jnp.int32, (t, t), 0) - lax.broadcasted_iota(jnp.int32, (t, t), 1)).astype(F32)
        causal = rel >= 0.0
        for hh in range(hps):
            s_scr[hh] = s0_ref[0, hh]
            decay_scr[hh] = jnp.where(causal, jnp.exp(jnp.where(causal, rel, 0.0) * lg_ref[hg * hps + hh]), 0.0)

    for hh in range(hps):
        lg = lg_ref[hg * hps + hh]
        q = q_ref[:, hh * dk:(hh + 1) * dk]
        k = k_ref[:, hh * dk:(hh + 1) * dk]
        v = v_ref[:, hh * dv:(hh + 1) * dv]
        scores = lax.dot_general(q, k, (((1,), (1,)), ((), ())), preferred_element_type=F32) * decay_scr[hh]
        o = jnp.dot(scores.astype(BF16), v, preferred_element_type=F32)
        s_old = s_scr[hh]
        q_dec = (q.astype(F32) * jnp.exp((idx + 1.0) * lg)).astype(BF16)
        o = o + jnp.dot(q_dec, s_old.astype(BF16), preferred_element_type=F32)
        k_dec = (k.astype(F32) * jnp.exp((t - 1.0 - idx) * lg)).astype(BF16)
        kv = lax.dot_general(k_dec, v, (((0,), (0,)), ((), ())), preferred_element_type=F32)
        s_new = jnp.exp(jnp.full((1, 1), t, F32) * lg) * s_old + kv
        s_scr[hh] = s_new
        on = o * lax.rsqrt(jnp.mean(o * o, axis=-1, keepdims=True) + EPS)
        g = gate_ref[:, hh * dv:(hh + 1) * dv].astype(F32)
        og_ref[:, hh * dv:(hh + 1) * dv] = ((g * jax.nn.sigmoid(g)) * on).astype(og_ref.dtype)

        @pl.when(c == nc - 1)
        def _():
            sout_ref[0, hh] = s_new


def retention(log_g, q, k, v, gate, s0, *, nb, rows, chunk, row_off, heads):
    dk = q.shape[1] // heads
    dv = v.shape[1] // heads
    nc = rows // chunk
    hps = 2 if heads % 2 == 0 else 1
    assert rows % chunk == 0 and row_off % chunk == 0
    roff = row_off // chunk
    shared = s0.shape[0] == 1
    row_map = lambda b, h, c: (roff + b * nc + c, h)
    vmem = hps * (2 * chunk * (2 * dk * 2 + 3 * dv * 2) + 5 * dk * dv * 4 + 8 * chunk * chunk * 4)
    og, s_out = pl.pallas_call(
        functools.partial(_retention_kernel, nc=nc, hps=hps),
        grid=(nb, heads // hps, nc),
        in_specs=[pl.BlockSpec(memory_space=pltpu.SMEM),
                  pl.BlockSpec((chunk, hps * dk), row_map),
                  pl.BlockSpec((chunk, hps * dk), row_map),
                  pl.BlockSpec((chunk, hps * dv), row_map),
                  pl.BlockSpec((chunk, hps * dv), row_map),
                  pl.BlockSpec((1, hps, dk, dv), lambda b, h, c: (0 if shared else b, h, 0, 0))],
        out_specs=[pl.BlockSpec((chunk, hps * dv), lambda b, h, c: (b * nc + c, h)),
                   pl.BlockSpec((1, hps, dk, dv), lambda b, h, c: (b, h, 0, 0))],
        out_shape=[jax.ShapeDtypeStruct((nb * rows, heads * dv), BF16),
                   jax.ShapeDtypeStruct((nb, heads, dk, dv), F32)],
        scratch_shapes=[pltpu.VMEM((hps, dk, dv), F32), pltpu.VMEM((hps, chunk, chunk), F32)],
        compiler_params=_params(("parallel", "parallel", "arbitrary"), vmem),
        name="retention",
    )(log_g, q, k, v, gate, s0)
    return og, s_out


def _conv_silu_gate(a, u, halo, cw, cb):
    r1 = pltpu.roll(a, 1, axis=0)
    r2 = pltpu.roll(a, 2, axis=0)
    row = lax.broadcasted_iota(jnp.int32, (SUBLANE, a.shape[1]), 0)
    top1 = jnp.where(row == 0, halo[7:8, :], r1[:SUBLANE])
    top2 = jnp.where(row == 0, halo[6:7, :], jnp.where(row == 1, halo[7:8, :], r2[:SUBLANE]))
    prev1 = jnp.concatenate([top1, r1[SUBLANE:]], axis=0)
    prev2 = jnp.concatenate([top2, r2[SUBLANE:]], axis=0)
    c = cb + cw[0:1, :] * prev2
    c = c + cw[1:2, :] * prev1
    c = c + cw[2:3, :] * a
    return (c * jax.nn.sigmoid(c)) * u


def _conv_gate_kernel(a_ref, u_ref, halo_ref, state_ref, cw_ref, cb_ref, o_ref, *, tiles_per_stream):
    first = (pl.program_id(0) % tiles_per_stream) == 0
    halo = jnp.where(first, state_ref[0], halo_ref[...])
    o_ref[...] = _conv_silu_gate(a_ref[...], u_ref[...], halo, cw_ref[...], cb_ref[...]).astype(o_ref.dtype)


def conv_gate(a, u, state, cw, cb, *, rows_per_stream, row_tile):
    m, f = a.shape
    tr = row_tile
    tc = _divisor_tile(f, 1024, LANE)
    assert rows_per_stream % tr == 0 and tr % SUBLANE == 0
    tps = rows_per_stream // tr
    nfb = f // tc
    hb = tr // SUBLANE
    vmem = 2 * (tr * tc * (4 + 4 + 2) + 3 * 8 * tc * 4) + 6 * tr * tc * 4
    return pl.pallas_call(
        functools.partial(_conv_gate_kernel, tiles_per_stream=tps),
        grid=(m // tr, nfb),
        in_specs=[pl.BlockSpec((tr, tc), lambda i, j: (i, j)),
                  pl.BlockSpec((tr, tc), lambda i, j: (i, j)),
                  pl.BlockSpec((SUBLANE, tc), lambda i, j: (jnp.maximum(i * hb - 1, 0), j)),
                  pl.BlockSpec((1, SUBLANE, tc), lambda i, j: (i // tps, 0, j)),
                  pl.BlockSpec((SUBLANE, tc), lambda i, j: (0, j)),
                  pl.BlockSpec((1, tc), lambda i, j: (0, j))],
        out_specs=pl.BlockSpec((tr, tc), lambda i, j: (i, j)),
        out_shape=jax.ShapeDtypeStruct((m, f), BF16),
        compiler_params=_params(("parallel", "parallel"), vmem),
        name="conv_gate",
    )(a, u, a, state, cw, cb)


def _gate_up_conv_kernel(x_ref, *refs, tiles_per_stream, nsub, blocks):
    wa_refs, wu_refs = refs[:nsub], refs[nsub:2 * nsub]
    state_ref, cw_ref, cb_ref, o_ref, tail_ref, halo_scr = refs[2 * nsub:]
    i = pl.program_id(0)
    j = pl.program_id(1)
    first = (i % tiles_per_stream) == 0
    tm = x_ref.shape[0]
    for s in range(nsub):
        cols = slice(s * MXU_COLS, (s + 1) * MXU_COLS)
        a = jnp.dot(x_ref[...], wa_refs[s][...], preferred_element_type=F32)
        u = jnp.dot(x_ref[...], wu_refs[s][...], preferred_element_type=F32)
        halo = jnp.where(first, state_ref[0, :, cols], halo_scr[j, :, cols])
        act = _conv_silu_gate(a, u, halo, cw_ref[:, cols], cb_ref[:, cols])
        o_ref[:, cols] = jnp.where(j * nsub + s < blocks, act, 0.0).astype(o_ref.dtype)
        tail = a[tm - SUBLANE:, :]
        halo_scr[j, :, cols] = tail
        tail_ref[0, :, cols] = tail


def matmul_conv_gate(x, w, layer, state, cw, cb, *, rows_per_stream):
    m, k = x.shape
    f = w.shape[2] // 2
    fp = cw.shape[1]
    tm = _divisor_tile(rows_per_stream, 1024, 16)
    nsub = 2
    tn = nsub * MXU_COLS
    assert f % MXU_COLS == 0 and fp % tn == 0 and m % rows_per_stream == 0
    blocks = f // MXU_COLS
    tps = rows_per_stream // tm
    nfb = fp // tn
    vmem = 2 * (tm * k * 2 + 2 * k * tn * 2 + tm * tn * 2) + SUBLANE * fp * 4 + 8 * tm * MXU_COLS * 4

    def weight_spec(branch, s):
        return pl.BlockSpec((None, k, MXU_COLS),
                            lambda i, j: (layer, 0, branch * blocks + jnp.minimum(j * nsub + s, blocks - 1)))

    act, tails = pl.pallas_call(
        functools.partial(_gate_up_conv_kernel, tiles_per_stream=tps, nsub=nsub, blocks=blocks),
        grid=(m // tm, nfb),
        in_specs=[pl.BlockSpec((tm, k), lambda i, j: (i, 0)),
                  *[weight_spec(0, s) for s in range(nsub)],
                  *[weight_spec(1, s) for s in range(nsub)],
                  pl.BlockSpec((1, SUBLANE, tn), lambda i, j: (i // tps, 0, j)),
                  pl.BlockSpec((SUBLANE, tn), lambda i, j: (0, j)),
                  pl.BlockSpec((1, tn), lambda i, j: (0, j))],
        out_specs=[pl.BlockSpec((tm, tn), lambda i, j: (i, j)),
                   pl.BlockSpec((1, SUBLANE, tn), lambda i, j: (i, 0, j))],
        out_shape=[jax.ShapeDtypeStruct((m, fp), BF16),
                   jax.ShapeDtypeStruct((m // tm, SUBLANE, fp), F32)],
        scratch_shapes=[pltpu.VMEM((nfb, SUBLANE, tn), F32)],
        compiler_params=_params(("arbitrary", "arbitrary"), vmem),
        name="matmul_gate_up_conv",
    )(x, *([w] * (2 * nsub)), state, cw, cb)
    return act, tails[tps - 1::tps]


def _cumsum_kernel(x_ref, c0_ref, o_ref, *, blk):
    t = x_ref.shape[1]
    r = lax.broadcasted_iota(jnp.int32, (blk, blk), 0)
    s = lax.broadcasted_iota(jnp.int32, (blk, blk), 1)
    tri = (s <= r).astype(BF16)
    carry = c0_ref[0]
    for b in range(t // blk):
        x = x_ref[0, b * blk:(b + 1) * blk, :]
        x_hi = x.astype(BF16)
        r1 = x - x_hi.astype(F32)
        x_mid = r1.astype(BF16)
        x_lo = (r1 - x_mid.astype(F32)).astype(BF16)
        c = jnp.dot(tri, x_hi, preferred_element_type=F32)
        c = c + jnp.dot(tri, x_mid, preferred_element_type=F32)
        c = c + jnp.dot(tri, x_lo, preferred_element_type=F32)
        c = c + carry
        o_ref[0, b * blk:(b + 1) * blk, :] = c
        carry = c[blk - 1:blk, :]


def cumsum_rows(x, c0):
    nb, t, l = x.shape
    blk = _divisor_tile(t, 256, 16)
    shared = c0.shape[0] == 1
    return pl.pallas_call(
        functools.partial(_cumsum_kernel, blk=blk),
        grid=(nb,),
        in_specs=[pl.BlockSpec((1, t, l), lambda b: (b, 0, 0)),
                  pl.BlockSpec((1, 1, l), lambda b: (0 if shared else b, 0, 0))],
        out_specs=pl.BlockSpec((1, t, l), lambda b: (b, 0, 0)),
        out_shape=jax.ShapeDtypeStruct((nb, t, l), F32),
        compiler_params=_params(("parallel",), 4 * t * l * 4 + 2 ** 20),
        name="cumsum",
    )(x, c0)


def _fox_kernel(*refs, tq, prefix):
    m_scr, cq_scr, alpha_scr, acc_scr, sa_scr, sb_scr, pa_scr, pb_scr, kb_scr, v1_scr = refs[-10:]
    if prefix:
        q_ref, cq_ref, km_ref, vm_ref, ckm_ref, kp_ref, vp_ref, ckp_ref, o_ref = refs[:-10]
    else:
        q_ref, cq_ref, km_ref, vm_ref, ckm_ref, o_ref = refs[:-10]
    hd = q_ref.shape[1]
    nq = q_ref.shape[0] // tq

    def lanes(x, width):
        return jnp.tile(x, (1, width // LANE)) if width % LANE == 0 else x[:, :1]

    def ones_columns(vb):
        return jnp.concatenate([vb.astype(BF16), jnp.ones((vb.shape[0], LANE), BF16)], axis=1)

    for c in range(nq):
        rows = slice(c * tq, (c + 1) * tq)
        kb_scr[rows, :] = km_ref[rows, :].astype(BF16)
        v1_scr[rows, :] = ones_columns(vm_ref[rows, :])

    def block(j):
        return pl.ds(pl.multiple_of(j * tq, tq), tq)

    head_lane = lax.broadcasted_iota(jnp.int32, (tq, LANE), 1) == pl.program_id(1)

    def query_block(qi, carry):
        q = q_ref[block(qi), :]
        m_scr[...] = jnp.full(m_scr.shape, MASK_VALUE, F32)
        acc_scr[...] = jnp.zeros(acc_scr.shape, F32)
        cq = jnp.sum(jnp.where(head_lane, cq_ref[0, block(qi), :], 0.0), axis=-1, keepdims=True)
        cq_scr[...] = jnp.broadcast_to(cq, cq_scr.shape)

        def scores(s_ref, kb):
            s_ref[:, :kb.shape[0]] = lax.dot_general(q, kb, (((1,), (1,)), ((), ())), preferred_element_type=F32)

        def softmax(s_ref, p_ref, ck, causal):
            width = ck.shape[1]
            rb = min(tq, max(16, 32768 // max(width, LANE)))
            for r in range(tq // rb):
                rows = slice(r * rb, (r + 1) * rb)
                s = s_ref[rows, :width] - ck
                if causal:
                    row = r * rb + lax.broadcasted_iota(jnp.int32, (rb, width), 0)
                    s = jnp.where(lax.broadcasted_iota(jnp.int32, (rb, width), 1) <= row, s, MASK_VALUE)
                cq = cq_scr[rows, :]
                m_old = m_scr[rows, :]
                m_new = jnp.maximum(m_old, jnp.max(s, axis=-1, keepdims=True) + cq)
                p_ref[rows, :width] = jnp.exp2(s - lanes(m_new - cq, width)).astype(BF16)
                m_scr[rows, :] = m_new
                alpha_scr[rows, :] = jnp.exp2(m_old - m_new)

        def accumulate(p, v1, rescale):
            pv = jnp.dot(p, v1, preferred_element_type=F32)
            if rescale:
                acc_scr[...] = (acc_scr[...] + pv) * jnp.tile(alpha_scr[...], (1, 2))
            else:
                acc_scr[...] += pv

        if prefix:
            pw = kp_ref.shape[1]
            scores(sa_scr, kp_ref[0].astype(BF16))
            softmax(sa_scr, pa_scr, ckp_ref[0], False)
            accumulate(pa_scr[:, :pw], ones_columns(vp_ref[0]), False)

        scores(sa_scr, kb_scr[block(0), :])
        pb_scr[...] = jnp.zeros(pb_scr.shape, BF16)

        def pair(t, carry):
            j = 2 * t
            scores(sb_scr, kb_scr[block(j + 1), :])
            softmax(sa_scr, pa_scr, ckm_ref[0, j], False)
            accumulate(pb_scr[:, :tq], v1_scr[block(jnp.maximum(j - 1, 0)), :], True)
            scores(sa_scr, kb_scr[block(j + 2), :])
            softmax(sb_scr, pb_scr, ckm_ref[0, j + 1], False)
            accumulate(pa_scr[:, :tq], v1_scr[block(j), :], True)
            return carry

        lax.fori_loop(0, qi // 2, pair, 0)
        last = jnp.maximum(qi - 1, 0)

        @pl.when(qi % 2 == 0)
        def _():
            softmax(sa_scr, pa_scr, ckm_ref[0, qi], True)
            accumulate(pb_scr[:, :tq], v1_scr[block(last), :], True)
            accumulate(pa_scr[:, :tq], v1_scr[block(qi), :], False)

        @pl.when(qi % 2 == 1)
        def _():
            scores(sb_scr, kb_scr[block(qi), :])
            softmax(sa_scr, pa_scr, ckm_ref[0, last], False)
            accumulate(pb_scr[:, :tq], v1_scr[block(jnp.maximum(qi - 2, 0)), :], True)
            softmax(sb_scr, pb_scr, ckm_ref[0, qi], True)
            accumulate(pa_scr[:, :tq], v1_scr[block(last), :], True)
            accumulate(pb_scr[:, :tq], v1_scr[block(qi), :], False)

        o_ref[block(qi), :] = (acc_scr[:, :hd] / acc_scr[:, hd:]).astype(o_ref.dtype)
        return carry

    lax.fori_loop(0, nq, query_block, 0)


def fox_attention(q, cq, km, vm, ckm, prefix, *, nb, rows, row_off, heads, q_tile):
    hd = q.shape[1] // heads
    tq = q_tile
    nq = rows // tq
    assert rows % tq == 0 and row_off % rows == 0 and hd == LANE and heads <= LANE
    boff = row_off // rows
    in_specs = [pl.BlockSpec((rows, hd), lambda b, h: (boff + b, h)),
                pl.BlockSpec((1, rows, LANE), lambda b, h: (b, 0, 0)),
                pl.BlockSpec((rows, hd), lambda b, h: (boff + b, h)),
                pl.BlockSpec((rows, hd), lambda b, h: (boff + b, h)),
                pl.BlockSpec((1, nq, 1, tq), lambda b, h: (b * heads + h, 0, 0, 0))]
    args = [q, cq, km, vm, ckm.reshape(nb * heads, nq, 1, tq)]
    vmem = 2 * (2 * rows * hd * 2 + rows * LANE * 4 + 2 * rows * hd * 4 + 8 * rows * 4)
    width = tq
    if prefix is not None:
        kp, vp, ckp = prefix
        p = kp.shape[1]
        width = max(tq, p)
        shared = kp.shape[0] == 1
        in_specs += [pl.BlockSpec((1, p, hd), lambda b, h: (0 if shared else b, 0, h)),
                     pl.BlockSpec((1, p, hd), lambda b, h: (0 if shared else b, 0, h)),
                     pl.BlockSpec((1, 1, p), lambda b, h: (h if shared else b * heads + h, 0, 0))]
        args += [kp, vp, ckp]
        vmem += 2 * (2 * p * hd * 4 + 8 * p * 4)
    width = -(-width // LANE) * LANE
    vmem += tq * width * 12 + 5 * tq * LANE * 4 + rows * hd * 6 + 2 ** 21
    return pl.pallas_call(
        functools.partial(_fox_kernel, tq=tq, prefix=prefix is not None),
        grid=(nb, heads),
        in_specs=in_specs,
        out_specs=pl.BlockSpec((rows, hd), lambda b, h: (b, h)),
        out_shape=jax.ShapeDtypeStruct((nb * rows, heads * hd), BF16),
        scratch_shapes=[pltpu.VMEM((tq, LANE), F32), pltpu.VMEM((tq, LANE), F32), pltpu.VMEM((tq, LANE), F32),
                        pltpu.VMEM((tq, 2 * hd), F32),
                        pltpu.VMEM((tq, width), F32), pltpu.VMEM((tq, width), F32),
                        pltpu.VMEM((tq, width), BF16), pltpu.VMEM((tq, width), BF16),
                        pltpu.VMEM((rows, hd), BF16), pltpu.VMEM((rows, 2 * hd), BF16)],
        compiler_params=_params(("parallel", "parallel"), vmem),
        name="fox_attention",
    )(*args)


def _rope_tables(pos, half):
    inv = ROPE_BASE ** (-jnp.arange(half, dtype=F32) / half)
    ang = pos.astype(F32)[:, None] * inv[None, :]
    return jnp.cos(ang), jnp.sin(ang)


def _head_major(c, heads):
    nb, t, _ = c.shape
    c2 = c * LOG2E
    return c2, jnp.swapaxes(c2[:, :, :heads], 1, 2).reshape(nb * heads, 1, t)


def kernel(x_prompt, x_sample, cache_k, cache_v, cache_logf, state_ret, state_conv, meta, g_attn, g_ffn,
           w_ret_in, w_ret_out, g_kv, w_kvf, b_f, g_k, w_q_b, g_q, w_o_b, w_gu, conv_w, conv_b, w_down):
    nbp, seq, d = x_prompt.shape
    nbs, dseq, _ = x_sample.shape
    n_meta = meta.shape[0]
    past = cache_k.shape[1]
    depth = g_attn.shape[0]
    n_a = w_ret_in.shape[0]
    h_a, dk_a, dv_a = state_ret.shape[2:]
    h_b, hd_b = cache_k.shape[2:]
    qk_w, v_w = h_a * dk_a, h_a * dv_a
    d_ff = w_down.shape[1]
    cw_taps = conv_w.shape[1]
    assert n_a == 1 and depth == 2 and cw_taps == 3, "kernel is written for one retention + one attention layer"
    assert d_ff % MXU_COLS == 0
    ff_tile = 1024 if d_ff >= 4096 else 2 * MXU_COLS
    ffp = -(-d_ff // ff_tile) * ff_tile
    m_main = nbp * seq
    m_s = nbs * dseq
    m_sm = m_s + n_meta
    m_rest = -(-m_sm // dseq) * dseq
    dt = x_prompt.dtype

    w_in = w_ret_in[0].astype(BF16)
    w_out = w_ret_out[0].astype(BF16)
    w_kv = w_kvf[:, :2 * d].astype(BF16)
    w_f = jnp.pad(w_kvf[:, 2 * d:], ((0, 0), (0, LANE - h_b))).astype(BF16)
    b_fp = jnp.pad(b_f, (0, LANE - h_b)).reshape(1, LANE)
    w_q = w_q_b[0].astype(BF16)
    w_o = w_o_b[0].astype(BF16)
    padc = lambda a: jnp.pad(a, ((0, 0),) * (a.ndim - 1) + ((0, ffp - d_ff),))
    w_gu_b = w_gu.astype(BF16)
    w_dn_p = jnp.pad(w_down, ((0, 0), (0, ffp - d_ff), (0, 0))).astype(BF16)
    cw_p = [jnp.pad(padc(conv_w[l]), ((0, SUBLANE - cw_taps), (0, 0))) for l in range(depth)]
    cb_p = [padc(conv_b[l]).reshape(1, ffp) for l in range(depth)]
    log_g = jnp.log1p(-jnp.exp2(-5.0 - jnp.arange(h_a, dtype=F32)))

    x_main = x_prompt.reshape(m_main, d)
    pad_rows = lambda a: jnp.pad(a, ((0, m_rest - m_sm),) + ((0, 0),) * (a.ndim - 1))
    x_rest = pad_rows(jnp.concatenate([x_sample.reshape(m_s, d), meta.astype(dt)], axis=0))
    cos_m, sin_m = _rope_tables(n_meta + jnp.arange(seq), dk_a // 2)
    pos_rest = pad_rows(jnp.concatenate([jnp.tile(n_meta + past + jnp.arange(dseq), nbs), jnp.arange(n_meta)]))
    cos_r, sin_r = _rope_tables(pos_rest, dk_a // 2)

    def state_rows(rows2):
        return jnp.pad(padc(rows2), ((0, 0), (SUBLANE - 2, 0), (0, 0)))

    def conv_ffn(x, l, state):
        (hn,) = rmsnorm_rows(x, g_ffn[l:l + 1])
        act, tails = matmul_conv_gate(hn, w_gu_b, l, state, cw_p[l], cb_p[l], rows_per_stream=seq)
        return matmul_residual(act, w_dn_p, x, layer=l), tails[:, SUBLANE - 2:, :d_ff]

    def ret_in(h, cos, sin, period):
        q = matmul_rope(h, w_in, cos, sin, col_off=0, n_out=qk_w, head_dim=dk_a, scale=1.0, period_rows=period)
        k = matmul_rope(h, w_in, cos, sin, col_off=qk_w, n_out=qk_w, head_dim=dk_a, scale=dk_a ** -0.5,
                        period_rows=period)
        v = matmul(h, w_in, col_off=2 * qk_w, n_out=v_w, out_dtype=BF16)
        gate = matmul(h, w_in, col_off=2 * qk_w + v_w, n_out=v_w, out_dtype=BF16)
        return q, k, v, gate

    (h_r,) = rmsnorm_rows(x_rest, g_attn[0:1])
    q_r, k_r, v_r, gate_r = ret_in(h_r, cos_r, sin_r, m_rest)
    og_s, ret_s = retention(log_g, q_r, k_r, v_r, gate_r, state_ret[0], nb=nbs, rows=dseq, chunk=dseq,
                            row_off=0, heads=h_a)
    og_m, ret_meta = retention(log_g, q_r, k_r, v_r, gate_r, jnp.zeros((1, h_a, dk_a, dv_a), F32), nb=1,
                               rows=n_meta, chunk=n_meta, row_off=m_s, heads=h_a)
    x_r1 = matmul_residual(pad_rows(jnp.concatenate([og_s, og_m], axis=0)), w_out, x_rest)
    assert n_meta <= dseq and n_meta % SUBLANE == 0
    state_r0 = jnp.concatenate([state_rows(state_conv[0]), jnp.zeros((1, SUBLANE, ffp), F32)], axis=0)

    def conv_ffn_rest(x, l, state):
        (hn,) = rmsnorm_rows(x, g_ffn[l:l + 1])
        a = matmul(hn, w_gu_b, layer=l, col_off=0, n_out=d_ff)
        u = matmul(hn, w_gu_b, layer=l, col_off=d_ff, n_out=d_ff)
        act = conv_gate(padc(a), padc(u), state, cw_p[l], cb_p[l], rows_per_stream=dseq, row_tile=dseq)
        return matmul_residual(act, w_dn_p, x, layer=l), a

    x_r2, a_r0 = conv_ffn_rest(x_r1, 0, state_r0)

    (h_m,) = rmsnorm_rows(x_main, g_attn[0:1])
    q_m, k_m, v_m, gate_m = ret_in(h_m, cos_m, sin_m, seq)
    chunk = _divisor_tile(seq, 256, 16)
    og_main, ret_p = retention(log_g, q_m, k_m, v_m, gate_m, ret_meta, nb=nbp, rows=seq, chunk=chunk,
                               row_off=0, heads=h_a)
    x_m1 = matmul_residual(og_main, w_out, x_main)
    state_m0 = jnp.broadcast_to(state_rows(a_r0[None, m_sm - 2:m_sm, :d_ff]), (nbp, SUBLANE, ffp))
    x_m2, conv_p0 = conv_ffn(x_m1, 0, state_m0)

    def kv_side(x, g_row):
        h, hk = rmsnorm_rows(x, jnp.stack([g_row, g_kv]))
        kk = matmul_headnorm(hk, w_kv, g_k, col_off=0, n_out=d, head_dim=hd_b, scale=1.0, out_dtype=F32)
        vv = matmul(hk, w_kv, col_off=d, n_out=d)
        lf = matmul_logsigmoid(hk, w_f, b_fp)
        q = matmul_headnorm(h, w_q, g_q[0], col_off=0, n_out=d, head_dim=hd_b, scale=hd_b ** -0.5 * LOG2E,
                            out_dtype=BF16)
        return q, kk, vv, lf

    q_r, kk_r, vv_r, lf_r = kv_side(x_r2, g_attn[1])
    q_m, kk_m, vv_m, lf_m = kv_side(x_m2, g_attn[1])

    c_cache = cumsum_rows(jnp.pad(cache_logf, ((0, 0), (0, 0), (0, LANE - h_b))), jnp.zeros((1, 1, LANE), F32))
    c_s = cumsum_rows(lf_r[:m_s].reshape(nbs, dseq, LANE), c_cache[:, past - 1:past, :])
    c_meta = cumsum_rows(lf_r[m_s:m_sm].reshape(1, n_meta, LANE), jnp.zeros((1, 1, LANE), F32))
    c_main = cumsum_rows(lf_m.reshape(nbp, seq, LANE), c_meta[:, n_meta - 1:n_meta, :])

    cq_s, ck_s = _head_major(c_s, h_b)
    _, ck_cache = _head_major(c_cache, h_b)
    o_s = fox_attention(q_r, cq_s, kk_r, vv_r, ck_s,
                        (cache_k.reshape(nbs, past, d), cache_v.reshape(nbs, past, d), ck_cache),
                        nb=nbs, rows=dseq, row_off=0, heads=h_b, q_tile=dseq)
    cq_meta, ck_meta = _head_major(c_meta, h_b)
    o_meta = fox_attention(q_r, cq_meta, kk_r, vv_r, ck_meta, None,
                           nb=1, rows=n_meta, row_off=m_s, heads=h_b, q_tile=n_meta)
    cq_main, ck_main = _head_major(c_main, h_b)
    k_meta = kk_r[m_s:m_sm]
    v_meta = vv_r[m_s:m_sm]
    o_main = fox_attention(q_m, cq_main, kk_m, vv_m, ck_main, (k_meta[None], v_meta[None], ck_meta),
                           nb=nbp, rows=seq, row_off=0, heads=h_b, q_tile=_divisor_tile(seq, 512, 16))

    x_r3 = matmul_residual(pad_rows(jnp.concatenate([o_s, o_meta], axis=0)), w_o, x_r2)
    state_r1 = jnp.concatenate([state_rows(state_conv[1]), jnp.zeros((1, SUBLANE, ffp), F32)], axis=0)
    x_r4, a_r1 = conv_ffn_rest(x_r3, 1, state_r1)
    x_m3 = matmul_residual(o_main, w_o, x_m2)
    state_m1 = jnp.broadcast_to(state_rows(a_r1[None, m_sm - 2:m_sm, :d_ff]), (nbp, SUBLANE, ffp))
    x_m4, conv_p1 = conv_ffn(x_m3, 1, state_m1)

    def with_meta(meta_rows, main_rows, tail):
        mr = jnp.broadcast_to(meta_rows.reshape((1, n_meta) + tail), (nbp, n_meta) + tail)
        return jnp.concatenate([mr, main_rows.reshape((nbp, seq) + tail)], axis=1)

    y_prompt = x_m4.reshape(nbp, seq, d)
    y_sample = x_r4[:m_s].reshape(nbs, dseq, d)
    k_prompt = with_meta(k_meta, kk_m, (h_b, hd_b))
    v_prompt = with_meta(v_meta, vv_m, (h_b, hd_b))
    logf_prompt = with_meta(lf_r[m_s:m_sm, :h_b], lf_m[:, :h_b], (h_b,))
    conv_prompt = jnp.stack([conv_p0, conv_p1])
    conv_sample = jnp.stack([a[:m_s].reshape(nbs, dseq, d_ff)[:, dseq - 2:] for a in (a_r0, a_r1)])
    return (y_prompt, y_sample, k_prompt, v_prompt, logf_prompt, ret_p[None], conv_prompt,
            kk_r[:m_s].reshape(nbs, dseq, h_b, hd_b), vv_r[:m_s].reshape(nbs, dseq, h_b, hd_b),
            lf_r[:m_s, :h_b].reshape(nbs, dseq, h_b), ret_s[None], conv_sample)
```

```python
import functools

import jax
import jax.numpy as jnp
from jax import lax
from jax.experimental import pallas as pl
from jax.experimental.pallas import tpu as pltpu

EPS = 1e-6
LOG2E = 1.4426950408889634
ROPE_BASE = 10000.0
MASK_VALUE = -1e30
LANE = 128
SUBLANE = 8
MXU_COLS = 256
VMEM_BUDGET = 44 * 2 ** 20
F32 = jnp.float32
BF16 = jnp.bfloat16


def _params(semantics, vmem_bytes):
    return pltpu.CompilerParams(dimension_semantics=semantics,
                                vmem_limit_bytes=int(min(vmem_bytes + 8 * 2 ** 20, 60 * 2 ** 20)))


def _divisor_tile(n, cap, mult):
    if n <= cap:
        return n
    t = cap - cap % mult
    while t >= mult:
        if n % t == 0:
            return t
        t -= mult
    raise ValueError(f"no tile for {n} (cap {cap}, multiple {mult})")


def _rmsnorm_kernel(x_ref, g_ref, *o_refs):
    x = x_ref[...]
    y = x * lax.rsqrt(jnp.mean(x * x, axis=-1, keepdims=True) + EPS)
    for n, o_ref in enumerate(o_refs):
        o_ref[...] = (y * g_ref[n:n + 1, :]).astype(o_ref.dtype)


def rmsnorm_rows(x, gains):
    m, d = x.shape
    g = gains.shape[0]
    tm = _divisor_tile(m, 256, 16)
    vmem = 2 * tm * d * (4 + 2 * g)
    outs = pl.pallas_call(
        _rmsnorm_kernel,
        grid=(m // tm,),
        in_specs=[pl.BlockSpec((tm, d), lambda i: (i, 0)),
                  pl.BlockSpec((g, d), lambda i: (0, 0))],
        out_specs=[pl.BlockSpec((tm, d), lambda i: (i, 0)) for _ in range(g)],
        out_shape=[jax.ShapeDtypeStruct((m, d), BF16) for _ in range(g)],
        compiler_params=_params(("parallel",), vmem),
        name="rmsnorm",
    )(x, gains)
    return tuple(outs)


def _epi_plain(acc, o_ref):
    o_ref[...] = acc.astype(o_ref.dtype)


def _epi_residual(acc, res_ref, o_ref):
    o_ref[...] = res_ref[...] + acc


def _epi_rope(acc, cos_ref, sin_ref, o_ref, *, head_dim, scale):
    half = head_dim // 2
    cos = cos_ref[...]
    sin = sin_ref[...]
    for h in range(acc.shape[1] // head_dim):
        x1 = acc[:, h * head_dim:h * head_dim + half]
        x2 = acc[:, h * head_dim + half:(h + 1) * head_dim]
        o_ref[:, h * head_dim:h * head_dim + half] = ((x1 * cos - x2 * sin) * scale).astype(o_ref.dtype)
        o_ref[:, h * head_dim + half:(h + 1) * head_dim] = ((x1 * sin + x2 * cos) * scale).astype(o_ref.dtype)


def _epi_headnorm(acc, g_ref, o_ref, *, head_dim, scale):
    g = g_ref[...]
    for h in range(acc.shape[1] // head_dim):
        x = acc[:, h * head_dim:(h + 1) * head_dim]
        y = x * lax.rsqrt(jnp.mean(x * x, axis=-1, keepdims=True) + EPS) * g
        o_ref[:, h * head_dim:(h + 1) * head_dim] = (y * scale).astype(o_ref.dtype)


def _epi_logsigmoid(acc, b_ref, o_ref):
    z = acc + b_ref[...]
    o_ref[...] = jnp.minimum(z, 0.0) - jnp.log1p(jnp.exp(-jnp.abs(z)))


def _mm_kernel(x_ref, w_ref, *rest, epilogue, nk):
    if nk == 1:
        epilogue(jnp.dot(x_ref[...], w_ref[...], preferred_element_type=F32), *rest)
        return
    acc_ref = rest[-1]
    k = pl.program_id(2)

    @pl.when(k == 0)
    def _():
        acc_ref[...] = jnp.zeros_like(acc_ref)

    acc_ref[...] += jnp.dot(x_ref[...], w_ref[...], preferred_element_type=F32)

    @pl.when(k == nk - 1)
    def _():
        epilogue(acc_ref[...], *rest[:-1])


def _mm_tiles(m, k, n, out_bytes, extra_bytes_per_elem):
    tm = _divisor_tile(m, 1024, 16)
    for tn in (1024, 512, 256, 128):
        if n % tn:
            continue
        for nk in (1, 2, 4, 8):
            if k % nk or (k // nk) % LANE:
                continue
            tk = k // nk
            ws = 2 * (tm * tk * 2 + tk * tn * 2 + tm * tn * (out_bytes + extra_bytes_per_elem))
            ws += tm * tn * 4
            if ws <= VMEM_BUDGET:
                return tm, tn, tk, ws
    raise ValueError(f"no matmul tiling for {(m, k, n)}")


def matmul(x, w, *, col_off=0, n_out=None, epilogue=_epi_plain, extras=(), extra_specs=lambda tm, tn: [],
           out_dtype=F32, extra_bytes_per_elem=0, row_tile=None, layer=None):
    m, k = x.shape
    n_out = w.shape[-1] - col_off if n_out is None else n_out
    tm, tn, tk, ws = _mm_tiles(m, k, n_out, jnp.dtype(out_dtype).itemsize, extra_bytes_per_elem)
    if row_tile is not None:
        tm = row_tile
    assert col_off % tn == 0 and m % tm == 0
    nk = k // tk
    joff = col_off // tn
    kern = functools.partial(_mm_kernel, epilogue=epilogue, nk=nk)
    if layer is None:
        w_spec = pl.BlockSpec((tk, tn), lambda i, j, kk: (kk, j + joff))
    else:
        w_spec = pl.BlockSpec((None, tk, tn), lambda i, j, kk: (layer, kk, j + joff))
    return pl.pallas_call(
        kern,
        grid=(m // tm, n_out // tn, nk),
        in_specs=[pl.BlockSpec((tm, tk), lambda i, j, kk: (i, kk)), w_spec, *extra_specs(tm, tn)],
        out_specs=pl.BlockSpec((tm, tn), lambda i, j, kk: (i, j)),
        out_shape=jax.ShapeDtypeStruct((m, n_out), out_dtype),
        scratch_shapes=[pltpu.VMEM((tm, tn), F32)] if nk > 1 else [],
        compiler_params=_params(("parallel", "parallel", "arbitrary"), ws),
        name="matmul_" + getattr(epilogue, "func", epilogue).__name__.lstrip("_"),
    )(x, w, *extras)


def matmul_residual(x, w, res, layer=None):
    return matmul(x, w, epilogue=_epi_residual, extras=(res,), extra_bytes_per_elem=4, layer=layer,
                  extra_specs=lambda tm, tn: [pl.BlockSpec((tm, tn), lambda i, j, kk: (i, j))])


def matmul_rope(x, w, cos, sin, *, col_off, n_out, head_dim, scale, period_rows):
    half = head_dim // 2

    def specs(tm, tn):
        assert period_rows % tm == 0 and tn % head_dim == 0
        nper = period_rows // tm
        return [pl.BlockSpec((tm, half), lambda i, j, kk: (i % nper, 0))] * 2

    row_tile = _divisor_tile(period_rows, _divisor_tile(x.shape[0], 1024, 16), 16)
    return matmul(x, w, col_off=col_off, n_out=n_out, out_dtype=BF16, extras=(cos, sin), extra_specs=specs,
                  row_tile=row_tile,
                  epilogue=functools.partial(_epi_rope, head_dim=head_dim, scale=scale))


def matmul_headnorm(x, w, gain, *, col_off, n_out, head_dim, scale, out_dtype):
    return matmul(x, w, col_off=col_off, n_out=n_out, out_dtype=out_dtype, extras=(gain.reshape(1, head_dim),),
                  extra_specs=lambda tm, tn: [pl.BlockSpec((1, head_dim), lambda i, j, kk: (0, 0))],
                  epilogue=functools.partial(_epi_headnorm, head_dim=head_dim, scale=scale))


def matmul_logsigmoid(x, w, bias):
    return matmul(x, w, extras=(bias,), epilogue=_epi_logsigmoid,
                  extra_specs=lambda tm, tn: [pl.BlockSpec((1, tn), lambda i, j, kk: (0, j))])


def _retention_kernel(lg_ref, q_ref, k_ref, v_ref, gate_ref, s0_ref, og_ref, sout_ref, s_scr, decay_scr, *,
                      nc, hps):
    hg = pl.program_id(1)
    c = pl.program_id(2)
    t = q_ref.shape[0]
    dk = q_ref.shape[1] // hps
    dv = v_ref.shape[1] // hps
    idx = lax.broadcasted_iota(jnp.int32, (t, 1), 0).astype(F32)

    @pl.when(c == 0)
    def _():
        rel = (lax.broadcasted_iota(jnp.int32, (t, t), 0) - lax.broadcasted_iota(jnp.int32, (t, t), 1)).astype(F32)
        causal = rel >= 0.0
        for hh in range(hps):
            s_scr[hh] = s0_ref[0, hh]
            decay_scr[hh] = jnp.where(causal, jnp.exp(jnp.where(causal, rel, 0.0) * lg_ref[hg * hps + hh]), 0.0)

    for hh in range(hps):
        lg = lg_ref[hg * hps + hh]
        q = q_ref[:, hh * dk:(hh + 1) * dk]
        k = k_ref[:, hh * dk:(hh + 1) * dk]
        v = v_ref[:, hh * dv:(hh + 1) * dv]
        scores = lax.dot_general(q, k, (((1,), (1,)), ((), ())), preferred_element_type=F32) * decay_scr[hh]
        o = jnp.dot(scores.astype(BF16), v, preferred_element_type=F32)
        s_old = s_scr[hh]
        q_dec = (q.astype(F32) * jnp.exp((idx + 1.0) * lg)).astype(BF16)
        o = o + jnp.dot(q_dec, s_old.astype(BF16), preferred_element_type=F32)
        k_dec = (k.astype(F32) * jnp.exp((t - 1.0 - idx) * lg)).astype(BF16)
        kv = lax.dot_general(k_dec, v, (((0,), (0,)), ((), ())), preferred_element_type=F32)
        s_new = jnp.exp(jnp.full((1, 1), t, F32) * lg) * s_old + kv
        s_scr[hh] = s_new
        on = o * lax.rsqrt(jnp.mean(o * o, axis=-1, keepdims=True) + EPS)
        g = gate_ref[:, hh * dv:(hh + 1) * dv].astype(F32)
        og_ref[:, hh * dv:(hh + 1) * dv] = ((g * jax.nn.sigmoid(g)) * on).astype(og_ref.dtype)

        @pl.when(c == nc - 1)
        def _():
            sout_ref[0, hh] = s_new


def retention(log_g, q, k, v, gate, s0, *, nb, rows, chunk, row_off, heads):
    dk = q.shape[1] // heads
    dv = v.shape[1] // heads
    nc = rows // chunk
    hps = 2 if heads % 2 == 0 else 1
    assert rows % chunk == 0 and row_off % chunk == 0
    roff = row_off // chunk
    shared = s0.shape[0] == 1
    row_map = lambda b, h, c: (roff + b * nc + c, h)
    vmem = hps * (2 * chunk * (2 * dk * 2 + 3 * dv * 2) + 5 * dk * dv * 4 + 8 * chunk * chunk * 4)
    og, s_out = pl.pallas_call(
        functools.partial(_retention_kernel, nc=nc, hps=hps),
        grid=(nb, heads // hps, nc),
        in_specs=[pl.BlockSpec(memory_space=pltpu.SMEM),
                  pl.BlockSpec((chunk, hps * dk), row_map),
                  pl.BlockSpec((chunk, hps * dk), row_map),
                  pl.BlockSpec((chunk, hps * dv), row_map),
                  pl.BlockSpec((chunk, hps * dv), row_map),
                  pl.BlockSpec((1, hps, dk, dv), lambda b, h, c: (0 if shared else b, h, 0, 0))],
        out_specs=[pl.BlockSpec((chunk, hps * dv), lambda b, h, c: (b * nc + c, h)),
                   pl.BlockSpec((1, hps, dk, dv), lambda b, h, c: (b, h, 0, 0))],
        out_shape=[jax.ShapeDtypeStruct((nb * rows, heads * dv), BF16),
                   jax.ShapeDtypeStruct((nb, heads, dk, dv), F32)],
        scratch_shapes=[pltpu.VMEM((hps, dk, dv), F32), pltpu.VMEM((hps, chunk, chunk), F32)],
        compiler_params=_params(("parallel", "parallel", "arbitrary"), vmem),
        name="retention",
    )(log_g, q, k, v, gate, s0)
    return og, s_out


def _conv_silu_gate(a, u, halo, cw, cb):
    r1 = pltpu.roll(a, 1, axis=0)
    r2 = pltpu.roll(a, 2, axis=0)
    row = lax.broadcasted_iota(jnp.int32, (SUBLANE, a.shape[1]), 0)
    top1 = jnp.where(row == 0, halo[7:8, :], r1[:SUBLANE])
    top2 = jnp.where(row == 0, halo[6:7, :], jnp.where(row == 1, halo[7:8, :], r2[:SUBLANE]))
    prev1 = jnp.concatenate([top1, r1[SUBLANE:]], axis=0)
    prev2 = jnp.concatenate([top2, r2[SUBLANE:]], axis=0)
    c = cb + cw[0:1, :] * prev2
    c = c + cw[1:2, :] * prev1
    c = c + cw[2:3, :] * a
    return (c * jax.nn.sigmoid(c)) * u


def _conv_gate_kernel(a_ref, u_ref, halo_ref, state_ref, cw_ref, cb_ref, o_ref, *, tiles_per_stream):
    first = (pl.program_id(0) % tiles_per_stream) == 0
    halo = jnp.where(first, state_ref[0], halo_ref[...])
    o_ref[...] = _conv_silu_gate(a_ref[...], u_ref[...], halo, cw_ref[...], cb_ref[...]).astype(o_ref.dtype)


def conv_gate(a, u, state, cw, cb, *, rows_per_stream, row_tile):
    m, f = a.shape
    tr = row_tile
    tc = _divisor_tile(f, 1024, LANE)
    assert rows_per_stream % tr == 0 and tr % SUBLANE == 0
    tps = rows_per_stream // tr
    nfb = f // tc
    hb = tr // SUBLANE
    vmem = 2 * (tr * tc * (4 + 4 + 2) + 3 * 8 * tc * 4) + 6 * tr * tc * 4
    return pl.pallas_call(
        functools.partial(_conv_gate_kernel, tiles_per_stream=tps),
        grid=(m // tr, nfb),
        in_specs=[pl.BlockSpec((tr, tc), lambda i, j: (i, j)),
                  pl.BlockSpec((tr, tc), lambda i, j: (i, j)),
                  pl.BlockSpec((SUBLANE, tc), lambda i, j: (jnp.maximum(i * hb - 1, 0), j)),
                  pl.BlockSpec((1, SUBLANE, tc), lambda i, j: (i // tps, 0, j)),
                  pl.BlockSpec((SUBLANE, tc), lambda i, j: (0, j)),
                  pl.BlockSpec((1, tc), lambda i, j: (0, j))],
        out_specs=pl.BlockSpec((tr, tc), lambda i, j: (i, j)),
        out_shape=jax.ShapeDtypeStruct((m, f), BF16),
        compiler_params=_params(("parallel", "parallel"), vmem),
        name="conv_gate",
    )(a, u, a, state, cw, cb)


def _gate_up_conv_kernel(x_ref, *refs, tiles_per_stream, nsub, blocks):
    wa_refs, wu_refs = refs[:nsub], refs[nsub:2 * nsub]
    state_ref, cw_ref, cb_ref, o_ref, tail_ref, halo_scr = refs[2 * nsub:]
    i = pl.program_id(0)
    j = pl.program_id(1)
    first = (i % tiles_per_stream) == 0
    tm = x_ref.shape[0]
    for s in range(nsub):
        cols = slice(s * MXU_COLS, (s + 1) * MXU_COLS)
        a = jnp.dot(x_ref[...], wa_refs[s][...], preferred_element_type=F32)
        u = jnp.dot(x_ref[...], wu_refs[s][...], preferred_element_type=F32)
        halo = jnp.where(first, state_ref[0, :, cols], halo_scr[j, :, cols])
        act = _conv_silu_gate(a, u, halo, cw_ref[:, cols], cb_ref[:, cols])
        o_ref[:, cols] = jnp.where(j * nsub + s < blocks, act, 0.0).astype(o_ref.dtype)
        tail = a[tm - SUBLANE:, :]
        halo_scr[j, :, cols] = tail
        tail_ref[0, :, cols] = tail


def matmul_conv_gate(x, w, layer, state, cw, cb, *, rows_per_stream):
    m, k = x.shape
    f = w.shape[2] // 2
    fp = cw.shape[1]
    tm = _divisor_tile(rows_per_stream, 1024, 16)
    nsub = 2
    tn = nsub * MXU_COLS
    assert f % MXU_COLS == 0 and fp % tn == 0 and m % rows_per_stream == 0
    blocks = f // MXU_COLS
    tps = rows_per_stream // tm
    nfb = fp // tn
    vmem = 2 * (tm * k * 2 + 2 * k * tn * 2 + tm * tn * 2) + SUBLANE * fp * 4 + 8 * tm * MXU_COLS * 4

    def weight_spec(branch, s):
        return pl.BlockSpec((None, k, MXU_COLS),
                            lambda i, j: (layer, 0, branch * blocks + jnp.minimum(j * nsub + s, blocks - 1)))

    act, tails = pl.pallas_call(
        functools.partial(_gate_up_conv_kernel, tiles_per_stream=tps, nsub=nsub, blocks=blocks),
        grid=(m // tm, nfb),
        in_specs=[pl.BlockSpec((tm, k), lambda i, j: (i, 0)),
                  *[weight_spec(0, s) for s in range(nsub)],
                  *[weight_spec(1, s) for s in range(nsub)],
                  pl.BlockSpec((1, SUBLANE, tn), lambda i, j: (i // tps, 0, j)),
                  pl.BlockSpec((SUBLANE, tn), lambda i, j: (0, j)),
                  pl.BlockSpec((1, tn), lambda i, j: (0, j))],
        out_specs=[pl.BlockSpec((tm, tn), lambda i, j: (i, j)),
                   pl.BlockSpec((1, SUBLANE, tn), lambda i, j: (i, 0, j))],
        out_shape=[jax.ShapeDtypeStruct((m, fp), BF16),
                   jax.ShapeDtypeStruct((m // tm, SUBLANE, fp), F32)],
        scratch_shapes=[pltpu.VMEM((nfb, SUBLANE, tn), F32)],
        compiler_params=_params(("arbitrary", "arbitrary"), vmem),
        name="matmul_gate_up_conv",
    )(x, *([w] * (2 * nsub)), state, cw, cb)
    return act, tails[tps - 1::tps]


def _cumsum_kernel(x_ref, c0_ref, o_ref, *, blk):
    t = x_ref.shape[1]
    r = lax.broadcasted_iota(jnp.int32, (blk, blk), 0)
    s = lax.broadcasted_iota(jnp.int32, (blk, blk), 1)
    tri = (s <= r).astype(BF16)
    carry = c0_ref[0]
    for b in range(t // blk):
        x = x_ref[0, b * blk:(b + 1) * blk, :]
        x_hi = x.astype(BF16)
        r1 = x - x_hi.astype(F32)
        x_mid = r1.astype(BF16)
        x_lo = (r1 - x_mid.astype(F32)).astype(BF16)
        c = jnp.dot(tri, x_hi, preferred_element_type=F32)
        c = c + jnp.dot(tri, x_mid, preferred_element_type=F32)
        c = c + jnp.dot(tri, x_lo, preferred_element_type=F32)
        c = c + carry
        o_ref[0, b * blk:(b + 1) * blk, :] = c
        carry = c[blk - 1:blk, :]


def cumsum_rows(x, c0):
    nb, t, l = x.shape
    blk = _divisor_tile(t, 256, 16)
    shared = c0.shape[0] == 1
    return pl.pallas_call(
        functools.partial(_cumsum_kernel, blk=blk),
        grid=(nb,),
        in_specs=[pl.BlockSpec((1, t, l), lambda b: (b, 0, 0)),
                  pl.BlockSpec((1, 1, l), lambda b: (0 if shared else b, 0, 0))],
        out_specs=pl.BlockSpec((1, t, l), lambda b: (b, 0, 0)),
        out_shape=jax.ShapeDtypeStruct((nb, t, l), F32),
        compiler_params=_params(("parallel",), 4 * t * l * 4 + 2 ** 20),
        name="cumsum",
    )(x, c0)


def _fox_kernel(*refs, tq, prefix):
    m_scr, cq_scr, alpha_scr, acc_scr, sa_scr, sb_scr, pa_scr, pb_scr, kb_scr, v1_scr, m0_scr, acc0_scr = refs[-12:]
    if prefix:
        q_ref, cq_ref, km_ref, vm_ref, ckm_ref, kp_ref, vp_ref, ckp_ref, o_ref = refs[:-12]
    else:
        q_ref, cq_ref, km_ref, vm_ref, ckm_ref, o_ref = refs[:-12]
    hd = q_ref.shape[1]
    nq = q_ref.shape[0] // tq

    def lanes(x, width):
        return jnp.tile(x, (1, width // LANE)) if width % LANE == 0 else x[:, :1]

    def ones_columns(vb):
        return jnp.concatenate([vb.astype(BF16), jnp.ones((vb.shape[0], LANE), BF16)], axis=1)

    for c in range(nq):
        rows = slice(c * tq, (c + 1) * tq)
        kb_scr[rows, :] = km_ref[rows, :].astype(BF16)
        v1_scr[rows, :] = ones_columns(vm_ref[rows, :])

    def block(j):
        return pl.ds(pl.multiple_of(j * tq, tq), tq)

    head_lane = lax.broadcasted_iota(jnp.int32, (tq, LANE), 1) == pl.program_id(1)

    def query_log_forget(rows):
        return jnp.sum(jnp.where(head_lane, cq_ref[0, rows, :], 0.0), axis=-1, keepdims=True)

    if prefix:
        kp = kp_ref[0].astype(BF16)
        vp1 = ones_columns(vp_ref[0])
        for c in range(nq):
            rows = slice(c * tq, (c + 1) * tq)
            s = lax.dot_general(q_ref[rows, :], kp, (((1,), (1,)), ((), ())),
                                preferred_element_type=F32) - ckp_ref[0]
            cq = query_log_forget(rows)
            m0 = jnp.max(s, axis=-1, keepdims=True) + cq
            p = jnp.exp2(s - (m0 - cq)).astype(BF16)
            m0_scr[rows, :] = jnp.broadcast_to(m0, (tq, LANE))
            acc0_scr[rows, :] = jnp.dot(p, vp1, preferred_element_type=F32)

    def query_block(qi, carry):
        q = q_ref[block(qi), :]
        if prefix:
            m_scr[...] = m0_scr[block(qi), :]
            acc_scr[...] = acc0_scr[block(qi), :]
        else:
            m_scr[...] = jnp.full(m_scr.shape, MASK_VALUE, F32)
            acc_scr[...] = jnp.zeros(acc_scr.shape, F32)
        cq_scr[...] = jnp.broadcast_to(query_log_forget(block(qi)), cq_scr.shape)

        def scores(s_ref, kb):
            s_ref[:, :kb.shape[0]] = lax.dot_general(q, kb, (((1,), (1,)), ((), ())), preferred_element_type=F32)

        def softmax(s_ref, p_ref, ck, causal):
            width = ck.shape[1]
            rb = min(tq, max(16, 32768 // max(width, LANE)))
            for r in range(tq // rb):
                rows = slice(r * rb, (r + 1) * rb)
                s = s_ref[rows, :width] - ck
                if causal:
                    row = r * rb + lax.broadcasted_iota(jnp.int32, (rb, width), 0)
                    s = jnp.where(lax.broadcasted_iota(jnp.int32, (rb, width), 1) <= row, s, MASK_VALUE)
                cq = cq_scr[rows, :]
                m_old = m_scr[rows, :]
                m_new = jnp.maximum(m_old, jnp.max(s, axis=-1, keepdims=True) + cq)
                p_ref[rows, :width] = jnp.exp2(s - lanes(m_new - cq, width)).astype(BF16)
                m_scr[rows, :] = m_new
                alpha_scr[rows, :] = jnp.exp2(m_old - m_new)

        def accumulate(p, v1, rescale):
            pv = jnp.dot(p, v1, preferred_element_type=F32)
            if rescale:
                acc_scr[...] = (acc_scr[...] + pv) * jnp.tile(alpha_scr[...], (1, 2))
            else:
                acc_scr[...] += pv

        scores(sa_scr, kb_scr[block(0), :])
        pb_scr[...] = jnp.zeros(pb_scr.shape, BF16)

        def pair(t, carry):
            j = 2 * t
            scores(sb_scr, kb_scr[block(j + 1), :])
            softmax(sa_scr, pa_scr, ckm_ref[0, j], False)
            accumulate(pb_scr[:, :tq], v1_scr[block(jnp.maximum(j - 1, 0)), :], True)
            scores(sa_scr, kb_scr[block(j + 2), :])
            softmax(sb_scr, pb_scr, ckm_ref[0, j + 1], False)
            accumulate(pa_scr[:, :tq], v1_scr[block(j), :], True)
            return carry

        lax.fori_loop(0, qi // 2, pair, 0)
        last = jnp.maximum(qi - 1, 0)

        @pl.when(qi % 2 == 0)
        def _():
            softmax(sa_scr, pa_scr, ckm_ref[0, qi], True)
            accumulate(pb_scr[:, :tq], v1_scr[block(last), :], True)
            accumulate(pa_scr[:, :tq], v1_scr[block(qi), :], False)

        @pl.when(qi % 2 == 1)
        def _():
            scores(sb_scr, kb_scr[block(qi), :])
            softmax(sa_scr, pa_scr, ckm_ref[0, last], False)
            accumulate(pb_scr[:, :tq], v1_scr[block(jnp.maximum(qi - 2, 0)), :], True)
            softmax(sb_scr, pb_scr, ckm_ref[0, qi], True)
            accumulate(pa_scr[:, :tq], v1_scr[block(last), :], True)
            accumulate(pb_scr[:, :tq], v1_scr[block(qi), :], False)

        o_ref[block(qi), :] = (acc_scr[:, :hd] / acc_scr[:, hd:]).astype(o_ref.dtype)
        return carry

    lax.fori_loop(0, nq, query_block, 0)


def fox_attention(q, cq, km, vm, ckm, prefix, *, nb, rows, row_off, heads, q_tile):
    hd = q.shape[1] // heads
    tq = q_tile
    nq = rows // tq
    assert rows % tq == 0 and row_off % rows == 0 and hd == LANE and heads <= LANE
    boff = row_off // rows
    in_specs = [pl.BlockSpec((rows, hd), lambda b, h: (boff + b, h)),
                pl.BlockSpec((1, rows, LANE), lambda b, h: (b, 0, 0)),
                pl.BlockSpec((rows, hd), lambda b, h: (boff + b, h)),
                pl.BlockSpec((rows, hd), lambda b, h: (boff + b, h)),
                pl.BlockSpec((1, nq, 1, tq), lambda b, h: (b * heads + h, 0, 0, 0))]
    args = [q, cq, km, vm, ckm.reshape(nb * heads, nq, 1, tq)]
    vmem = 2 * (2 * rows * hd * 2 + rows * LANE * 4 + 2 * rows * hd * 4 + 8 * rows * 4)
    if prefix is not None:
        kp, vp, ckp = prefix
        p = kp.shape[1]
        shared = kp.shape[0] == 1
        in_specs += [pl.BlockSpec((1, p, hd), lambda b, h: (0 if shared else b, 0, h)),
                     pl.BlockSpec((1, p, hd), lambda b, h: (0 if shared else b, 0, h)),
                     pl.BlockSpec((1, 1, p), lambda b, h: (h if shared else b * heads + h, 0, 0))]
        args += [kp, vp, ckp]
        vmem += 2 * (2 * p * hd * 4 + 8 * p * 4) + 4 * tq * max(p, LANE) * 4
    width = -(-tq // LANE) * LANE
    vmem += tq * width * 12 + 5 * tq * LANE * 4 + rows * hd * 6 + rows * (LANE + 2 * hd) * 4 + 2 ** 21
    return pl.pallas_call(
        functools.partial(_fox_kernel, tq=tq, prefix=prefix is not None),
        grid=(nb, heads),
        in_specs=in_specs,
        out_specs=pl.BlockSpec((rows, hd), lambda b, h: (b, h)),
        out_shape=jax.ShapeDtypeStruct((nb * rows, heads * hd), BF16),
        scratch_shapes=[pltpu.VMEM((tq, LANE), F32), pltpu.VMEM((tq, LANE), F32), pltpu.VMEM((tq, LANE), F32),
                        pltpu.VMEM((tq, 2 * hd), F32),
                        pltpu.VMEM((tq, width), F32), pltpu.VMEM((tq, width), F32),
                        pltpu.VMEM((tq, width), BF16), pltpu.VMEM((tq, width), BF16),
                        pltpu.VMEM((rows, hd), BF16), pltpu.VMEM((rows, 2 * hd), BF16),
                        pltpu.VMEM((rows, LANE), F32), pltpu.VMEM((rows, 2 * hd), F32)],
        compiler_params=_params(("parallel", "parallel"), vmem),
        name="fox_attention",
    )(*args)


def _rope_tables(pos, half):
    inv = ROPE_BASE ** (-jnp.arange(half, dtype=F32) / half)
    ang = pos.astype(F32)[:, None] * inv[None, :]
    return jnp.cos(ang), jnp.sin(ang)


def _head_major(c, heads):
    nb, t, _ = c.shape
    c2 = c * LOG2E
    return c2, jnp.swapaxes(c2[:, :, :heads], 1, 2).reshape(nb * heads, 1, t)


def kernel(x_prompt, x_sample, cache_k, cache_v, cache_logf, state_ret, state_conv, meta, g_attn, g_ffn,
           w_ret_in, w_ret_out, g_kv, w_kvf, b_f, g_k, w_q_b, g_q, w_o_b, w_gu, conv_w, conv_b, w_down):
    nbp, seq, d = x_prompt.shape
    nbs, dseq, _ = x_sample.shape
    n_meta = meta.shape[0]
    past = cache_k.shape[1]
    depth = g_attn.shape[0]
    n_a = w_ret_in.shape[0]
    h_a, dk_a, dv_a = state_ret.shape[2:]
    h_b, hd_b = cache_k.shape[2:]
    qk_w, v_w = h_a * dk_a, h_a * dv_a
    d_ff = w_down.shape[1]
    cw_taps = conv_w.shape[1]
    assert n_a == 1 and depth == 2 and cw_taps == 3, "kernel is written for one retention + one attention layer"
    assert d_ff % MXU_COLS == 0
    ff_tile = 1024 if d_ff >= 4096 else 2 * MXU_COLS
    ffp = -(-d_ff // ff_tile) * ff_tile
    m_main = nbp * seq
    m_s = nbs * dseq
    m_sm = m_s + n_meta
    m_rest = -(-m_sm // dseq) * dseq
    dt = x_prompt.dtype

    w_in = w_ret_in[0].astype(BF16)
    w_out = w_ret_out[0].astype(BF16)
    w_kv = w_kvf.astype(BF16)
    w_f = jnp.pad(w_kvf[:, 2 * d:], ((0, 0), (0, LANE - h_b))).astype(BF16)
    b_fp = jnp.pad(b_f, (0, LANE - h_b)).reshape(1, LANE)
    w_q = w_q_b[0].astype(BF16)
    w_o = w_o_b[0].astype(BF16)
    padc = lambda a: jnp.pad(a, ((0, 0),) * (a.ndim - 1) + ((0, ffp - d_ff),))
    w_gu_b = w_gu.astype(BF16)
    w_dn_p = jnp.pad(w_down, ((0, 0), (0, ffp - d_ff), (0, 0))).astype(BF16)
    cw_p = [jnp.pad(padc(conv_w[l]), ((0, SUBLANE - cw_taps), (0, 0))) for l in range(depth)]
    cb_p = [padc(conv_b[l]).reshape(1, ffp) for l in range(depth)]
    log_g = jnp.log1p(-jnp.exp2(-5.0 - jnp.arange(h_a, dtype=F32)))

    x_main = x_prompt.reshape(m_main, d)
    pad_rows = lambda a: jnp.pad(a, ((0, m_rest - m_sm),) + ((0, 0),) * (a.ndim - 1))
    x_rest = pad_rows(jnp.concatenate([x_sample.reshape(m_s, d), meta.astype(dt)], axis=0))
    cos_m, sin_m = _rope_tables(n_meta + jnp.arange(seq), dk_a // 2)
    pos_rest = pad_rows(jnp.concatenate([jnp.tile(n_meta + past + jnp.arange(dseq), nbs), jnp.arange(n_meta)]))
    cos_r, sin_r = _rope_tables(pos_rest, dk_a // 2)

    def state_rows(rows2):
        return jnp.pad(padc(rows2), ((0, 0), (SUBLANE - 2, 0), (0, 0)))

    def conv_ffn(x, l, state):
        (hn,) = rmsnorm_rows(x, g_ffn[l:l + 1])
        act, tails = matmul_conv_gate(hn, w_gu_b, l, state, cw_p[l], cb_p[l], rows_per_stream=seq)
        return matmul_residual(act, w_dn_p, x, layer=l), tails[:, SUBLANE - 2:, :d_ff]

    def ret_in(h, cos, sin, period):
        q = matmul_rope(h, w_in, cos, sin, col_off=0, n_out=qk_w, head_dim=dk_a, scale=1.0, period_rows=period)
        k = matmul_rope(h, w_in, cos, sin, col_off=qk_w, n_out=qk_w, head_dim=dk_a, scale=dk_a ** -0.5,
                        period_rows=period)
        v = matmul(h, w_in, col_off=2 * qk_w, n_out=v_w, out_dtype=BF16)
        gate = matmul(h, w_in, col_off=2 * qk_w + v_w, n_out=v_w, out_dtype=BF16)
        return q, k, v, gate

    (h_r,) = rmsnorm_rows(x_rest, g_attn[0:1])
    q_r, k_r, v_r, gate_r = ret_in(h_r, cos_r, sin_r, m_rest)
    og_s, ret_s = retention(log_g, q_r, k_r, v_r, gate_r, state_ret[0], nb=nbs, rows=dseq, chunk=dseq,
                            row_off=0, heads=h_a)
    og_m, ret_meta = retention(log_g, q_r, k_r, v_r, gate_r, jnp.zeros((1, h_a, dk_a, dv_a), F32), nb=1,
                               rows=n_meta, chunk=n_meta, row_off=m_s, heads=h_a)
    x_r1 = matmul_residual(pad_rows(jnp.concatenate([og_s, og_m], axis=0)), w_out, x_rest)
    assert n_meta <= dseq and n_meta % SUBLANE == 0
    state_r0 = jnp.concatenate([state_rows(state_conv[0]), jnp.zeros((1, SUBLANE, ffp), F32)], axis=0)

    def conv_ffn_rest(x, l, state):
        (hn,) = rmsnorm_rows(x, g_ffn[l:l + 1])
        a = matmul(hn, w_gu_b, layer=l, col_off=0, n_out=d_ff)
        u = matmul(hn, w_gu_b, layer=l, col_off=d_ff, n_out=d_ff)
        act = conv_gate(padc(a), padc(u), state, cw_p[l], cb_p[l], rows_per_stream=dseq, row_tile=dseq)
        return matmul_residual(act, w_dn_p, x, layer=l), a

    x_r2, a_r0 = conv_ffn_rest(x_r1, 0, state_r0)

    (h_m,) = rmsnorm_rows(x_main, g_attn[0:1])
    q_m, k_m, v_m, gate_m = ret_in(h_m, cos_m, sin_m, seq)
    chunk = _divisor_tile(seq, 256, 16)
    og_main, ret_p = retention(log_g, q_m, k_m, v_m, gate_m, ret_meta, nb=nbp, rows=seq, chunk=chunk,
                               row_off=0, heads=h_a)
    x_m1 = matmul_residual(og_main, w_out, x_main)
    state_m0 = jnp.broadcast_to(state_rows(a_r0[None, m_sm - 2:m_sm, :d_ff]), (nbp, SUBLANE, ffp))
    x_m2, conv_p0 = conv_ffn(x_m1, 0, state_m0)

    def kv_side(x, g_row):
        h, hk = rmsnorm_rows(x, jnp.stack([g_row, g_kv]))
        kk = matmul_headnorm(hk, w_kv, g_k, col_off=0, n_out=d, head_dim=hd_b, scale=1.0, out_dtype=F32)
        vv = matmul(hk, w_kv, col_off=d, n_out=d)
        lf = matmul_logsigmoid(hk, w_f, b_fp)
        q = matmul_headnorm(h, w_q, g_q[0], col_off=0, n_out=d, head_dim=hd_b, scale=hd_b ** -0.5 * LOG2E,
                            out_dtype=BF16)
        return q, kk, vv, lf

    q_r, kk_r, vv_r, lf_r = kv_side(x_r2, g_attn[1])
    q_m, kk_m, vv_m, lf_m = kv_side(x_m2, g_attn[1])

    c_cache = cumsum_rows(jnp.pad(cache_logf, ((0, 0), (0, 0), (0, LANE - h_b))), jnp.zeros((1, 1, LANE), F32))
    c_s = cumsum_rows(lf_r[:m_s].reshape(nbs, dseq, LANE), c_cache[:, past - 1:past, :])
    c_meta = cumsum_rows(lf_r[m_s:m_sm].reshape(1, n_meta, LANE), jnp.zeros((1, 1, LANE), F32))
    c_main = cumsum_rows(lf_m.reshape(nbp, seq, LANE), c_meta[:, n_meta - 1:n_meta, :])

    cq_s, ck_s = _head_major(c_s, h_b)
    _, ck_cache = _head_major(c_cache, h_b)
    o_s = fox_attention(q_r, cq_s, kk_r, vv_r, ck_s,
                        (cache_k.reshape(nbs, past, d), cache_v.reshape(nbs, past, d), ck_cache),
                        nb=nbs, rows=dseq, row_off=0, heads=h_b, q_tile=dseq)
    cq_meta, ck_meta = _head_major(c_meta, h_b)
    o_meta = fox_attention(q_r, cq_meta, kk_r, vv_r, ck_meta, None,
                           nb=1, rows=n_meta, row_off=m_s, heads=h_b, q_tile=n_meta)
    cq_main, ck_main = _head_major(c_main, h_b)
    k_meta = kk_r[m_s:m_sm]
    v_meta = vv_r[m_s:m_sm]
    o_main = fox_attention(q_m, cq_main, kk_m, vv_m, ck_main, (k_meta[None], v_meta[None], ck_meta),
                           nb=nbp, rows=seq, row_off=0, heads=h_b, q_tile=_divisor_tile(seq, 512, 16))

    x_r3 = matmul_residual(pad_rows(jnp.concatenate([o_s, o_meta], axis=0)), w_o, x_r2)
    state_r1 = jnp.concatenate([state_rows(state_conv[1]), jnp.zeros((1, SUBLANE, ffp), F32)], axis=0)
    x_r4, a_r1 = conv_ffn_rest(x_r3, 1, state_r1)
    x_m3 = matmul_residual(o_main, w_o, x_m2)
    state_m1 = jnp.broadcast_to(state_rows(a_r1[None, m_sm - 2:m_sm, :d_ff]), (nbp, SUBLANE, ffp))
    x_m4, conv_p1 = conv_ffn(x_m3, 1, state_m1)

    def with_meta(meta_rows, main_rows, tail):
        mr = jnp.broadcast_to(meta_rows.reshape((1, n_meta) + tail), (nbp, n_meta) + tail)
        return jnp.concatenate([mr, main_rows.reshape((nbp, seq) + tail)], axis=1)

    y_prompt = x_m4.reshape(nbp, seq, d)
    y_sample = x_r4[:m_s].reshape(nbs, dseq, d)
    k_prompt = with_meta(k_meta, kk_m, (h_b, hd_b))
    v_prompt = with_meta(v_meta, vv_m, (h_b, hd_b))
    logf_prompt = with_meta(lf_r[m_s:m_sm, :h_b], lf_m[:, :h_b], (h_b,))
    conv_prompt = jnp.stack([conv_p0, conv_p1])
    conv_sample = jnp.stack([a[:m_s].reshape(nbs, dseq, d_ff)[:, dseq - 2:] for a in (a_r0, a_r1)])
    return (y_prompt, y_sample, k_prompt, v_prompt, logf_prompt, ret_p[None], conv_prompt,
            kk_r[:m_s].reshape(nbs, dseq, h_b, hd_b), vv_r[:m_s].reshape(nbs, dseq, h_b, hd_b),
            lf_r[:m_s, :h_b].reshape(nbs, dseq, h_b), ret_s[None], conv_sample)
```

```python
import functools

import jax
import jax.numpy as jnp
from jax import lax
from jax.experimental import pallas as pl
from jax.experimental.pallas import tpu as pltpu

EPS = 1e-6
LOG2E = 1.4426950408889634
ROPE_BASE = 10000.0
MASK_VALUE = -1e30
LANE = 128
SUBLANE = 8
MXU_COLS = 256
VMEM_BUDGET = 46 * 2 ** 20
F32 = jnp.float32
BF16 = jnp.bfloat16


def _params(semantics, vmem_bytes):
    return pltpu.CompilerParams(dimension_semantics=semantics,
                                vmem_limit_bytes=int(min(vmem_bytes + 8 * 2 ** 20, 60 * 2 ** 20)))


def _divisor_tile(n, cap, mult):
    if n <= cap:
        return n
    t = cap - cap % mult
    while t >= mult:
        if n % t == 0:
            return t
        t -= mult
    raise ValueError(f"no tile for {n} (cap {cap}, multiple {mult})")


def _rmsnorm_kernel(x_ref, g_ref, *o_refs):
    x = x_ref[...]
    y = x * lax.rsqrt(jnp.mean(x * x, axis=-1, keepdims=True) + EPS)
    for n, o_ref in enumerate(o_refs):
        o_ref[...] = (y * g_ref[n:n + 1, :]).astype(o_ref.dtype)


def rmsnorm_rows(x, gains):
    m, d = x.shape
    g = gains.shape[0]
    tm = _divisor_tile(m, 256, 16)
    vmem = 2 * tm * d * (4 + 2 * g)
    outs = pl.pallas_call(
        _rmsnorm_kernel,
        grid=(m // tm,),
        in_specs=[pl.BlockSpec((tm, d), lambda i: (i, 0)),
                  pl.BlockSpec((g, d), lambda i: (0, 0))],
        out_specs=[pl.BlockSpec((tm, d), lambda i: (i, 0)) for _ in range(g)],
        out_shape=[jax.ShapeDtypeStruct((m, d), BF16) for _ in range(g)],
        compiler_params=_params(("parallel",), vmem),
        name="rmsnorm",
    )(x, gains)
    return tuple(outs)


def _epi_plain(acc, o_ref):
    o_ref[...] = acc.astype(o_ref.dtype)


def _epi_residual(acc, res_ref, o_ref):
    o_ref[...] = res_ref[...] + acc


def _epi_rope(acc, cos_ref, sin_ref, o_ref, *, head_dim, scale):
    half = head_dim // 2
    cos = cos_ref[...]
    sin = sin_ref[...]
    for h in range(acc.shape[1] // head_dim):
        x1 = acc[:, h * head_dim:h * head_dim + half]
        x2 = acc[:, h * head_dim + half:(h + 1) * head_dim]
        o_ref[:, h * head_dim:h * head_dim + half] = ((x1 * cos - x2 * sin) * scale).astype(o_ref.dtype)
        o_ref[:, h * head_dim + half:(h + 1) * head_dim] = ((x1 * sin + x2 * cos) * scale).astype(o_ref.dtype)


def _epi_headnorm(acc, g_ref, o_ref, *, head_dim, scale):
    g = g_ref[...]
    for h in range(acc.shape[1] // head_dim):
        x = acc[:, h * head_dim:(h + 1) * head_dim]
        y = x * lax.rsqrt(jnp.mean(x * x, axis=-1, keepdims=True) + EPS) * g
        o_ref[:, h * head_dim:(h + 1) * head_dim] = (y * scale).astype(o_ref.dtype)


def _epi_logsigmoid(acc, b_ref, o_ref):
    z = acc + b_ref[...]
    o_ref[...] = jnp.minimum(z, 0.0) - jnp.log1p(jnp.exp(-jnp.abs(z)))


def _mm_kernel(x_ref, w_ref, *rest, epilogue, nk):
    if nk == 1:
        epilogue(jnp.dot(x_ref[...], w_ref[...], preferred_element_type=F32), *rest)
        return
    acc_ref = rest[-1]
    k = pl.program_id(2)

    @pl.when(k == 0)
    def _():
        acc_ref[...] = jnp.zeros_like(acc_ref)

    acc_ref[...] += jnp.dot(x_ref[...], w_ref[...], preferred_element_type=F32)

    @pl.when(k == nk - 1)
    def _():
        epilogue(acc_ref[...], *rest[:-1])


def _mm_tiles(m, k, n, out_bytes, extra_bytes_per_elem):
    for row_cap, steps in ((1024, (1,)), (512, (2, 4, 8)), (1024, (2, 4, 8))):
        tm = _divisor_tile(m, row_cap, 16)
        for tn in (1024, 512, 256, 128):
            if n % tn:
                continue
            for nk in steps:
                if k % nk or (k // nk) % LANE:
                    continue
                tk = k // nk
                ws = 2 * (tm * tk * 2 + tk * tn * 2 + tm * tn * (out_bytes + extra_bytes_per_elem))
                ws += tm * tn * 4
                if ws <= VMEM_BUDGET:
                    return tm, tn, tk, ws
    raise ValueError(f"no matmul tiling for {(m, k, n)}")


def matmul(x, w, *, col_off=0, n_out=None, epilogue=_epi_plain, extras=(), extra_specs=lambda tm, tn: [],
           out_dtype=F32, extra_bytes_per_elem=0, row_tile=None, layer=None):
    m, k = x.shape
    n_out = w.shape[-1] - col_off if n_out is None else n_out
    tm, tn, tk, ws = _mm_tiles(m, k, n_out, jnp.dtype(out_dtype).itemsize, extra_bytes_per_elem)
    if row_tile is not None:
        tm = row_tile
    assert col_off % tn == 0 and m % tm == 0
    nk = k // tk
    joff = col_off // tn
    kern = functools.partial(_mm_kernel, epilogue=epilogue, nk=nk)
    if layer is None:
        w_spec = pl.BlockSpec((tk, tn), lambda i, j, kk: (kk, j + joff))
    else:
        w_spec = pl.BlockSpec((None, tk, tn), lambda i, j, kk: (layer, kk, j + joff))
    return pl.pallas_call(
        kern,
        grid=(m // tm, n_out // tn, nk),
        in_specs=[pl.BlockSpec((tm, tk), lambda i, j, kk: (i, kk)), w_spec, *extra_specs(tm, tn)],
        out_specs=pl.BlockSpec((tm, tn), lambda i, j, kk: (i, j)),
        out_shape=jax.ShapeDtypeStruct((m, n_out), out_dtype),
        scratch_shapes=[pltpu.VMEM((tm, tn), F32)] if nk > 1 else [],
        compiler_params=_params(("parallel", "parallel", "arbitrary"), ws),
        name="matmul_" + getattr(epilogue, "func", epilogue).__name__.lstrip("_"),
    )(x, w, *extras)


def matmul_residual(x, w, res, layer=None):
    return matmul(x, w, epilogue=_epi_residual, extras=(res,), extra_bytes_per_elem=4, layer=layer,
                  extra_specs=lambda tm, tn: [pl.BlockSpec((tm, tn), lambda i, j, kk: (i, j))])


def matmul_rope(x, w, cos, sin, *, col_off, n_out, head_dim, scale, period_rows):
    half = head_dim // 2

    def specs(tm, tn):
        assert period_rows % tm == 0 and tn % head_dim == 0
        nper = period_rows // tm
        return [pl.BlockSpec((tm, half), lambda i, j, kk: (i % nper, 0))] * 2

    row_tile = _divisor_tile(period_rows, _divisor_tile(x.shape[0], 1024, 16), 16)
    return matmul(x, w, col_off=col_off, n_out=n_out, out_dtype=BF16, extras=(cos, sin), extra_specs=specs,
                  row_tile=row_tile,
                  epilogue=functools.partial(_epi_rope, head_dim=head_dim, scale=scale))


def matmul_headnorm(x, w, gain, *, col_off, n_out, head_dim, scale, out_dtype):
    return matmul(x, w, col_off=col_off, n_out=n_out, out_dtype=out_dtype, extras=(gain.reshape(1, head_dim),),
                  extra_specs=lambda tm, tn: [pl.BlockSpec((1, head_dim), lambda i, j, kk: (0, 0))],
                  epilogue=functools.partial(_epi_headnorm, head_dim=head_dim, scale=scale))


def matmul_logsigmoid(x, w, bias):
    return matmul(x, w, extras=(bias,), epilogue=_epi_logsigmoid,
                  extra_specs=lambda tm, tn: [pl.BlockSpec((1, tn), lambda i, j, kk: (0, j))])


def _retention_kernel(lg_ref, q_ref, k_ref, v_ref, gate_ref, s0_ref, og_ref, sout_ref, s_scr, decay_scr, *,
                      nc, hps):
    hg = pl.program_id(1)
    c = pl.program_id(2)
    t = q_ref.shape[0]
    dk = q_ref.shape[1] // hps
    dv = v_ref.shape[1] // hps
    idx = lax.broadcasted_iota(jnp.int32, (t, 1), 0).astype(F32)

    @pl.when(c == 0)
    def _():
        rel = (lax.broadcasted_iota(jnp.int32, (t, t), 0) - lax.broadcasted_iota(jnp.int32, (t, t), 1)).astype(F32)
        causal = rel >= 0.0
        for hh in range(hps):
            s_scr[hh] = s0_ref[0, hh]
            decay_scr[hh] = jnp.where(causal, jnp.exp(jnp.where(causal, rel, 0.0) * lg_ref[hg * hps + hh]), 0.0)

    for hh in range(hps):
        lg = lg_ref[hg * hps + hh]
        q = q_ref[:, hh * dk:(hh + 1) * dk]
        k = k_ref[:, hh * dk:(hh + 1) * dk]
        v = v_ref[:, hh * dv:(hh + 1) * dv]
        scores = lax.dot_general(q, k, (((1,), (1,)), ((), ())), preferred_element_type=F32) * decay_scr[hh]
        o = jnp.dot(scores.astype(BF16), v, preferred_element_type=F32)
        s_old = s_scr[hh]
        q_dec = (q.astype(F32) * jnp.exp((idx + 1.0) * lg)).astype(BF16)
        o = o + jnp.dot(q_dec, s_old.astype(BF16), preferred_element_type=F32)
        k_dec = (k.astype(F32) * jnp.exp((t - 1.0 - idx) * lg)).astype(BF16)
        kv = lax.dot_general(k_dec, v, (((0,), (0,)), ((), ())), preferred_element_type=F32)
        s_new = jnp.exp(jnp.full((1, 1), t, F32) * lg) * s_old + kv
        s_scr[hh] = s_new
        on = o * lax.rsqrt(jnp.mean(o * o, axis=-1, keepdims=True) + EPS)
        g = gate_ref[:, hh * dv:(hh + 1) * dv].astype(F32)
        og_ref[:, hh * dv:(hh + 1) * dv] = ((g * jax.nn.sigmoid(g)) * on).astype(og_ref.dtype)

        @pl.when(c == nc - 1)
        def _():
            sout_ref[0, hh] = s_new


def retention(log_g, q, k, v, gate, s0, *, nb, rows, chunk, row_off, heads):
    dk = q.shape[1] // heads
    dv = v.shape[1] // heads
    nc = rows // chunk
    hps = 2 if heads % 2 == 0 else 1
    assert rows % chunk == 0 and row_off % chunk == 0
    roff = row_off // chunk
    shared = s0.shape[0] == 1
    row_map = lambda b, h, c: (roff + b * nc + c, h)
    vmem = hps * (2 * chunk * (2 * dk * 2 + 3 * dv * 2) + 5 * dk * dv * 4 + 8 * chunk * chunk * 4)
    og, s_out = pl.pallas_call(
        functools.partial(_retention_kernel, nc=nc, hps=hps),
        grid=(nb, heads // hps, nc),
        in_specs=[pl.BlockSpec(memory_space=pltpu.SMEM),
                  pl.BlockSpec((chunk, hps * dk), row_map),
                  pl.BlockSpec((chunk, hps * dk), row_map),
                  pl.BlockSpec((chunk, hps * dv), row_map),
                  pl.BlockSpec((chunk, hps * dv), row_map),
                  pl.BlockSpec((1, hps, dk, dv), lambda b, h, c: (0 if shared else b, h, 0, 0))],
        out_specs=[pl.BlockSpec((chunk, hps * dv), lambda b, h, c: (b * nc + c, h)),
                   pl.BlockSpec((1, hps, dk, dv), lambda b, h, c: (b, h, 0, 0))],
        out_shape=[jax.ShapeDtypeStruct((nb * rows, heads * dv), BF16),
                   jax.ShapeDtypeStruct((nb, heads, dk, dv), F32)],
        scratch_shapes=[pltpu.VMEM((hps, dk, dv), F32), pltpu.VMEM((hps, chunk, chunk), F32)],
        compiler_params=_params(("parallel", "parallel", "arbitrary"), vmem),
        name="retention",
    )(log_g, q, k, v, gate, s0)
    return og, s_out


def _conv_silu_gate(a, u, halo, cw, cb):
    r1 = pltpu.roll(a, 1, axis=0)
    r2 = pltpu.roll(a, 2, axis=0)
    row = lax.broadcasted_iota(jnp.int32, (SUBLANE, a.shape[1]), 0)
    top1 = jnp.where(row == 0, halo[7:8, :], r1[:SUBLANE])
    top2 = jnp.where(row == 0, halo[6:7, :], jnp.where(row == 1, halo[7:8, :], r2[:SUBLANE]))
    prev1 = jnp.concatenate([top1, r1[SUBLANE:]], axis=0)
    prev2 = jnp.concatenate([top2, r2[SUBLANE:]], axis=0)
    c = cb + cw[0:1, :] * prev2
    c = c + cw[1:2, :] * prev1
    c = c + cw[2:3, :] * a
    return (c * jax.nn.sigmoid(c)) * u


def _conv_gate_kernel(a_ref, u_ref, halo_ref, state_ref, cw_ref, cb_ref, o_ref, *, tiles_per_stream):
    first = (pl.program_id(0) % tiles_per_stream) == 0
    halo = jnp.where(first, state_ref[0], halo_ref[...])
    o_ref[...] = _conv_silu_gate(a_ref[...], u_ref[...], halo, cw_ref[...], cb_ref[...]).astype(o_ref.dtype)


def conv_gate(a, u, state, cw, cb, *, rows_per_stream, row_tile):
    m, f = a.shape
    tr = row_tile
    tc = _divisor_tile(f, 1024, LANE)
    assert rows_per_stream % tr == 0 and tr % SUBLANE == 0
    tps = rows_per_stream // tr
    nfb = f // tc
    hb = tr // SUBLANE
    vmem = 2 * (tr * tc * (4 + 4 + 2) + 3 * 8 * tc * 4) + 6 * tr * tc * 4
    return pl.pallas_call(
        functools.partial(_conv_gate_kernel, tiles_per_stream=tps),
        grid=(m // tr, nfb),
        in_specs=[pl.BlockSpec((tr, tc), lambda i, j: (i, j)),
                  pl.BlockSpec((tr, tc), lambda i, j: (i, j)),
                  pl.BlockSpec((SUBLANE, tc), lambda i, j: (jnp.maximum(i * hb - 1, 0), j)),
                  pl.BlockSpec((1, SUBLANE, tc), lambda i, j: (i // tps, 0, j)),
                  pl.BlockSpec((SUBLANE, tc), lambda i, j: (0, j)),
                  pl.BlockSpec((1, tc), lambda i, j: (0, j))],
        out_specs=pl.BlockSpec((tr, tc), lambda i, j: (i, j)),
        out_shape=jax.ShapeDtypeStruct((m, f), BF16),
        compiler_params=_params(("parallel", "parallel"), vmem),
        name="conv_gate",
    )(a, u, a, state, cw, cb)


def _gate_up_conv_kernel(x_ref, *refs, tiles_per_stream, nsub, blocks):
    wa_refs, wu_refs = refs[:nsub], refs[nsub:2 * nsub]
    state_ref, cw_ref, cb_ref, o_ref, tail_ref, halo_scr = refs[2 * nsub:]
    i = pl.program_id(0)
    j = pl.program_id(1)
    first = (i % tiles_per_stream) == 0
    tm = x_ref.shape[0]
    for s in range(nsub):
        cols = slice(s * MXU_COLS, (s + 1) * MXU_COLS)
        a = jnp.dot(x_ref[...], wa_refs[s][...], preferred_element_type=F32)
        u = jnp.dot(x_ref[...], wu_refs[s][...], preferred_element_type=F32)
        halo = jnp.where(first, state_ref[0, :, cols], halo_scr[j, :, cols])
        act = _conv_silu_gate(a, u, halo, cw_ref[:, cols], cb_ref[:, cols])
        o_ref[:, cols] = jnp.where(j * nsub + s < blocks, act, 0.0).astype(o_ref.dtype)
        tail = a[tm - SUBLANE:, :]
        halo_scr[j, :, cols] = tail
        tail_ref[0, :, cols] = tail


def matmul_conv_gate(x, w, layer, state, cw, cb, *, rows_per_stream):
    m, k = x.shape
    f = w.shape[2] // 2
    fp = cw.shape[1]
    tm = _divisor_tile(rows_per_stream, 1024, 16)
    nsub = 2
    tn = nsub * MXU_COLS
    assert f % MXU_COLS == 0 and fp % tn == 0 and m % rows_per_stream == 0
    blocks = f // MXU_COLS
    tps = rows_per_stream // tm
    nfb = fp // tn
    vmem = 2 * (tm * k * 2 + 2 * k * tn * 2 + tm * tn * 2) + SUBLANE * fp * 4 + 8 * tm * MXU_COLS * 4

    def weight_spec(branch, s):
        return pl.BlockSpec((None, k, MXU_COLS),
                            lambda i, j: (layer, 0, branch * blocks + jnp.minimum(j * nsub + s, blocks - 1)))

    act, tails = pl.pallas_call(
        functools.partial(_gate_up_conv_kernel, tiles_per_stream=tps, nsub=nsub, blocks=blocks),
        grid=(m // tm, nfb),
        in_specs=[pl.BlockSpec((tm, k), lambda i, j: (i, 0)),
                  *[weight_spec(0, s) for s in range(nsub)],
                  *[weight_spec(1, s) for s in range(nsub)],
                  pl.BlockSpec((1, SUBLANE, tn), lambda i, j: (i // tps, 0, j)),
                  pl.BlockSpec((SUBLANE, tn), lambda i, j: (0, j)),
                  pl.BlockSpec((1, tn), lambda i, j: (0, j))],
        out_specs=[pl.BlockSpec((tm, tn), lambda i, j: (i, j)),
                   pl.BlockSpec((1, SUBLANE, tn), lambda i, j: (i, 0, j))],
        out_shape=[jax.ShapeDtypeStruct((m, fp), BF16),
                   jax.ShapeDtypeStruct((m // tm, SUBLANE, fp), F32)],
        scratch_shapes=[pltpu.VMEM((nfb, SUBLANE, tn), F32)],
        compiler_params=_params(("arbitrary", "arbitrary"), vmem),
        name="matmul_gate_up_conv",
    )(x, *([w] * (2 * nsub)), state, cw, cb)
    return act, tails[tps - 1::tps]


def _cumsum_kernel(x_ref, c0_ref, o_ref, *, blk):
    t = x_ref.shape[1]
    r = lax.broadcasted_iota(jnp.int32, (blk, blk), 0)
    s = lax.broadcasted_iota(jnp.int32, (blk, blk), 1)
    tri = (s <= r).astype(BF16)
    carry = c0_ref[0]
    for b in range(t // blk):
        x = x_ref[0, b * blk:(b + 1) * blk, :]
        x_hi = x.astype(BF16)
        r1 = x - x_hi.astype(F32)
        x_mid = r1.astype(BF16)
        x_lo = (r1 - x_mid.astype(F32)).astype(BF16)
        c = jnp.dot(tri, x_hi, preferred_element_type=F32)
        c = c + jnp.dot(tri, x_mid, preferred_element_type=F32)
        c = c + jnp.dot(tri, x_lo, preferred_element_type=F32)
        c = c + carry
        o_ref[0, b * blk:(b + 1) * blk, :] = c
        carry = c[blk - 1:blk, :]


def cumsum_rows(x, c0):
    nb, t, l = x.shape
    blk = _divisor_tile(t, 256, 16)
    shared = c0.shape[0] == 1
    return pl.pallas_call(
        functools.partial(_cumsum_kernel, blk=blk),
        grid=(nb,),
        in_specs=[pl.BlockSpec((1, t, l), lambda b: (b, 0, 0)),
                  pl.BlockSpec((1, 1, l), lambda b: (0 if shared else b, 0, 0))],
        out_specs=pl.BlockSpec((1, t, l), lambda b: (b, 0, 0)),
        out_shape=jax.ShapeDtypeStruct((nb, t, l), F32),
        compiler_params=_params(("parallel",), 4 * t * l * 4 + 2 ** 20),
        name="cumsum",
    )(x, c0)


def _fox_kernel(*refs, tq, prefix):
    m_scr, cq_scr, alpha_scr, acc_scr, sa_scr, sb_scr, pa_scr, pb_scr, kb_scr, v1_scr, m0_scr, acc0_scr = refs[-12:]
    if prefix:
        q_ref, cq_ref, km_ref, vm_ref, ckm_ref, kp_ref, vp_ref, ckp_ref, o_ref = refs[:-12]
    else:
        q_ref, cq_ref, km_ref, vm_ref, ckm_ref, o_ref = refs[:-12]
    hd = q_ref.shape[1]
    nq = q_ref.shape[0] // tq

    def lanes(x, width):
        return jnp.tile(x, (1, width // LANE)) if width % LANE == 0 else x[:, :1]

    def ones_columns(vb):
        return jnp.concatenate([vb.astype(BF16), jnp.ones((vb.shape[0], LANE), BF16)], axis=1)

    for c in range(nq):
        rows = slice(c * tq, (c + 1) * tq)
        kb_scr[rows, :] = km_ref[rows, :].astype(BF16)
        v1_scr[rows, :] = ones_columns(vm_ref[rows, :])

    def block(j):
        return pl.ds(pl.multiple_of(j * tq, tq), tq)

    head_lane = lax.broadcasted_iota(jnp.int32, (tq, LANE), 1) == pl.program_id(1)

    def query_log_forget(rows):
        return jnp.sum(jnp.where(head_lane, cq_ref[0, rows, :], 0.0), axis=-1, keepdims=True)

    if prefix:
        kp = kp_ref[0].astype(BF16)
        vp1 = ones_columns(vp_ref[0])
        for c in range(nq):
            rows = slice(c * tq, (c + 1) * tq)
            s = lax.dot_general(q_ref[rows, :], kp, (((1,), (1,)), ((), ())),
                                preferred_element_type=F32) - ckp_ref[0]
            cq = query_log_forget(rows)
            m0 = jnp.max(s, axis=-1, keepdims=True) + cq
            p = jnp.exp2(s - (m0 - cq)).astype(BF16)
            m0_scr[rows, :] = jnp.broadcast_to(m0, (tq, LANE))
            acc0_scr[rows, :] = jnp.dot(p, vp1, preferred_element_type=F32)

    def query_block(qi, carry):
        q = q_ref[block(qi), :]
        if prefix:
            m_scr[...] = m0_scr[block(qi), :]
            acc_scr[...] = acc0_scr[block(qi), :]
        else:
            m_scr[...] = jnp.full(m_scr.shape, MASK_VALUE, F32)
            acc_scr[...] = jnp.zeros(acc_scr.shape, F32)
        cq_scr[...] = jnp.broadcast_to(query_log_forget(block(qi)), cq_scr.shape)

        def scores(s_ref, kb):
            s_ref[:, :kb.shape[0]] = lax.dot_general(q, kb, (((1,), (1,)), ((), ())), preferred_element_type=F32)

        def softmax(s_ref, p_ref, ck, causal):
            width = ck.shape[1]
            rb = min(tq, max(16, 32768 // max(width, LANE)))
            for r in range(tq // rb):
                rows = slice(r * rb, (r + 1) * rb)
                s = s_ref[rows, :width] - ck
                if causal:
                    row = r * rb + lax.broadcasted_iota(jnp.int32, (rb, width), 0)
                    s = jnp.where(lax.broadcasted_iota(jnp.int32, (rb, width), 1) <= row, s, MASK_VALUE)
                cq = cq_scr[rows, :]
                m_old = m_scr[rows, :]
                m_new = jnp.maximum(m_old, jnp.max(s, axis=-1, keepdims=True) + cq)
                p_ref[rows, :width] = jnp.exp2(s - lanes(m_new - cq, width)).astype(BF16)
                m_scr[rows, :] = m_new
                alpha_scr[rows, :] = jnp.exp2(m_old - m_new)

        def accumulate(p, v1, rescale):
            pv = jnp.dot(p, v1, preferred_element_type=F32)
            if rescale:
                acc_scr[...] = (acc_scr[...] + pv) * jnp.tile(alpha_scr[...], (1, 2))
            else:
                acc_scr[...] += pv

        scores(sa_scr, kb_scr[block(0), :])
        pb_scr[...] = jnp.zeros(pb_scr.shape, BF16)

        def pair(t, carry):
            j = 2 * t
            scores(sb_scr, kb_scr[block(j + 1), :])
            softmax(sa_scr, pa_scr, ckm_ref[0, j], False)
            accumulate(pb_scr[:, :tq], v1_scr[block(jnp.maximum(j - 1, 0)), :], True)
            scores(sa_scr, kb_scr[block(j + 2), :])
            softmax(sb_scr, pb_scr, ckm_ref[0, j + 1], False)
            accumulate(pa_scr[:, :tq], v1_scr[block(j), :], True)
            return carry

        lax.fori_loop(0, qi // 2, pair, 0)
        last = jnp.maximum(qi - 1, 0)

        @pl.when(qi % 2 == 0)
        def _():
            softmax(sa_scr, pa_scr, ckm_ref[0, qi], True)
            accumulate(pb_scr[:, :tq], v1_scr[block(last), :], True)
            accumulate(pa_scr[:, :tq], v1_scr[block(qi), :], False)

        @pl.when(qi % 2 == 1)
        def _():
            scores(sb_scr, kb_scr[block(qi), :])
            softmax(sa_scr, pa_scr, ckm_ref[0, last], False)
            accumulate(pb_scr[:, :tq], v1_scr[block(jnp.maximum(qi - 2, 0)), :], True)
            softmax(sb_scr, pb_scr, ckm_ref[0, qi], True)
            accumulate(pa_scr[:, :tq], v1_scr[block(last), :], True)
            accumulate(pb_scr[:, :tq], v1_scr[block(qi), :], False)

        o_ref[block(qi), :] = (acc_scr[:, :hd] / acc_scr[:, hd:]).astype(o_ref.dtype)
        return carry

    lax.fori_loop(0, nq, query_block, 0)


def fox_attention(q, cq, km, vm, ckm, prefix, *, nb, rows, row_off, heads, q_tile):
    hd = q.shape[1] // heads
    tq = q_tile
    nq = rows // tq
    assert rows % tq == 0 and row_off % rows == 0 and hd == LANE and heads <= LANE
    boff = row_off // rows
    in_specs = [pl.BlockSpec((rows, hd), lambda b, h: (boff + b, h)),
                pl.BlockSpec((1, rows, LANE), lambda b, h: (b, 0, 0)),
                pl.BlockSpec((rows, hd), lambda b, h: (boff + b, h)),
                pl.BlockSpec((rows, hd), lambda b, h: (boff + b, h)),
                pl.BlockSpec((1, nq, 1, tq), lambda b, h: (b * heads + h, 0, 0, 0))]
    args = [q, cq, km, vm, ckm.reshape(nb * heads, nq, 1, tq)]
    vmem = 2 * (2 * rows * hd * 2 + rows * LANE * 4 + 2 * rows * hd * 4 + 8 * rows * 4)
    if prefix is not None:
        kp, vp, ckp = prefix
        p = kp.shape[1]
        shared = kp.shape[0] == 1
        in_specs += [pl.BlockSpec((1, p, hd), lambda b, h: (0 if shared else b, 0, h)),
                     pl.BlockSpec((1, p, hd), lambda b, h: (0 if shared else b, 0, h)),
                     pl.BlockSpec((1, 1, p), lambda b, h: (h if shared else b * heads + h, 0, 0))]
        args += [kp, vp, ckp]
        vmem += 2 * (2 * p * hd * 4 + 8 * p * 4) + 4 * tq * max(p, LANE) * 4
    width = -(-tq // LANE) * LANE
    vmem += tq * width * 12 + 5 * tq * LANE * 4 + rows * hd * 6 + rows * (LANE + 2 * hd) * 4 + 2 ** 21
    return pl.pallas_call(
        functools.partial(_fox_kernel, tq=tq, prefix=prefix is not None),
        grid=(nb, heads),
        in_specs=in_specs,
        out_specs=pl.BlockSpec((rows, hd), lambda b, h: (b, h)),
        out_shape=jax.ShapeDtypeStruct((nb * rows, heads * hd), BF16),
        scratch_shapes=[pltpu.VMEM((tq, LANE), F32), pltpu.VMEM((tq, LANE), F32), pltpu.VMEM((tq, LANE), F32),
                        pltpu.VMEM((tq, 2 * hd), F32),
                        pltpu.VMEM((tq, width), F32), pltpu.VMEM((tq, width), F32),
                        pltpu.VMEM((tq, width), BF16), pltpu.VMEM((tq, width), BF16),
                        pltpu.VMEM((rows, hd), BF16), pltpu.VMEM((rows, 2 * hd), BF16),
                        pltpu.VMEM((rows, LANE), F32), pltpu.VMEM((rows, 2 * hd), F32)],
        compiler_params=_params(("parallel", "parallel"), vmem),
        name="fox_attention",
    )(*args)


def _rope_tables(pos, half):
    inv = ROPE_BASE ** (-jnp.arange(half, dtype=F32) / half)
    ang = pos.astype(F32)[:, None] * inv[None, :]
    return jnp.cos(ang), jnp.sin(ang)


def _head_major(c, heads):
    nb, t, _ = c.shape
    c2 = c * LOG2E
    return c2, jnp.swapaxes(c2[:, :, :heads], 1, 2).reshape(nb * heads, 1, t)


def kernel(x_prompt, x_sample, cache_k, cache_v, cache_logf, state_ret, state_conv, meta, g_attn, g_ffn,
           w_ret_in, w_ret_out, g_kv, w_kvf, b_f, g_k, w_q_b, g_q, w_o_b, w_gu, conv_w, conv_b, w_down):
    nbp, seq, d = x_prompt.shape
    nbs, dseq, _ = x_sample.shape
    n_meta = meta.shape[0]
    past = cache_k.shape[1]
    depth = g_attn.shape[0]
    n_a = w_ret_in.shape[0]
    h_a, dk_a, dv_a = state_ret.shape[2:]
    h_b, hd_b = cache_k.shape[2:]
    qk_w, v_w = h_a * dk_a, h_a * dv_a
    d_ff = w_down.shape[1]
    cw_taps = conv_w.shape[1]
    assert n_a == 1 and depth == 2 and cw_taps == 3, "kernel is written for one retention + one attention layer"
    assert d_ff % MXU_COLS == 0
    ff_tile = 1024 if d_ff >= 4096 else 2 * MXU_COLS
    ffp = -(-d_ff // ff_tile) * ff_tile
    m_main = nbp * seq
    m_s = nbs * dseq
    m_sm = m_s + n_meta
    m_rest = -(-m_sm // dseq) * dseq
    dt = x_prompt.dtype

    w_in = w_ret_in[0].astype(BF16)
    w_out = w_ret_out[0].astype(BF16)
    w_kv = w_kvf.astype(BF16)
    w_f = jnp.pad(w_kvf[:, 2 * d:], ((0, 0), (0, LANE - h_b))).astype(BF16)
    b_fp = jnp.pad(b_f, (0, LANE - h_b)).reshape(1, LANE)
    w_q = w_q_b[0].astype(BF16)
    w_o = w_o_b[0].astype(BF16)
    padc = lambda a: jnp.pad(a, ((0, 0),) * (a.ndim - 1) + ((0, ffp - d_ff),))
    w_gu_b = w_gu.astype(BF16)
    w_dn_p = jnp.pad(w_down, ((0, 0), (0, ffp - d_ff), (0, 0))).astype(BF16)
    cw_p = [jnp.pad(padc(conv_w[l]), ((0, SUBLANE - cw_taps), (0, 0))) for l in range(depth)]
    cb_p = [padc(conv_b[l]).reshape(1, ffp) for l in range(depth)]
    log_g = jnp.log1p(-jnp.exp2(-5.0 - jnp.arange(h_a, dtype=F32)))

    x_main = x_prompt.reshape(m_main, d)
    pad_rows = lambda a: jnp.pad(a, ((0, m_rest - m_sm),) + ((0, 0),) * (a.ndim - 1))
    x_rest = pad_rows(jnp.concatenate([x_sample.reshape(m_s, d), meta.astype(dt)], axis=0))
    cos_m, sin_m = _rope_tables(n_meta + jnp.arange(seq), dk_a // 2)
    pos_rest = pad_rows(jnp.concatenate([jnp.tile(n_meta + past + jnp.arange(dseq), nbs), jnp.arange(n_meta)]))
    cos_r, sin_r = _rope_tables(pos_rest, dk_a // 2)

    def state_rows(rows2):
        return jnp.pad(padc(rows2), ((0, 0), (SUBLANE - 2, 0), (0, 0)))

    def conv_ffn(x, l, state):
        (hn,) = rmsnorm_rows(x, g_ffn[l:l + 1])
        act, tails = matmul_conv_gate(hn, w_gu_b, l, state, cw_p[l], cb_p[l], rows_per_stream=seq)
        return matmul_residual(act, w_dn_p, x, layer=l), tails[:, SUBLANE - 2:, :d_ff]

    def ret_in(h, cos, sin, period):
        q = matmul_rope(h, w_in, cos, sin, col_off=0, n_out=qk_w, head_dim=dk_a, scale=1.0, period_rows=period)
        k = matmul_rope(h, w_in, cos, sin, col_off=qk_w, n_out=qk_w, head_dim=dk_a, scale=dk_a ** -0.5,
                        period_rows=period)
        v = matmul(h, w_in, col_off=2 * qk_w, n_out=v_w, out_dtype=BF16)
        gate = matmul(h, w_in, col_off=2 * qk_w + v_w, n_out=v_w, out_dtype=BF16)
        return q, k, v, gate

    (h_r,) = rmsnorm_rows(x_rest, g_attn[0:1])
    q_r, k_r, v_r, gate_r = ret_in(h_r, cos_r, sin_r, m_rest)
    og_s, ret_s = retention(log_g, q_r, k_r, v_r, gate_r, state_ret[0], nb=nbs, rows=dseq, chunk=dseq,
                            row_off=0, heads=h_a)
    og_m, ret_meta = retention(log_g, q_r, k_r, v_r, gate_r, jnp.zeros((1, h_a, dk_a, dv_a), F32), nb=1,
                               rows=n_meta, chunk=n_meta, row_off=m_s, heads=h_a)
    x_r1 = matmul_residual(pad_rows(jnp.concatenate([og_s, og_m], axis=0)), w_out, x_rest)
    assert n_meta <= dseq and n_meta % SUBLANE == 0
    state_r0 = jnp.concatenate([state_rows(state_conv[0]), jnp.zeros((1, SUBLANE, ffp), F32)], axis=0)

    def conv_ffn_rest(x, l, state):
        (hn,) = rmsnorm_rows(x, g_ffn[l:l + 1])
        a = matmul(hn, w_gu_b, layer=l, col_off=0, n_out=d_ff)
        u = matmul(hn, w_gu_b, layer=l, col_off=d_ff, n_out=d_ff)
        act = conv_gate(padc(a), padc(u), state, cw_p[l], cb_p[l], rows_per_stream=dseq, row_tile=dseq)
        return matmul_residual(act, w_dn_p, x, layer=l), a

    x_r2, a_r0 = conv_ffn_rest(x_r1, 0, state_r0)

    (h_m,) = rmsnorm_rows(x_main, g_attn[0:1])
    q_m, k_m, v_m, gate_m = ret_in(h_m, cos_m, sin_m, seq)
    chunk = _divisor_tile(seq, 256, 16)
    og_main, ret_p = retention(log_g, q_m, k_m, v_m, gate_m, ret_meta, nb=nbp, rows=seq, chunk=chunk,
                               row_off=0, heads=h_a)
    x_m1 = matmul_residual(og_main, w_out, x_main)
    state_m0 = jnp.broadcast_to(state_rows(a_r0[None, m_sm - 2:m_sm, :d_ff]), (nbp, SUBLANE, ffp))
    x_m2, conv_p0 = conv_ffn(x_m1, 0, state_m0)

    def kv_side(x, g_row):
        h, hk = rmsnorm_rows(x, jnp.stack([g_row, g_kv]))
        kk = matmul_headnorm(hk, w_kv, g_k, col_off=0, n_out=d, head_dim=hd_b, scale=1.0, out_dtype=F32)
        vv = matmul(hk, w_kv, col_off=d, n_out=d)
        lf = matmul_logsigmoid(hk, w_f, b_fp)
        q = matmul_headnorm(h, w_q, g_q[0], col_off=0, n_out=d, head_dim=hd_b, scale=hd_b ** -0.5 * LOG2E,
                            out_dtype=BF16)
        return q, kk, vv, lf

    q_r, kk_r, vv_r, lf_r = kv_side(x_r2, g_attn[1])
    q_m, kk_m, vv_m, lf_m = kv_side(x_m2, g_attn[1])

    c_cache = cumsum_rows(jnp.pad(cache_logf, ((0, 0), (0, 0), (0, LANE - h_b))), jnp.zeros((1, 1, LANE), F32))
    c_s = cumsum_rows(lf_r[:m_s].reshape(nbs, dseq, LANE), c_cache[:, past - 1:past, :])
    c_meta = cumsum_rows(lf_r[m_s:m_sm].reshape(1, n_meta, LANE), jnp.zeros((1, 1, LANE), F32))
    c_main = cumsum_rows(lf_m.reshape(nbp, seq, LANE), c_meta[:, n_meta - 1:n_meta, :])

    cq_s, ck_s = _head_major(c_s, h_b)
    _, ck_cache = _head_major(c_cache, h_b)
    o_s = fox_attention(q_r, cq_s, kk_r, vv_r, ck_s,
                        (cache_k.reshape(nbs, past, d), cache_v.reshape(nbs, past, d), ck_cache),
                        nb=nbs, rows=dseq, row_off=0, heads=h_b, q_tile=dseq)
    cq_meta, ck_meta = _head_major(c_meta, h_b)
    o_meta = fox_attention(q_r, cq_meta, kk_r, vv_r, ck_meta, None,
                           nb=1, rows=n_meta, row_off=m_s, heads=h_b, q_tile=n_meta)
    cq_main, ck_main = _head_major(c_main, h_b)
    k_meta = kk_r[m_s:m_sm]
    v_meta = vv_r[m_s:m_sm]
    o_main = fox_attention(q_m, cq_main, kk_m, vv_m, ck_main, (k_meta[None], v_meta[None], ck_meta),
                           nb=nbp, rows=seq, row_off=0, heads=h_b, q_tile=_divisor_tile(seq, 512, 16))

    x_r3 = matmul_residual(pad_rows(jnp.concatenate([o_s, o_meta], axis=0)), w_o, x_r2)
    state_r1 = jnp.concatenate([state_rows(state_conv[1]), jnp.zeros((1, SUBLANE, ffp), F32)], axis=0)
    x_r4, a_r1 = conv_ffn_rest(x_r3, 1, state_r1)
    x_m3 = matmul_residual(o_main, w_o, x_m2)
    state_m1 = jnp.broadcast_to(state_rows(a_r1[None, m_sm - 2:m_sm, :d_ff]), (nbp, SUBLANE, ffp))
    x_m4, conv_p1 = conv_ffn(x_m3, 1, state_m1)

    def with_meta(meta_rows, main_rows, tail):
        mr = jnp.broadcast_to(meta_rows.reshape((1, n_meta) + tail), (nbp, n_meta) + tail)
        return jnp.concatenate([mr, main_rows.reshape((nbp, seq) + tail)], axis=1)

    y_prompt = x_m4.reshape(nbp, seq, d)
    y_sample = x_r4[:m_s].reshape(nbs, dseq, d)
    k_prompt = with_meta(k_meta, kk_m, (h_b, hd_b))
    v_prompt = with_meta(v_meta, vv_m, (h_b, hd_b))
    logf_prompt = with_meta(lf_r[m_s:m_sm, :h_b], lf_m[:, :h_b], (h_b,))
    conv_prompt = jnp.stack([conv_p0, conv_p1])
    conv_sample = jnp.stack([a[:m_s].reshape(nbs, dseq, d_ff)[:, dseq - 2:] for a in (a_r0, a_r1)])
    return (y_prompt, y_sample, k_prompt, v_prompt, logf_prompt, ret_p[None], conv_prompt,
            kk_r[:m_s].reshape(nbs, dseq, h_b, hd_b), vv_r[:m_s].reshape(nbs, dseq, h_b, hd_b),
            lf_r[:m_s, :h_b].reshape(nbs, dseq, h_b), ret_s[None], conv_sample)
```

```python
import functools

import jax
import jax.numpy as jnp
from jax import lax
from jax.experimental import pallas as pl
from jax.experimental.pallas import tpu as pltpu

EPS = 1e-6
LOG2E = 1.4426950408889634
ROPE_BASE = 10000.0
MASK_VALUE = -1e30
LANE = 128
SUBLANE = 8
MXU_COLS = 256
VMEM_BUDGET = 44 * 2 ** 20
F32 = jnp.float32
BF16 = jnp.bfloat16


def _params(semantics, vmem_bytes):
    return pltpu.CompilerParams(dimension_semantics=semantics,
                                vmem_limit_bytes=int(min(vmem_bytes + 8 * 2 ** 20, 60 * 2 ** 20)))


def _divisor_tile(n, cap, mult):
    if n <= cap:
        return n
    t = cap - cap % mult
    while t >= mult:
        if n % t == 0:
            return t
        t -= mult
    raise ValueError(f"no tile for {n} (cap {cap}, multiple {mult})")


def _rmsnorm_kernel(x_ref, g_ref, *o_refs):
    x = x_ref[...]
    y = x * lax.rsqrt(jnp.mean(x * x, axis=-1, keepdims=True) + EPS)
    for n, o_ref in enumerate(o_refs):
        o_ref[...] = (y * g_ref[n:n + 1, :]).astype(o_ref.dtype)


def rmsnorm_rows(x, gains):
    m, d = x.shape
    g = gains.shape[0]
    tm = _divisor_tile(m, 256, 16)
    vmem = 2 * tm * d * (4 + 2 * g)
    outs = pl.pallas_call(
        _rmsnorm_kernel,
        grid=(m // tm,),
        in_specs=[pl.BlockSpec((tm, d), lambda i: (i, 0)),
                  pl.BlockSpec((g, d), lambda i: (0, 0))],
        out_specs=[pl.BlockSpec((tm, d), lambda i: (i, 0)) for _ in range(g)],
        out_shape=[jax.ShapeDtypeStruct((m, d), BF16) for _ in range(g)],
        compiler_params=_params(("parallel",), vmem),
        name="rmsnorm",
    )(x, gains)
    return tuple(outs)


def _epi_plain(acc, o_ref):
    o_ref[...] = acc.astype(o_ref.dtype)


def _epi_residual(acc, res_ref, o_ref):
    o_ref[...] = res_ref[...] + acc


def _epi_rope(acc, cos_ref, sin_ref, o_ref, *, head_dim, scale):
    half = head_dim // 2
    cos = cos_ref[...]
    sin = sin_ref[...]
    for h in range(acc.shape[1] // head_dim):
        x1 = acc[:, h * head_dim:h * head_dim + half]
        x2 = acc[:, h * head_dim + half:(h + 1) * head_dim]
        o_ref[:, h * head_dim:h * head_dim + half] = ((x1 * cos - x2 * sin) * scale).astype(o_ref.dtype)
        o_ref[:, h * head_dim + half:(h + 1) * head_dim] = ((x1 * sin + x2 * cos) * scale).astype(o_ref.dtype)


def _epi_headnorm(acc, g_ref, o_ref, *, head_dim, scale):
    g = g_ref[...]
    for h in range(acc.shape[1] // head_dim):
        x = acc[:, h * head_dim:(h + 1) * head_dim]
        y = x * lax.rsqrt(jnp.mean(x * x, axis=-1, keepdims=True) + EPS) * g
        o_ref[:, h * head_dim:(h + 1) * head_dim] = (y * scale).astype(o_ref.dtype)


def _epi_logsigmoid(acc, b_ref, o_ref):
    z = acc + b_ref[...]
    o_ref[...] = jnp.minimum(z, 0.0) - jnp.log1p(jnp.exp(-jnp.abs(z)))


def _mm_kernel(x_ref, w_ref, *rest, epilogue, nk, k_valid):
    tk = w_ref.shape[0]

    def last_w():
        if k_valid is None:
            return w_ref[...]
        row = lax.broadcasted_iota(jnp.int32, (tk, 1), 0)
        return jnp.where(row < k_valid - (nk - 1) * tk, w_ref[...], jnp.zeros((), w_ref.dtype))

    if nk == 1:
        epilogue(jnp.dot(x_ref[...], last_w(), preferred_element_type=F32), *rest)
        return
    acc_ref = rest[-1]
    k = pl.program_id(2)

    @pl.when(k == 0)
    def _():
        acc_ref[...] = jnp.zeros_like(acc_ref)

    @pl.when(k < nk - 1)
    def _():
        acc_ref[...] += jnp.dot(x_ref[...], w_ref[...], preferred_element_type=F32)

    @pl.when(k == nk - 1)
    def _():
        epilogue(acc_ref[...] + jnp.dot(x_ref[...], last_w(), preferred_element_type=F32), *rest[:-1])


def _mm_tiles(m, k, n, out_bytes, extra_bytes_per_elem):
    tm = _divisor_tile(m, 1024, 16)
    for tn in (1024, 512, 256, 128):
        if n % tn:
            continue
        for nk in (1, 2, 4, 8):
            if k % nk or (k // nk) % LANE:
                continue
            tk = k // nk
            ws = 2 * (tm * tk * 2 + tk * tn * 2 + tm * tn * (out_bytes + extra_bytes_per_elem))
            ws += tm * tn * 4
            if ws <= VMEM_BUDGET:
                return tm, tn, tk, ws
    raise ValueError(f"no matmul tiling for {(m, k, n)}")


def matmul(x, w, *, col_off=0, n_out=None, epilogue=_epi_plain, extras=(), extra_specs=lambda tm, tn: [],
           out_dtype=F32, extra_bytes_per_elem=0, row_tile=None, layer=None):
    m, k = x.shape
    n_out = w.shape[-1] - col_off if n_out is None else n_out
    tm, tn, tk, ws = _mm_tiles(m, k, n_out, jnp.dtype(out_dtype).itemsize, extra_bytes_per_elem)
    if row_tile is not None:
        tm = row_tile
    assert col_off % tn == 0 and m % tm == 0
    nk = k // tk
    joff = col_off // tn
    k_rows = w.shape[-2]
    assert k - tk < k_rows <= k
    kern = functools.partial(_mm_kernel, epilogue=epilogue, nk=nk, k_valid=None if k_rows == k else k_rows)
    if layer is None:
        w_spec = pl.BlockSpec((tk, tn), lambda i, j, kk: (kk, j + joff))
    else:
        w_spec = pl.BlockSpec((None, tk, tn), lambda i, j, kk: (layer, kk, j + joff))
    return pl.pallas_call(
        kern,
        grid=(m // tm, n_out // tn, nk),
        in_specs=[pl.BlockSpec((tm, tk), lambda i, j, kk: (i, kk)), w_spec, *extra_specs(tm, tn)],
        out_specs=pl.BlockSpec((tm, tn), lambda i, j, kk: (i, j)),
        out_shape=jax.ShapeDtypeStruct((m, n_out), out_dtype),
        scratch_shapes=[pltpu.VMEM((tm, tn), F32)] if nk > 1 else [],
        compiler_params=_params(("parallel", "parallel", "arbitrary"), ws),
        name="matmul_" + getattr(epilogue, "func", epilogue).__name__.lstrip("_"),
    )(x, w, *extras)


def matmul_residual(x, w, res, layer=None):
    return matmul(x, w, epilogue=_epi_residual, extras=(res,), extra_bytes_per_elem=4, layer=layer,
                  extra_specs=lambda tm, tn: [pl.BlockSpec((tm, tn), lambda i, j, kk: (i, j))])


def matmul_rope(x, w, cos, sin, *, col_off, n_out, head_dim, scale, period_rows):
    half = head_dim // 2

    def specs(tm, tn):
        assert period_rows % tm == 0 and tn % head_dim == 0
        nper = period_rows // tm
        return [pl.BlockSpec((tm, half), lambda i, j, kk: (i % nper, 0))] * 2

    row_tile = _divisor_tile(period_rows, _divisor_tile(x.shape[0], 1024, 16), 16)
    return matmul(x, w, col_off=col_off, n_out=n_out, out_dtype=BF16, extras=(cos, sin), extra_specs=specs,
                  row_tile=row_tile,
                  epilogue=functools.partial(_epi_rope, head_dim=head_dim, scale=scale))


def matmul_headnorm(x, w, gain, *, col_off, n_out, head_dim, scale, out_dtype):
    return matmul(x, w, col_off=col_off, n_out=n_out, out_dtype=out_dtype, extras=(gain.reshape(1, head_dim),),
                  extra_specs=lambda tm, tn: [pl.BlockSpec((1, head_dim), lambda i, j, kk: (0, 0))],
                  epilogue=functools.partial(_epi_headnorm, head_dim=head_dim, scale=scale))


def matmul_logsigmoid(x, w, bias):
    return matmul(x, w, extras=(bias,), epilogue=_epi_logsigmoid,
                  extra_specs=lambda tm, tn: [pl.BlockSpec((1, tn), lambda i, j, kk: (0, j))])


def _retention_kernel(lg_ref, q_ref, k_ref, v_ref, gate_ref, s0_ref, og_ref, sout_ref, s_scr, decay_scr, *,
                      nc, hps):
    hg = pl.program_id(1)
    c = pl.program_id(2)
    t = q_ref.shape[0]
    dk = q_ref.shape[1] // hps
    dv = v_ref.shape[1] // hps
    idx = lax.broadcasted_iota(jnp.int32, (t, 1), 0).astype(F32)

    @pl.when(c == 0)
    def _():
        rel = (lax.broadcasted_iota(jnp.int32, (t, t), 0) - lax.broadcasted_iota(jnp.int32, (t, t), 1)).astype(F32)
        causal = rel >= 0.0
        for hh in range(hps):
            s_scr[hh] = s0_ref[0, hh]
            decay_scr[hh] = jnp.where(causal, jnp.exp(jnp.where(causal, rel, 0.0) * lg_ref[hg * hps + hh]), 0.0)

    for hh in range(hps):
        lg = lg_ref[hg * hps + hh]
        q = q_ref[:, hh * dk:(hh + 1) * dk]
        k = k_ref[:, hh * dk:(hh + 1) * dk]
        v = v_ref[:, hh * dv:(hh + 1) * dv]
        scores = lax.dot_general(q, k, (((1,), (1,)), ((), ())), preferred_element_type=F32) * decay_scr[hh]
        o = jnp.dot(scores.astype(BF16), v, preferred_element_type=F32)
        s_old = s_scr[hh]
        q_dec = (q.astype(F32) * jnp.exp((idx + 1.0) * lg)).astype(BF16)
        o = o + jnp.dot(q_dec, s_old.astype(BF16), preferred_element_type=F32)
        k_dec = (k.astype(F32) * jnp.exp((t - 1.0 - idx) * lg)).astype(BF16)
        kv = lax.dot_general(k_dec, v, (((0,), (0,)), ((), ())), preferred_element_type=F32)
        s_new = jnp.exp(jnp.full((1, 1), t, F32) * lg) * s_old + kv
        s_scr[hh] = s_new
        on = o * lax.rsqrt(jnp.mean(o * o, axis=-1, keepdims=True) + EPS)
        g = gate_ref[:, hh * dv:(hh + 1) * dv].astype(F32)
        og_ref[:, hh * dv:(hh + 1) * dv] = ((g * jax.nn.sigmoid(g)) * on).astype(og_ref.dtype)

        @pl.when(c == nc - 1)
        def _():
            sout_ref[0, hh] = s_new


def retention(log_g, q, k, v, gate, s0, *, nb, rows, chunk, row_off, heads):
    dk = q.shape[1] // heads
    dv = v.shape[1] // heads
    nc = rows // chunk
    hps = 2 if heads % 2 == 0 else 1
    assert rows % chunk == 0 and row_off % chunk == 0
    roff = row_off // chunk
    shared = s0.shape[0] == 1
    row_map = lambda b, h, c: (roff + b * nc + c, h)
    vmem = hps * (2 * chunk * (2 * dk * 2 + 3 * dv * 2) + 5 * dk * dv * 4 + 8 * chunk * chunk * 4)
    og, s_out = pl.pallas_call(
        functools.partial(_retention_kernel, nc=nc, hps=hps),
        grid=(nb, heads // hps, nc),
        in_specs=[pl.BlockSpec(memory_space=pltpu.SMEM),
                  pl.BlockSpec((chunk, hps * dk), row_map),
                  pl.BlockSpec((chunk, hps * dk), row_map),
                  pl.BlockSpec((chunk, hps * dv), row_map),
                  pl.BlockSpec((chunk, hps * dv), row_map),
                  pl.BlockSpec((1, hps, dk, dv), lambda b, h, c: (0 if shared else b, h, 0, 0))],
        out_specs=[pl.BlockSpec((chunk, hps * dv), lambda b, h, c: (b * nc + c, h)),
                   pl.BlockSpec((1, hps, dk, dv), lambda b, h, c: (b, h, 0, 0))],
        out_shape=[jax.ShapeDtypeStruct((nb * rows, heads * dv), BF16),
                   jax.ShapeDtypeStruct((nb, heads, dk, dv), F32)],
        scratch_shapes=[pltpu.VMEM((hps, dk, dv), F32), pltpu.VMEM((hps, chunk, chunk), F32)],
        compiler_params=_params(("parallel", "parallel", "arbitrary"), vmem),
        name="retention",
    )(log_g, q, k, v, gate, s0)
    return og, s_out


def _conv_silu_gate(a, u, halo, cw, cb):
    r1 = pltpu.roll(a, 1, axis=0)
    r2 = pltpu.roll(a, 2, axis=0)
    row = lax.broadcasted_iota(jnp.int32, (SUBLANE, a.shape[1]), 0)
    top1 = jnp.where(row == 0, halo[7:8, :], r1[:SUBLANE])
    top2 = jnp.where(row == 0, halo[6:7, :], jnp.where(row == 1, halo[7:8, :], r2[:SUBLANE]))
    prev1 = jnp.concatenate([top1, r1[SUBLANE:]], axis=0)
    prev2 = jnp.concatenate([top2, r2[SUBLANE:]], axis=0)
    c = cb + cw[0:1, :] * prev2
    c = c + cw[1:2, :] * prev1
    c = c + cw[2:3, :] * a
    return (c * jax.nn.sigmoid(c)) * u


def _conv_gate_kernel(a_ref, u_ref, halo_ref, state_ref, cw_ref, cb_ref, o_ref, *, tiles_per_stream):
    first = (pl.program_id(0) % tiles_per_stream) == 0
    halo = jnp.where(first, state_ref[0], halo_ref[...])
    o_ref[...] = _conv_silu_gate(a_ref[...], u_ref[...], halo, cw_ref[...], cb_ref[...]).astype(o_ref.dtype)


def conv_gate(a, u, state, cw, cb, *, rows_per_stream, row_tile):
    m, f = a.shape
    tr = row_tile
    tc = _divisor_tile(f, 1024, LANE)
    assert rows_per_stream % tr == 0 and tr % SUBLANE == 0
    tps = rows_per_stream // tr
    nfb = f // tc
    hb = tr // SUBLANE
    vmem = 2 * (tr * tc * (4 + 4 + 2) + 3 * 8 * tc * 4) + 6 * tr * tc * 4
    return pl.pallas_call(
        functools.partial(_conv_gate_kernel, tiles_per_stream=tps),
        grid=(m // tr, nfb),
        in_specs=[pl.BlockSpec((tr, tc), lambda i, j: (i, j)),
                  pl.BlockSpec((tr, tc), lambda i, j: (i, j)),
                  pl.BlockSpec((SUBLANE, tc), lambda i, j: (jnp.maximum(i * hb - 1, 0), j)),
                  pl.BlockSpec((1, SUBLANE, tc), lambda i, j: (i // tps, 0, j)),
                  pl.BlockSpec((SUBLANE, tc), lambda i, j: (0, j)),
                  pl.BlockSpec((1, tc), lambda i, j: (0, j))],
        out_specs=pl.BlockSpec((tr, tc), lambda i, j: (i, j)),
        out_shape=jax.ShapeDtypeStruct((m, f), BF16),
        compiler_params=_params(("parallel", "parallel"), vmem),
        name="conv_gate",
    )(a, u, a, state, cw, cb)


def _gate_up_conv_kernel(x_ref, *refs, tiles_per_stream, nsub, blocks):
    wa_refs, wu_refs = refs[:nsub], refs[nsub:2 * nsub]
    state_ref, cw_ref, cb_ref, o_ref, tail_ref, halo_scr = refs[2 * nsub:]
    i = pl.program_id(0)
    j = pl.program_id(1)
    first = (i % tiles_per_stream) == 0
    tm = x_ref.shape[0]
    for s in range(nsub):
        cols = slice(s * MXU_COLS, (s + 1) * MXU_COLS)
        a = jnp.dot(x_ref[...], wa_refs[s][...], preferred_element_type=F32)
        u = jnp.dot(x_ref[...], wu_refs[s][...], preferred_element_type=F32)
        halo = jnp.where(first, state_ref[0, :, cols], halo_scr[j, :, cols])
        act = _conv_silu_gate(a, u, halo, cw_ref[:, cols], cb_ref[:, cols])
        o_ref[:, cols] = jnp.where(j * nsub + s < blocks, act, 0.0).astype(o_ref.dtype)
        tail = a[tm - SUBLANE:, :]
        halo_scr[j, :, cols] = tail
        tail_ref[0, :, cols] = tail


def matmul_conv_gate(x, w, layer, state, cw, cb, *, rows_per_stream):
    m, k = x.shape
    f = w.shape[2] // 2
    fp = cw.shape[1]
    tm = _divisor_tile(rows_per_stream, 1024, 16)
    nsub = 2
    tn = nsub * MXU_COLS
    assert f % MXU_COLS == 0 and fp % tn == 0 and m % rows_per_stream == 0
    blocks = f // MXU_COLS
    tps = rows_per_stream // tm
    nfb = fp // tn
    vmem = 2 * (tm * k * 2 + 2 * k * tn * 2 + tm * tn * 2) + SUBLANE * fp * 4 + 8 * tm * MXU_COLS * 4

    def weight_spec(branch, s):
        return pl.BlockSpec((None, k, MXU_COLS),
                            lambda i, j: (layer, 0, branch * blocks + jnp.minimum(j * nsub + s, blocks - 1)))

    act, tails = pl.pallas_call(
        functools.partial(_gate_up_conv_kernel, tiles_per_stream=tps, nsub=nsub, blocks=blocks),
        grid=(m // tm, nfb),
        in_specs=[pl.BlockSpec((tm, k), lambda i, j: (i, 0)),
                  *[weight_spec(0, s) for s in range(nsub)],
                  *[weight_spec(1, s) for s in range(nsub)],
                  pl.BlockSpec((1, SUBLANE, tn), lambda i, j: (i // tps, 0, j)),
                  pl.BlockSpec((SUBLANE, tn), lambda i, j: (0, j)),
                  pl.BlockSpec((1, tn), lambda i, j: (0, j))],
        out_specs=[pl.BlockSpec((tm, tn), lambda i, j: (i, j)),
                   pl.BlockSpec((1, SUBLANE, tn), lambda i, j: (i, 0, j))],
        out_shape=[jax.ShapeDtypeStruct((m, fp), BF16),
                   jax.ShapeDtypeStruct((m // tm, SUBLANE, fp), F32)],
        scratch_shapes=[pltpu.VMEM((nfb, SUBLANE, tn), F32)],
        compiler_params=_params(("arbitrary", "arbitrary"), vmem),
        name="matmul_gate_up_conv",
    )(x, *([w] * (2 * nsub)), state, cw, cb)
    return act, tails[tps - 1::tps]


def _cumsum_kernel(x_ref, c0_ref, o_ref, *, blk):
    t = x_ref.shape[1]
    r = lax.broadcasted_iota(jnp.int32, (blk, blk), 0)
    s = lax.broadcasted_iota(jnp.int32, (blk, blk), 1)
    tri = (s <= r).astype(BF16)
    carry = c0_ref[0]
    for b in range(t // blk):
        x = x_ref[0, b * blk:(b + 1) * blk, :]
        x_hi = x.astype(BF16)
        r1 = x - x_hi.astype(F32)
        x_mid = r1.astype(BF16)
        x_lo = (r1 - x_mid.astype(F32)).astype(BF16)
        c = jnp.dot(tri, x_hi, preferred_element_type=F32)
        c = c + jnp.dot(tri, x_mid, preferred_element_type=F32)
        c = c + jnp.dot(tri, x_lo, preferred_element_type=F32)
        c = c + carry
        o_ref[0, b * blk:(b + 1) * blk, :] = c
        carry = c[blk - 1:blk, :]


def cumsum_rows(x, c0):
    nb, t, l = x.shape
    blk = _divisor_tile(t, 256, 16)
    shared = c0.shape[0] == 1
    return pl.pallas_call(
        functools.partial(_cumsum_kernel, blk=blk),
        grid=(nb,),
        in_specs=[pl.BlockSpec((1, t, l), lambda b: (b, 0, 0)),
                  pl.BlockSpec((1, 1, l), lambda b: (0 if shared else b, 0, 0))],
        out_specs=pl.BlockSpec((1, t, l), lambda b: (b, 0, 0)),
        out_shape=jax.ShapeDtypeStruct((nb, t, l), F32),
        compiler_params=_params(("parallel",), 4 * t * l * 4 + 2 ** 20),
        name="cumsum",
    )(x, c0)


def _fox_kernel(*refs, tq, prefix):
    m_scr, cq_scr, alpha_scr, acc_scr, sa_scr, sb_scr, pa_scr, pb_scr, kb_scr, v1_scr, m0_scr, acc0_scr = refs[-12:]
    if prefix:
        q_ref, cq_ref, km_ref, vm_ref, ckm_ref, kp_ref, vp_ref, ckp_ref, o_ref = refs[:-12]
    else:
        q_ref, cq_ref, km_ref, vm_ref, ckm_ref, o_ref = refs[:-12]
    hd = q_ref.shape[1]
    nq = q_ref.shape[0] // tq

    def lanes(x, width):
        return jnp.tile(x, (1, width // LANE)) if width % LANE == 0 else x[:, :1]

    def ones_columns(vb):
        return jnp.concatenate([vb.astype(BF16), jnp.ones((vb.shape[0], LANE), BF16)], axis=1)

    for c in range(nq):
        rows = slice(c * tq, (c + 1) * tq)
        kb_scr[rows, :] = km_ref[rows, :].astype(BF16)
        v1_scr[rows, :] = ones_columns(vm_ref[rows, :])

    def block(j):
        return pl.ds(pl.multiple_of(j * tq, tq), tq)

    head_lane = lax.broadcasted_iota(jnp.int32, (tq, LANE), 1) == pl.program_id(1)

    def query_log_forget(rows):
        return jnp.sum(jnp.where(head_lane, cq_ref[0, rows, :], 0.0), axis=-1, keepdims=True)

    if prefix:
        kp = kp_ref[0].astype(BF16)
        vp1 = ones_columns(vp_ref[0])
        for c in range(nq):
            rows = slice(c * tq, (c + 1) * tq)
            s = lax.dot_general(q_ref[rows, :], kp, (((1,), (1,)), ((), ())),
                                preferred_element_type=F32) - ckp_ref[0]
            cq = query_log_forget(rows)
            m0 = jnp.max(s, axis=-1, keepdims=True) + cq
            p = jnp.exp2(s - (m0 - cq)).astype(BF16)
            m0_scr[rows, :] = jnp.broadcast_to(m0, (tq, LANE))
            acc0_scr[rows, :] = jnp.dot(p, vp1, preferred_element_type=F32)

    def query_block(qi, carry):
        q = q_ref[block(qi), :]
        if prefix:
            m_scr[...] = m0_scr[block(qi), :]
            acc_scr[...] = acc0_scr[block(qi), :]
        else:
            m_scr[...] = jnp.full(m_scr.shape, MASK_VALUE, F32)
            acc_scr[...] = jnp.zeros(acc_scr.shape, F32)
        cq_scr[...] = jnp.broadcast_to(query_log_forget(block(qi)), cq_scr.shape)

        def scores(s_ref, kb):
            s_ref[:, :kb.shape[0]] = lax.dot_general(q, kb, (((1,), (1,)), ((), ())), preferred_element_type=F32)

        def softmax(s_ref, p_ref, ck, causal):
            width = ck.shape[1]
            rb = min(tq, max(16, 32768 // max(width, LANE)))
            for r in range(tq // rb):
                rows = slice(r * rb, (r + 1) * rb)
                s = s_ref[rows, :width] - ck
                if causal:
                    row = r * rb + lax.broadcasted_iota(jnp.int32, (rb, width), 0)
                    s = jnp.where(lax.broadcasted_iota(jnp.int32, (rb, width), 1) <= row, s, MASK_VALUE)
                cq = cq_scr[rows, :]
                m_old = m_scr[rows, :]
                m_new = jnp.maximum(m_old, jnp.max(s, axis=-1, keepdims=True) + cq)
                p_ref[rows, :width] = jnp.exp2(s - lanes(m_new - cq, width)).astype(BF16)
                m_scr[rows, :] = m_new
                alpha_scr[rows, :] = jnp.exp2(m_old - m_new)

        def accumulate(p, v1, rescale):
            pv = jnp.dot(p, v1, preferred_element_type=F32)
            if rescale:
                acc_scr[...] = (acc_scr[...] + pv) * jnp.tile(alpha_scr[...], (1, 2))
            else:
                acc_scr[...] += pv

        scores(sa_scr, kb_scr[block(0), :])
        pb_scr[...] = jnp.zeros(pb_scr.shape, BF16)

        def pair(t, carry):
            j = 2 * t
            scores(sb_scr, kb_scr[block(j + 1), :])
            softmax(sa_scr, pa_scr, ckm_ref[0, j], False)
            accumulate(pb_scr[:, :tq], v1_scr[block(jnp.maximum(j - 1, 0)), :], True)
            scores(sa_scr, kb_scr[block(j + 2), :])
            softmax(sb_scr, pb_scr, ckm_ref[0, j + 1], False)
            accumulate(pa_scr[:, :tq], v1_scr[block(j), :], True)
            return carry

        lax.fori_loop(0, qi // 2, pair, 0)
        last = jnp.maximum(qi - 1, 0)

        @pl.when(qi % 2 == 0)
        def _():
            softmax(sa_scr, pa_scr, ckm_ref[0, qi], True)
            accumulate(pb_scr[:, :tq], v1_scr[block(last), :], True)
            accumulate(pa_scr[:, :tq], v1_scr[block(qi), :], False)

        @pl.when(qi % 2 == 1)
        def _():
            scores(sb_scr, kb_scr[block(qi), :])
            softmax(sa_scr, pa_scr, ckm_ref[0, last], False)
            accumulate(pb_scr[:, :tq], v1_scr[block(jnp.maximum(qi - 2, 0)), :], True)
            softmax(sb_scr, pb_scr, ckm_ref[0, qi], True)
            accumulate(pa_scr[:, :tq], v1_scr[block(last), :], True)
            accumulate(pb_scr[:, :tq], v1_scr[block(qi), :], False)

        o_ref[block(qi), :] = (acc_scr[:, :hd] / acc_scr[:, hd:]).astype(o_ref.dtype)
        return carry

    lax.fori_loop(0, nq, query_block, 0)


def fox_attention(q, cq, km, vm, ckm, prefix, *, nb, rows, row_off, heads, q_tile):
    hd = q.shape[1] // heads
    tq = q_tile
    nq = rows // tq
    assert rows % tq == 0 and row_off % rows == 0 and hd == LANE and heads <= LANE
    boff = row_off // rows
    in_specs = [pl.BlockSpec((rows, hd), lambda b, h: (boff + b, h)),
                pl.BlockSpec((1, rows, LANE), lambda b, h: (b, 0, 0)),
                pl.BlockSpec((rows, hd), lambda b, h: (boff + b, h)),
                pl.BlockSpec((rows, hd), lambda b, h: (boff + b, h)),
                pl.BlockSpec((1, nq, 1, tq), lambda b, h: (b * heads + h, 0, 0, 0))]
    args = [q, cq, km, vm, ckm.reshape(nb * heads, nq, 1, tq)]
    vmem = 2 * (2 * rows * hd * 2 + rows * LANE * 4 + 2 * rows * hd * 4 + 8 * rows * 4)
    if prefix is not None:
        kp, vp, ckp = prefix
        p = kp.shape[1]
        shared = kp.shape[0] == 1
        in_specs += [pl.BlockSpec((1, p, hd), lambda b, h: (0 if shared else b, 0, h)),
                     pl.BlockSpec((1, p, hd), lambda b, h: (0 if shared else b, 0, h)),
                     pl.BlockSpec((1, 1, p), lambda b, h: (h if shared else b * heads + h, 0, 0))]
        args += [kp, vp, ckp]
        vmem += 2 * (2 * p * hd * 4 + 8 * p * 4) + 4 * tq * max(p, LANE) * 4
    width = -(-tq // LANE) * LANE
    vmem += tq * width * 12 + 5 * tq * LANE * 4 + rows * hd * 6 + rows * (LANE + 2 * hd) * 4 + 2 ** 21
    return pl.pallas_call(
        functools.partial(_fox_kernel, tq=tq, prefix=prefix is not None),
        grid=(nb, heads),
        in_specs=in_specs,
        out_specs=pl.BlockSpec((rows, hd), lambda b, h: (b, h)),
        out_shape=jax.ShapeDtypeStruct((nb * rows, heads * hd), BF16),
        scratch_shapes=[pltpu.VMEM((tq, LANE), F32), pltpu.VMEM((tq, LANE), F32), pltpu.VMEM((tq, LANE), F32),
                        pltpu.VMEM((tq, 2 * hd), F32),
                        pltpu.VMEM((tq, width), F32), pltpu.VMEM((tq, width), F32),
                        pltpu.VMEM((tq, width), BF16), pltpu.VMEM((tq, width), BF16),
                        pltpu.VMEM((rows, hd), BF16), pltpu.VMEM((rows, 2 * hd), BF16),
                        pltpu.VMEM((rows, LANE), F32), pltpu.VMEM((rows, 2 * hd), F32)],
        compiler_params=_params(("parallel", "parallel"), vmem),
        name="fox_attention",
    )(*args)


def _rope_tables(pos, half):
    inv = ROPE_BASE ** (-jnp.arange(half, dtype=F32) / half)
    ang = pos.astype(F32)[:, None] * inv[None, :]
    return jnp.cos(ang), jnp.sin(ang)


def _head_major(c, heads):
    nb, t, _ = c.shape
    c2 = c * LOG2E
    return c2, jnp.swapaxes(c2[:, :, :heads], 1, 2).reshape(nb * heads, 1, t)


def kernel(x_prompt, x_sample, cache_k, cache_v, cache_logf, state_ret, state_conv, meta, g_attn, g_ffn,
           w_ret_in, w_ret_out, g_kv, w_kvf, b_f, g_k, w_q_b, g_q, w_o_b, w_gu, conv_w, conv_b, w_down):
    nbp, seq, d = x_prompt.shape
    nbs, dseq, _ = x_sample.shape
    n_meta = meta.shape[0]
    past = cache_k.shape[1]
    depth = g_attn.shape[0]
    n_a = w_ret_in.shape[0]
    h_a, dk_a, dv_a = state_ret.shape[2:]
    h_b, hd_b = cache_k.shape[2:]
    qk_w, v_w = h_a * dk_a, h_a * dv_a
    d_ff = w_down.shape[1]
    cw_taps = conv_w.shape[1]
    assert n_a == 1 and depth == 2 and cw_taps == 3, "kernel is written for one retention + one attention layer"
    assert d_ff % MXU_COLS == 0
    ff_tile = 1024 if d_ff >= 4096 else 2 * MXU_COLS
    ffp = -(-d_ff // ff_tile) * ff_tile
    m_main = nbp * seq
    m_s = nbs * dseq
    m_sm = m_s + n_meta
    m_rest = -(-m_sm // dseq) * dseq
    dt = x_prompt.dtype

    w_in = w_ret_in[0].astype(BF16)
    w_out = w_ret_out[0].astype(BF16)
    w_kv = w_kvf.astype(BF16)
    w_f = jnp.pad(w_kvf[:, 2 * d:], ((0, 0), (0, LANE - h_b))).astype(BF16)
    b_fp = jnp.pad(b_f, (0, LANE - h_b)).reshape(1, LANE)
    w_q = w_q_b[0].astype(BF16)
    w_o = w_o_b[0].astype(BF16)
    padc = lambda a: jnp.pad(a, ((0, 0),) * (a.ndim - 1) + ((0, ffp - d_ff),))
    w_gu_b = w_gu.astype(BF16)
    w_dn_p = w_down.astype(BF16)
    cw_p = [jnp.pad(padc(conv_w[l]), ((0, SUBLANE - cw_taps), (0, 0))) for l in range(depth)]
    cb_p = [padc(conv_b[l]).reshape(1, ffp) for l in range(depth)]
    log_g = jnp.log1p(-jnp.exp2(-5.0 - jnp.arange(h_a, dtype=F32)))

    x_main = x_prompt.reshape(m_main, d)
    pad_rows = lambda a: jnp.pad(a, ((0, m_rest - m_sm),) + ((0, 0),) * (a.ndim - 1))
    x_rest = pad_rows(jnp.concatenate([x_sample.reshape(m_s, d), meta.astype(dt)], axis=0))
    cos_m, sin_m = _rope_tables(n_meta + jnp.arange(seq), dk_a // 2)
    pos_rest = pad_rows(jnp.concatenate([jnp.tile(n_meta + past + jnp.arange(dseq), nbs), jnp.arange(n_meta)]))
    cos_r, sin_r = _rope_tables(pos_rest, dk_a // 2)

    def state_rows(rows2):
        return jnp.pad(padc(rows2), ((0, 0), (SUBLANE - 2, 0), (0, 0)))

    def conv_ffn(x, l, state):
        (hn,) = rmsnorm_rows(x, g_ffn[l:l + 1])
        act, tails = matmul_conv_gate(hn, w_gu_b, l, state, cw_p[l], cb_p[l], rows_per_stream=seq)
        return matmul_residual(act, w_dn_p, x, layer=l), tails[:, SUBLANE - 2:, :d_ff]

    def ret_in(h, cos, sin, period):
        q = matmul_rope(h, w_in, cos, sin, col_off=0, n_out=qk_w, head_dim=dk_a, scale=1.0, period_rows=period)
        k = matmul_rope(h, w_in, cos, sin, col_off=qk_w, n_out=qk_w, head_dim=dk_a, scale=dk_a ** -0.5,
                        period_rows=period)
        v = matmul(h, w_in, col_off=2 * qk_w, n_out=v_w, out_dtype=BF16)
        gate = matmul(h, w_in, col_off=2 * qk_w + v_w, n_out=v_w, out_dtype=BF16)
        return q, k, v, gate

    (h_r,) = rmsnorm_rows(x_rest, g_attn[0:1])
    q_r, k_r, v_r, gate_r = ret_in(h_r, cos_r, sin_r, m_rest)
    og_s, ret_s = retention(log_g, q_r, k_r, v_r, gate_r, state_ret[0], nb=nbs, rows=dseq, chunk=dseq,
                            row_off=0, heads=h_a)
    og_m, ret_meta = retention(log_g, q_r, k_r, v_r, gate_r, jnp.zeros((1, h_a, dk_a, dv_a), F32), nb=1,
                               rows=n_meta, chunk=n_meta, row_off=m_s, heads=h_a)
    x_r1 = matmul_residual(pad_rows(jnp.concatenate([og_s, og_m], axis=0)), w_out, x_rest)
    assert n_meta <= dseq and n_meta % SUBLANE == 0
    state_r0 = jnp.concatenate([state_rows(state_conv[0]), jnp.zeros((1, SUBLANE, ffp), F32)], axis=0)

    def conv_ffn_rest(x, l, state):
        (hn,) = rmsnorm_rows(x, g_ffn[l:l + 1])
        a = matmul(hn, w_gu_b, layer=l, col_off=0, n_out=d_ff)
        u = matmul(hn, w_gu_b, layer=l, col_off=d_ff, n_out=d_ff)
        act = conv_gate(padc(a), padc(u), state, cw_p[l], cb_p[l], rows_per_stream=dseq, row_tile=dseq)
        return matmul_residual(act, w_dn_p, x, layer=l), a

    x_r2, a_r0 = conv_ffn_rest(x_r1, 0, state_r0)

    (h_m,) = rmsnorm_rows(x_main, g_attn[0:1])
    q_m, k_m, v_m, gate_m = ret_in(h_m, cos_m, sin_m, seq)
    chunk = _divisor_tile(seq, 256, 16)
    og_main, ret_p = retention(log_g, q_m, k_m, v_m, gate_m, ret_meta, nb=nbp, rows=seq, chunk=chunk,
                               row_off=0, heads=h_a)
    x_m1 = matmul_residual(og_main, w_out, x_main)
    state_m0 = jnp.broadcast_to(state_rows(a_r0[None, m_sm - 2:m_sm, :d_ff]), (nbp, SUBLANE, ffp))
    x_m2, conv_p0 = conv_ffn(x_m1, 0, state_m0)

    def kv_side(x, g_row):
        h, hk = rmsnorm_rows(x, jnp.stack([g_row, g_kv]))
        kk = matmul_headnorm(hk, w_kv, g_k, col_off=0, n_out=d, head_dim=hd_b, scale=1.0, out_dtype=F32)
        vv = matmul(hk, w_kv, col_off=d, n_out=d)
        lf = matmul_logsigmoid(hk, w_f, b_fp)
        q = matmul_headnorm(h, w_q, g_q[0], col_off=0, n_out=d, head_dim=hd_b, scale=hd_b ** -0.5 * LOG2E,
                            out_dtype=BF16)
        return q, kk, vv, lf

    q_r, kk_r, vv_r, lf_r = kv_side(x_r2, g_attn[1])
    q_m, kk_m, vv_m, lf_m = kv_side(x_m2, g_attn[1])

    c_cache = cumsum_rows(jnp.pad(cache_logf, ((0, 0), (0, 0), (0, LANE - h_b))), jnp.zeros((1, 1, LANE), F32))
    c_s = cumsum_rows(lf_r[:m_s].reshape(nbs, dseq, LANE), c_cache[:, past - 1:past, :])
    c_meta = cumsum_rows(lf_r[m_s:m_sm].reshape(1, n_meta, LANE), jnp.zeros((1, 1, LANE), F32))
    c_main = cumsum_rows(lf_m.reshape(nbp, seq, LANE), c_meta[:, n_meta - 1:n_meta, :])

    cq_s, ck_s = _head_major(c_s, h_b)
    _, ck_cache = _head_major(c_cache, h_b)
    o_s = fox_attention(q_r, cq_s, kk_r, vv_r, ck_s,
                        (cache_k.reshape(nbs, past, d), cache_v.reshape(nbs, past, d), ck_cache),
                        nb=nbs, rows=dseq, row_off=0, heads=h_b, q_tile=dseq)
    cq_meta, ck_meta = _head_major(c_meta, h_b)
    o_meta = fox_attention(q_r, cq_meta, kk_r, vv_r, ck_meta, None,
                           nb=1, rows=n_meta, row_off=m_s, heads=h_b, q_tile=n_meta)
    cq_main, ck_main = _head_major(c_main, h_b)
    k_meta = kk_r[m_s:m_sm]
    v_meta = vv_r[m_s:m_sm]
    o_main = fox_attention(q_m, cq_main, kk_m, vv_m, ck_main, (k_meta[None], v_meta[None], ck_meta),
                           nb=nbp, rows=seq, row_off=0, heads=h_b, q_tile=_divisor_tile(seq, 512, 16))

    x_r3 = matmul_residual(pad_rows(jnp.concatenate([o_s, o_meta], axis=0)), w_o, x_r2)
    state_r1 = jnp.concatenate([state_rows(state_conv[1]), jnp.zeros((1, SUBLANE, ffp), F32)], axis=0)
    x_r4, a_r1 = conv_ffn_rest(x_r3, 1, state_r1)
    x_m3 = matmul_residual(o_main, w_o, x_m2)
    state_m1 = jnp.broadcast_to(state_rows(a_r1[None, m_sm - 2:m_sm, :d_ff]), (nbp, SUBLANE, ffp))
    x_m4, conv_p1 = conv_ffn(x_m3, 1, state_m1)

    def with_meta(meta_rows, main_rows, tail):
        mr = jnp.broadcast_to(meta_rows.reshape((1, n_meta) + tail), (nbp, n_meta) + tail)
        return jnp.concatenate([mr, main_rows.reshape((nbp, seq) + tail)], axis=1)

    y_prompt = x_m4.reshape(nbp, seq, d)
    y_sample = x_r4[:m_s].reshape(nbs, dseq, d)
    k_prompt = with_meta(k_meta, kk_m, (h_b, hd_b))
    v_prompt = with_meta(v_meta, vv_m, (h_b, hd_b))
    logf_prompt = with_meta(lf_r[m_s:m_sm, :h_b], lf_m[:, :h_b], (h_b,))
    conv_prompt = jnp.stack([conv_p0, conv_p1])
    conv_sample = jnp.stack([a[:m_s].reshape(nbs, dseq, d_ff)[:, dseq - 2:] for a in (a_r0, a_r1)])
    return (y_prompt, y_sample, k_prompt, v_prompt, logf_prompt, ret_p[None], conv_prompt,
            kk_r[:m_s].reshape(nbs, dseq, h_b, hd_b), vv_r[:m_s].reshape(nbs, dseq, h_b, hd_b),
            lf_r[:m_s, :h_b].reshape(nbs, dseq, h_b), ret_s[None], conv_sample)
```

```python
import functools

import jax
import jax.numpy as jnp
from jax import lax
from jax.experimental import pallas as pl
from jax.experimental.pallas import tpu as pltpu

EPS = 1e-6
LOG2E = 1.4426950408889634
ROPE_BASE = 10000.0
MASK_VALUE = -1e30
LANE = 128
SUBLANE = 8
MXU_COLS = 256
VMEM_BUDGET = 44 * 2 ** 20
F32 = jnp.float32
BF16 = jnp.bfloat16


def _params(semantics, vmem_bytes):
    return pltpu.CompilerParams(dimension_semantics=semantics,
                                vmem_limit_bytes=int(min(vmem_bytes + 8 * 2 ** 20, 60 * 2 ** 20)))


def _divisor_tile(n, cap, mult):
    if n <= cap:
        return n
    t = cap - cap % mult
    while t >= mult:
        if n % t == 0:
            return t
        t -= mult
    raise ValueError(f"no tile for {n} (cap {cap}, multiple {mult})")


def _rmsnorm_kernel(x_ref, g_ref, *o_refs):
    x = x_ref[...]
    y = x * lax.rsqrt(jnp.mean(x * x, axis=-1, keepdims=True) + EPS)
    for n, o_ref in enumerate(o_refs):
        o_ref[...] = (y * g_ref[n:n + 1, :]).astype(o_ref.dtype)


def rmsnorm_rows(x, gains):
    m, d = x.shape
    g = gains.shape[0]
    tm = _divisor_tile(m, 256, 16)
    vmem = 2 * tm * d * (4 + 2 * g)
    outs = pl.pallas_call(
        _rmsnorm_kernel,
        grid=(m // tm,),
        in_specs=[pl.BlockSpec((tm, d), lambda i: (i, 0)),
                  pl.BlockSpec((g, d), lambda i: (0, 0))],
        out_specs=[pl.BlockSpec((tm, d), lambda i: (i, 0)) for _ in range(g)],
        out_shape=[jax.ShapeDtypeStruct((m, d), BF16) for _ in range(g)],
        compiler_params=_params(("parallel",), vmem),
        name="rmsnorm",
    )(x, gains)
    return tuple(outs)


def _epi_plain(acc, o_ref):
    o_ref[...] = acc.astype(o_ref.dtype)


def _epi_residual(acc, res_ref, o_ref):
    o_ref[...] = res_ref[...] + acc


def _epi_rope(acc, cos_ref, sin_ref, o_ref, *, head_dim, scales, split_col):
    half = head_dim // 2
    cos = cos_ref[...]
    sin = sin_ref[...]
    scale = jnp.where(pl.program_id(1) < split_col // acc.shape[1], scales[0], scales[1])
    for h in range(acc.shape[1] // head_dim):
        x1 = acc[:, h * head_dim:h * head_dim + half]
        x2 = acc[:, h * head_dim + half:(h + 1) * head_dim]
        o_ref[:, h * head_dim:h * head_dim + half] = ((x1 * cos - x2 * sin) * scale).astype(o_ref.dtype)
        o_ref[:, h * head_dim + half:(h + 1) * head_dim] = ((x1 * sin + x2 * cos) * scale).astype(o_ref.dtype)


def _epi_headnorm(acc, g_ref, o_ref, *, head_dim, scale):
    g = g_ref[...]
    for h in range(acc.shape[1] // head_dim):
        x = acc[:, h * head_dim:(h + 1) * head_dim]
        y = x * lax.rsqrt(jnp.mean(x * x, axis=-1, keepdims=True) + EPS) * g
        o_ref[:, h * head_dim:(h + 1) * head_dim] = (y * scale).astype(o_ref.dtype)


def _epi_logsigmoid(acc, b_ref, o_ref):
    z = acc + b_ref[...]
    o_ref[...] = jnp.minimum(z, 0.0) - jnp.log1p(jnp.exp(-jnp.abs(z)))


def _mm_kernel(x_ref, w_ref, *rest, epilogue, nk, k_valid):
    tk = w_ref.shape[0]

    def last_w():
        if k_valid is None:
            return w_ref[...]
        row = lax.broadcasted_iota(jnp.int32, (tk, 1), 0)
        return jnp.where(row < k_valid - (nk - 1) * tk, w_ref[...], jnp.zeros((), w_ref.dtype))

    if nk == 1:
        epilogue(jnp.dot(x_ref[...], last_w(), preferred_element_type=F32), *rest)
        return
    acc_ref = rest[-1]
    k = pl.program_id(2)

    @pl.when(k == 0)
    def _():
        acc_ref[...] = jnp.zeros_like(acc_ref)

    @pl.when(k < nk - 1)
    def _():
        acc_ref[...] += jnp.dot(x_ref[...], w_ref[...], preferred_element_type=F32)

    @pl.when(k == nk - 1)
    def _():
        epilogue(acc_ref[...] + jnp.dot(x_ref[...], last_w(), preferred_element_type=F32), *rest[:-1])


def _mm_tiles(m, k, n, out_bytes, extra_bytes_per_elem):
    tm = _divisor_tile(m, 1024, 16)
    for tn in (1024, 512, 256, 128):
        if n % tn:
            continue
        for nk in (1, 2, 4, 8):
            if k % nk or (k // nk) % LANE:
                continue
            tk = k // nk
            ws = 2 * (tm * tk * 2 + tk * tn * 2 + tm * tn * (out_bytes + extra_bytes_per_elem))
            ws += tm * tn * 4
            if ws <= VMEM_BUDGET:
                return tm, tn, tk, ws
    raise ValueError(f"no matmul tiling for {(m, k, n)}")


def matmul(x, w, *, col_off=0, n_out=None, epilogue=_epi_plain, extras=(), extra_specs=lambda tm, tn: [],
           out_dtype=F32, extra_bytes_per_elem=0, row_tile=None, layer=None):
    m, k = x.shape
    n_out = w.shape[-1] - col_off if n_out is None else n_out
    tm, tn, tk, ws = _mm_tiles(m, k, n_out, jnp.dtype(out_dtype).itemsize, extra_bytes_per_elem)
    if row_tile is not None:
        tm = row_tile
    assert col_off % tn == 0 and m % tm == 0
    nk = k // tk
    joff = col_off // tn
    k_rows = w.shape[-2]
    assert k - tk < k_rows <= k
    kern = functools.partial(_mm_kernel, epilogue=epilogue, nk=nk, k_valid=None if k_rows == k else k_rows)
    if layer is None:
        w_spec = pl.BlockSpec((tk, tn), lambda i, j, kk: (kk, j + joff))
    else:
        w_spec = pl.BlockSpec((None, tk, tn), lambda i, j, kk: (layer, kk, j + joff))
    return pl.pallas_call(
        kern,
        grid=(m // tm, n_out // tn, nk),
        in_specs=[pl.BlockSpec((tm, tk), lambda i, j, kk: (i, kk)), w_spec, *extra_specs(tm, tn)],
        out_specs=pl.BlockSpec((tm, tn), lambda i, j, kk: (i, j)),
        out_shape=jax.ShapeDtypeStruct((m, n_out), out_dtype),
        scratch_shapes=[pltpu.VMEM((tm, tn), F32)] if nk > 1 else [],
        compiler_params=_params(("parallel", "parallel", "arbitrary"), ws),
        name="matmul_" + getattr(epilogue, "func", epilogue).__name__.lstrip("_"),
    )(x, w, *extras)


def matmul_residual(x, w, res, layer=None):
    return matmul(x, w, epilogue=_epi_residual, extras=(res,), extra_bytes_per_elem=4, layer=layer,
                  extra_specs=lambda tm, tn: [pl.BlockSpec((tm, tn), lambda i, j, kk: (i, j))])


def matmul_rope(x, w, cos, sin, *, col_off, n_out, head_dim, scales, split_col, period_rows):
    half = head_dim // 2

    def specs(tm, tn):
        assert period_rows % tm == 0 and tn % head_dim == 0 and split_col % tn == 0
        nper = period_rows // tm
        return [pl.BlockSpec((tm, half), lambda i, j, kk: (i % nper, 0))] * 2

    row_tile = _divisor_tile(period_rows, _divisor_tile(x.shape[0], 1024, 16), 16)
    return matmul(x, w, col_off=col_off, n_out=n_out, out_dtype=BF16, extras=(cos, sin), extra_specs=specs,
                  row_tile=row_tile,
                  epilogue=functools.partial(_epi_rope, head_dim=head_dim, scales=scales, split_col=split_col))


def matmul_headnorm(x, w, gain, *, col_off, n_out, head_dim, scale, out_dtype):
    return matmul(x, w, col_off=col_off, n_out=n_out, out_dtype=out_dtype, extras=(gain.reshape(1, head_dim),),
                  extra_specs=lambda tm, tn: [pl.BlockSpec((1, head_dim), lambda i, j, kk: (0, 0))],
                  epilogue=functools.partial(_epi_headnorm, head_dim=head_dim, scale=scale))


def matmul_logsigmoid(x, w, bias):
    return matmul(x, w, extras=(bias,), epilogue=_epi_logsigmoid,
                  extra_specs=lambda tm, tn: [pl.BlockSpec((1, tn), lambda i, j, kk: (0, j))])


def _retention_kernel(lg_ref, q_ref, k_ref, v_ref, gate_ref, s0_ref, og_ref, sout_ref, s_scr, decay_scr, *,
                      nc, hps):
    hg = pl.program_id(1)
    c = pl.program_id(2)
    t = q_ref.shape[0]
    dk = q_ref.shape[1] // hps
    dv = v_ref.shape[1] // hps
    idx = lax.broadcasted_iota(jnp.int32, (t, 1), 0).astype(F32)

    @pl.when(c == 0)
    def _():
        rel = (lax.broadcasted_iota(jnp.int32, (t, t), 0) - lax.broadcasted_iota(jnp.int32, (t, t), 1)).astype(F32)
        causal = rel >= 0.0
        for hh in range(hps):
            s_scr[hh] = s0_ref[0, hh]
            decay_scr[hh] = jnp.where(causal, jnp.exp(jnp.where(causal, rel, 0.0) * lg_ref[hg * hps + hh]), 0.0)

    for hh in range(hps):
        lg = lg_ref[hg * hps + hh]
        q = q_ref[:, hh * dk:(hh + 1) * dk]
        k = k_ref[:, hh * dk:(hh + 1) * dk]
        v = v_ref[:, hh * dv:(hh + 1) * dv]
        scores = lax.dot_general(q, k, (((1,), (1,)), ((), ())), preferred_element_type=F32) * decay_scr[hh]
        o = jnp.dot(scores.astype(BF16), v, preferred_element_type=F32)
        s_old = s_scr[hh]
        q_dec = (q.astype(F32) * jnp.exp((idx + 1.0) * lg)).astype(BF16)
        o = o + jnp.dot(q_dec, s_old.astype(BF16), preferred_element_type=F32)
        k_dec = (k.astype(F32) * jnp.exp((t - 1.0 - idx) * lg)).astype(BF16)
        kv = lax.dot_general(k_dec, v, (((0,), (0,)), ((), ())), preferred_element_type=F32)
        s_new = jnp.exp(jnp.full((1, 1), t, F32) * lg) * s_old + kv
        s_scr[hh] = s_new
        on = o * lax.rsqrt(jnp.mean(o * o, axis=-1, keepdims=True) + EPS)
        g = gate_ref[:, hh * dv:(hh + 1) * dv].astype(F32)
        og_ref[:, hh * dv:(hh + 1) * dv] = ((g * jax.nn.sigmoid(g)) * on).astype(og_ref.dtype)

        @pl.when(c == nc - 1)
        def _():
            sout_ref[0, hh] = s_new


def retention(log_g, qk, vg, s0, *, nb, rows, chunk, row_off, heads):
    dk = qk.shape[1] // (2 * heads)
    dv = vg.shape[1] // (2 * heads)
    nc = rows // chunk
    hps = 2 if heads % 2 == 0 else 1
    hgs = heads // hps
    assert rows % chunk == 0 and row_off % chunk == 0
    roff = row_off // chunk
    shared = s0.shape[0] == 1
    row_map = lambda b, h, c: (roff + b * nc + c, h)
    second = lambda b, h, c: (roff + b * nc + c, hgs + h)
    vmem = hps * (2 * chunk * (2 * dk * 2 + 3 * dv * 2) + 5 * dk * dv * 4 + 8 * chunk * chunk * 4)
    og, s_out = pl.pallas_call(
        functools.partial(_retention_kernel, nc=nc, hps=hps),
        grid=(nb, heads // hps, nc),
        in_specs=[pl.BlockSpec(memory_space=pltpu.SMEM),
                  pl.BlockSpec((chunk, hps * dk), row_map),
                  pl.BlockSpec((chunk, hps * dk), second),
                  pl.BlockSpec((chunk, hps * dv), row_map),
                  pl.BlockSpec((chunk, hps * dv), second),
                  pl.BlockSpec((1, hps, dk, dv), lambda b, h, c: (0 if shared else b, h, 0, 0))],
        out_specs=[pl.BlockSpec((chunk, hps * dv), lambda b, h, c: (b * nc + c, h)),
                   pl.BlockSpec((1, hps, dk, dv), lambda b, h, c: (b, h, 0, 0))],
        out_shape=[jax.ShapeDtypeStruct((nb * rows, heads * dv), BF16),
                   jax.ShapeDtypeStruct((nb, heads, dk, dv), F32)],
        scratch_shapes=[pltpu.VMEM((hps, dk, dv), F32), pltpu.VMEM((hps, chunk, chunk), F32)],
        compiler_params=_params(("parallel", "parallel", "arbitrary"), vmem),
        name="retention",
    )(log_g, qk, qk, vg, vg, s0)
    return og, s_out


def _conv_silu_gate(a, u, halo, cw, cb):
    r1 = pltpu.roll(a, 1, axis=0)
    r2 = pltpu.roll(a, 2, axis=0)
    row = lax.broadcasted_iota(jnp.int32, (SUBLANE, a.shape[1]), 0)
    top1 = jnp.where(row == 0, halo[7:8, :], r1[:SUBLANE])
    top2 = jnp.where(row == 0, halo[6:7, :], jnp.where(row == 1, halo[7:8, :], r2[:SUBLANE]))
    prev1 = jnp.concatenate([top1, r1[SUBLANE:]], axis=0)
    prev2 = jnp.concatenate([top2, r2[SUBLANE:]], axis=0)
    c = cb + cw[0:1, :] * prev2
    c = c + cw[1:2, :] * prev1
    c = c + cw[2:3, :] * a
    return (c * jax.nn.sigmoid(c)) * u


def _conv_gate_kernel(a_ref, u_ref, halo_ref, state_ref, cw_ref, cb_ref, o_ref, *, tiles_per_stream):
    first = (pl.program_id(0) % tiles_per_stream) == 0
    halo = jnp.where(first, state_ref[0], halo_ref[...])
    o_ref[...] = _conv_silu_gate(a_ref[...], u_ref[...], halo, cw_ref[...], cb_ref[...]).astype(o_ref.dtype)


def conv_gate(a, u, state, cw, cb, *, rows_per_stream, row_tile):
    m, f = a.shape
    tr = row_tile
    tc = _divisor_tile(f, 1024, LANE)
    assert rows_per_stream % tr == 0 and tr % SUBLANE == 0
    tps = rows_per_stream // tr
    nfb = f // tc
    hb = tr // SUBLANE
    vmem = 2 * (tr * tc * (4 + 4 + 2) + 3 * 8 * tc * 4) + 6 * tr * tc * 4
    return pl.pallas_call(
        functools.partial(_conv_gate_kernel, tiles_per_stream=tps),
        grid=(m // tr, nfb),
        in_specs=[pl.BlockSpec((tr, tc), lambda i, j: (i, j)),
                  pl.BlockSpec((tr, tc), lambda i, j: (i, j)),
                  pl.BlockSpec((SUBLANE, tc), lambda i, j: (jnp.maximum(i * hb - 1, 0), j)),
                  pl.BlockSpec((1, SUBLANE, tc), lambda i, j: (i // tps, 0, j)),
                  pl.BlockSpec((SUBLANE, tc), lambda i, j: (0, j)),
                  pl.BlockSpec((1, tc), lambda i, j: (0, j))],
        out_specs=pl.BlockSpec((tr, tc), lambda i, j: (i, j)),
        out_shape=jax.ShapeDtypeStruct((m, f), BF16),
        compiler_params=_params(("parallel", "parallel"), vmem),
        name="conv_gate",
    )(a, u, a, state, cw, cb)


def _gate_up_conv_kernel(x_ref, *refs, tiles_per_stream, nsub, blocks):
    wa_refs, wu_refs = refs[:nsub], refs[nsub:2 * nsub]
    state_ref, cw_ref, cb_ref, o_ref, tail_ref, halo_scr = refs[2 * nsub:]
    i = pl.program_id(0)
    j = pl.program_id(1)
    first = (i % tiles_per_stream) == 0
    tm = x_ref.shape[0]
    for s in range(nsub):
        cols = slice(s * MXU_COLS, (s + 1) * MXU_COLS)
        a = jnp.dot(x_ref[...], wa_refs[s][...], preferred_element_type=F32)
        u = jnp.dot(x_ref[...], wu_refs[s][...], preferred_element_type=F32)
        halo = jnp.where(first, state_ref[0, :, cols], halo_scr[j, :, cols])
        act = _conv_silu_gate(a, u, halo, cw_ref[:, cols], cb_ref[:, cols])
        o_ref[:, cols] = jnp.where(j * nsub + s < blocks, act, 0.0).astype(o_ref.dtype)
        tail = a[tm - SUBLANE:, :]
        halo_scr[j, :, cols] = tail
        tail_ref[0, :, cols] = tail


def matmul_conv_gate(x, w, layer, state, cw, cb, *, rows_per_stream):
    m, k = x.shape
    f = w.shape[2] // 2
    fp = cw.shape[1]
    tm = _divisor_tile(rows_per_stream, 1024, 16)
    nsub = 2
    tn = nsub * MXU_COLS
    assert f % MXU_COLS == 0 and fp % tn == 0 and m % rows_per_stream == 0
    blocks = f // MXU_COLS
    tps = rows_per_stream // tm
    nfb = fp // tn
    vmem = 2 * (tm * k * 2 + 2 * k * tn * 2 + tm * tn * 2) + SUBLANE * fp * 4 + 8 * tm * MXU_COLS * 4

    def weight_spec(branch, s):
        return pl.BlockSpec((None, k, MXU_COLS),
                            lambda i, j: (layer, 0, branch * blocks + jnp.minimum(j * nsub + s, blocks - 1)))

    act, tails = pl.pallas_call(
        functools.partial(_gate_up_conv_kernel, tiles_per_stream=tps, nsub=nsub, blocks=blocks),
        grid=(m // tm, nfb),
        in_specs=[pl.BlockSpec((tm, k), lambda i, j: (i, 0)),
                  *[weight_spec(0, s) for s in range(nsub)],
                  *[weight_spec(1, s) for s in range(nsub)],
                  pl.BlockSpec((1, SUBLANE, tn), lambda i, j: (i // tps, 0, j)),
                  pl.BlockSpec((SUBLANE, tn), lambda i, j: (0, j)),
                  pl.BlockSpec((1, tn), lambda i, j: (0, j))],
        out_specs=[pl.BlockSpec((tm, tn), lambda i, j: (i, j)),
                   pl.BlockSpec((1, SUBLANE, tn), lambda i, j: (i, 0, j))],
        out_shape=[jax.ShapeDtypeStruct((m, fp), BF16),
                   jax.ShapeDtypeStruct((m // tm, SUBLANE, fp), F32)],
        scratch_shapes=[pltpu.VMEM((nfb, SUBLANE, tn), F32)],
        compiler_params=_params(("arbitrary", "arbitrary"), vmem),
        name="matmul_gate_up_conv",
    )(x, *([w] * (2 * nsub)), state, cw, cb)
    return act, tails[tps - 1::tps]


def _cumsum_kernel(x_ref, c0_ref, o_ref, *, blk):
    t = x_ref.shape[1]
    r = lax.broadcasted_iota(jnp.int32, (blk, blk), 0)
    s = lax.broadcasted_iota(jnp.int32, (blk, blk), 1)
    tri = (s <= r).astype(BF16)
    carry = c0_ref[0]
    for b in range(t // blk):
        x = x_ref[0, b * blk:(b + 1) * blk, :]
        x_hi = x.astype(BF16)
        r1 = x - x_hi.astype(F32)
        x_mid = r1.astype(BF16)
        x_lo = (r1 - x_mid.astype(F32)).astype(BF16)
        c = jnp.dot(tri, x_hi, preferred_element_type=F32)
        c = c + jnp.dot(tri, x_mid, preferred_element_type=F32)
        c = c + jnp.dot(tri, x_lo, preferred_element_type=F32)
        c = c + carry
        o_ref[0, b * blk:(b + 1) * blk, :] = c
        carry = c[blk - 1:blk, :]


def cumsum_rows(x, c0):
    nb, t, l = x.shape
    blk = _divisor_tile(t, 256, 16)
    shared = c0.shape[0] == 1
    return pl.pallas_call(
        functools.partial(_cumsum_kernel, blk=blk),
        grid=(nb,),
        in_specs=[pl.BlockSpec((1, t, l), lambda b: (b, 0, 0)),
                  pl.BlockSpec((1, 1, l), lambda b: (0 if shared else b, 0, 0))],
        out_specs=pl.BlockSpec((1, t, l), lambda b: (b, 0, 0)),
        out_shape=jax.ShapeDtypeStruct((nb, t, l), F32),
        compiler_params=_params(("parallel",), 4 * t * l * 4 + 2 ** 20),
        name="cumsum",
    )(x, c0)


def _fox_kernel(*refs, tq, prefix):
    m_scr, cq_scr, alpha_scr, acc_scr, sa_scr, sb_scr, pa_scr, pb_scr, kb_scr, v1_scr, m0_scr, acc0_scr = refs[-12:]
    if prefix:
        q_ref, cq_ref, km_ref, vm_ref, ckm_ref, kp_ref, vp_ref, ckp_ref, o_ref = refs[:-12]
    else:
        q_ref, cq_ref, km_ref, vm_ref, ckm_ref, o_ref = refs[:-12]
    hd = q_ref.shape[1]
    nq = q_ref.shape[0] // tq

    def lanes(x, width):
        return jnp.tile(x, (1, width // LANE)) if width % LANE == 0 else x[:, :1]

    def ones_columns(vb):
        return jnp.concatenate([vb.astype(BF16), jnp.ones((vb.shape[0], LANE), BF16)], axis=1)

    for c in range(nq):
        rows = slice(c * tq, (c + 1) * tq)
        kb_scr[rows, :] = km_ref[rows, :].astype(BF16)
        v1_scr[rows, :] = ones_columns(vm_ref[rows, :])

    def block(j):
        return pl.ds(pl.multiple_of(j * tq, tq), tq)

    head_lane = lax.broadcasted_iota(jnp.int32, (tq, LANE), 1) == pl.program_id(1)

    def query_log_forget(rows):
        return jnp.sum(jnp.where(head_lane, cq_ref[0, rows, :], 0.0), axis=-1, keepdims=True)

    if prefix:
        kp = kp_ref[0].astype(BF16)
        vp1 = ones_columns(vp_ref[0])
        for c in range(nq):
            rows = slice(c * tq, (c + 1) * tq)
            s = lax.dot_general(q_ref[rows, :], kp, (((1,), (1,)), ((), ())),
                                preferred_element_type=F32) - ckp_ref[0]
            cq = query_log_forget(rows)
            m0 = jnp.max(s, axis=-1, keepdims=True) + cq
            p = jnp.exp2(s - (m0 - cq)).astype(BF16)
            m0_scr[rows, :] = jnp.broadcast_to(m0, (tq, LANE))
            acc0_scr[rows, :] = jnp.dot(p, vp1, preferred_element_type=F32)

    def query_block(qi, carry):
        q = q_ref[block(qi), :]
        if prefix:
            m_scr[...] = m0_scr[block(qi), :]
            acc_scr[...] = acc0_scr[block(qi), :]
        else:
            m_scr[...] = jnp.full(m_scr.shape, MASK_VALUE, F32)
            acc_scr[...] = jnp.zeros(acc_scr.shape, F32)
        cq_scr[...] = jnp.broadcast_to(query_log_forget(block(qi)), cq_scr.shape)

        def scores(s_ref, kb):
            s_ref[:, :kb.shape[0]] = lax.dot_general(q, kb, (((1,), (1,)), ((), ())), preferred_element_type=F32)

        def softmax(s_ref, p_ref, ck, causal):
            width = ck.shape[1]
            rb = min(tq, max(16, 32768 // max(width, LANE)))
            for r in range(tq // rb):
                rows = slice(r * rb, (r + 1) * rb)
                s = s_ref[rows, :width] - ck
                if causal:
                    row = r * rb + lax.broadcasted_iota(jnp.int32, (rb, width), 0)
                    s = jnp.where(lax.broadcasted_iota(jnp.int32, (rb, width), 1) <= row, s, MASK_VALUE)
                cq = cq_scr[rows, :]
                m_old = m_scr[rows, :]
                m_new = jnp.maximum(m_old, jnp.max(s, axis=-1, keepdims=True) + cq)
                p_ref[rows, :width] = jnp.exp2(s - lanes(m_new - cq, width)).astype(BF16)
                m_scr[rows, :] = m_new
                alpha_scr[rows, :] = jnp.exp2(m_old - m_new)

        def accumulate(p, v1, rescale):
            pv = jnp.dot(p, v1, preferred_element_type=F32)
            if rescale:
                acc_scr[...] = (acc_scr[...] + pv) * jnp.tile(alpha_scr[...], (1, 2))
            else:
                acc_scr[...] += pv

        scores(sa_scr, kb_scr[block(0), :])
        pb_scr[...] = jnp.zeros(pb_scr.shape, BF16)

        def pair(t, carry):
            j = 2 * t
            scores(sb_scr, kb_scr[block(j + 1), :])
            softmax(sa_scr, pa_scr, ckm_ref[0, j], False)
            accumulate(pb_scr[:, :tq], v1_scr[block(jnp.maximum(j - 1, 0)), :], True)
            scores(sa_scr, kb_scr[block(j + 2), :])
            softmax(sb_scr, pb_scr, ckm_ref[0, j + 1], False)
            accumulate(pa_scr[:, :tq], v1_scr[block(j), :], True)
            return carry

        lax.fori_loop(0, qi // 2, pair, 0)
        last = jnp.maximum(qi - 1, 0)

        @pl.when(qi % 2 == 0)
        def _():
            softmax(sa_scr, pa_scr, ckm_ref[0, qi], True)
            accumulate(pb_scr[:, :tq], v1_scr[block(last), :], True)
            accumulate(pa_scr[:, :tq], v1_scr[block(qi), :], False)

        @pl.when(qi % 2 == 1)
        def _():
            scores(sb_scr, kb_scr[block(qi), :])
            softmax(sa_scr, pa_scr, ckm_ref[0, last], False)
            accumulate(pb_scr[:, :tq], v1_scr[block(jnp.maximum(qi - 2, 0)), :], True)
            softmax(sb_scr, pb_scr, ckm_ref[0, qi], True)
            accumulate(pa_scr[:, :tq], v1_scr[block(last), :], True)
            accumulate(pb_scr[:, :tq], v1_scr[block(qi), :], False)

        o_ref[block(qi), :] = (acc_scr[:, :hd] / acc_scr[:, hd:]).astype(o_ref.dtype)
        return carry

    lax.fori_loop(0, nq, query_block, 0)


def fox_attention(q, cq, km, vm, ckm, prefix, *, nb, rows, row_off, heads, q_tile):
    hd = q.shape[1] // heads
    tq = q_tile
    nq = rows // tq
    assert rows % tq == 0 and row_off % rows == 0 and hd == LANE and heads <= LANE
    boff = row_off // rows
    in_specs = [pl.BlockSpec((rows, hd), lambda b, h: (boff + b, h)),
                pl.BlockSpec((1, rows, LANE), lambda b, h: (b, 0, 0)),
                pl.BlockSpec((rows, hd), lambda b, h: (boff + b, h)),
                pl.BlockSpec((rows, hd), lambda b, h: (boff + b, h)),
                pl.BlockSpec((1, nq, 1, tq), lambda b, h: (b * heads + h, 0, 0, 0))]
    args = [q, cq, km, vm, ckm.reshape(nb * heads, nq, 1, tq)]
    vmem = 2 * (2 * rows * hd * 2 + rows * LANE * 4 + 2 * rows * hd * 4 + 8 * rows * 4)
    if prefix is not None:
        kp, vp, ckp = prefix
        p = kp.shape[1]
        shared = kp.shape[0] == 1
        in_specs += [pl.BlockSpec((1, p, hd), lambda b, h: (0 if shared else b, 0, h)),
                     pl.BlockSpec((1, p, hd), lambda b, h: (0 if shared else b, 0, h)),
                     pl.BlockSpec((1, 1, p), lambda b, h: (h if shared else b * heads + h, 0, 0))]
        args += [kp, vp, ckp]
        vmem += 2 * (2 * p * hd * 4 + 8 * p * 4) + 4 * tq * max(p, LANE) * 4
    width = -(-tq // LANE) * LANE
    vmem += tq * width * 12 + 5 * tq * LANE * 4 + rows * hd * 6 + rows * (LANE + 2 * hd) * 4 + 2 ** 21
    return pl.pallas_call(
        functools.partial(_fox_kernel, tq=tq, prefix=prefix is not None),
        grid=(nb, heads),
        in_specs=in_specs,
        out_specs=pl.BlockSpec((rows, hd), lambda b, h: (b, h)),
        out_shape=jax.ShapeDtypeStruct((nb * rows, heads * hd), BF16),
        scratch_shapes=[pltpu.VMEM((tq, LANE), F32), pltpu.VMEM((tq, LANE), F32), pltpu.VMEM((tq, LANE), F32),
                        pltpu.VMEM((tq, 2 * hd), F32),
                        pltpu.VMEM((tq, width), F32), pltpu.VMEM((tq, width), F32),
                        pltpu.VMEM((tq, width), BF16), pltpu.VMEM((tq, width), BF16),
                        pltpu.VMEM((rows, hd), BF16), pltpu.VMEM((rows, 2 * hd), BF16),
                        pltpu.VMEM((rows, LANE), F32), pltpu.VMEM((rows, 2 * hd), F32)],
        compiler_params=_params(("parallel", "parallel"), vmem),
        name="fox_attention",
    )(*args)


def _rope_tables(pos, half):
    inv = ROPE_BASE ** (-jnp.arange(half, dtype=F32) / half)
    ang = pos.astype(F32)[:, None] * inv[None, :]
    return jnp.cos(ang), jnp.sin(ang)


def _head_major(c, heads):
    nb, t, _ = c.shape
    c2 = c * LOG2E
    return c2, jnp.swapaxes(c2[:, :, :heads], 1, 2).reshape(nb * heads, 1, t)


def kernel(x_prompt, x_sample, cache_k, cache_v, cache_logf, state_ret, state_conv, meta, g_attn, g_ffn,
           w_ret_in, w_ret_out, g_kv, w_kvf, b_f, g_k, w_q_b, g_q, w_o_b, w_gu, conv_w, conv_b, w_down):
    nbp, seq, d = x_prompt.shape
    nbs, dseq, _ = x_sample.shape
    n_meta = meta.shape[0]
    past = cache_k.shape[1]
    depth = g_attn.shape[0]
    n_a = w_ret_in.shape[0]
    h_a, dk_a, dv_a = state_ret.shape[2:]
    h_b, hd_b = cache_k.shape[2:]
    qk_w, v_w = h_a * dk_a, h_a * dv_a
    d_ff = w_down.shape[1]
    cw_taps = conv_w.shape[1]
    assert n_a == 1 and depth == 2 and cw_taps == 3, "kernel is written for one retention + one attention layer"
    assert d_ff % MXU_COLS == 0
    ff_tile = 1024 if d_ff >= 4096 else 2 * MXU_COLS
    ffp = -(-d_ff // ff_tile) * ff_tile
    m_main = nbp * seq
    m_s = nbs * dseq
    m_sm = m_s + n_meta
    m_rest = -(-m_sm // dseq) * dseq
    dt = x_prompt.dtype

    w_in = w_ret_in[0].astype(BF16)
    w_out = w_ret_out[0].astype(BF16)
    w_kv = w_kvf.astype(BF16)
    w_f = jnp.pad(w_kvf[:, 2 * d:], ((0, 0), (0, LANE - h_b))).astype(BF16)
    b_fp = jnp.pad(b_f, (0, LANE - h_b)).reshape(1, LANE)
    w_q = w_q_b[0].astype(BF16)
    w_o = w_o_b[0].astype(BF16)
    padc = lambda a: jnp.pad(a, ((0, 0),) * (a.ndim - 1) + ((0, ffp - d_ff),))
    w_gu_b = w_gu.astype(BF16)
    w_dn_p = w_down.astype(BF16)
    cw_p = [jnp.pad(padc(conv_w[l]), ((0, SUBLANE - cw_taps), (0, 0))) for l in range(depth)]
    cb_p = [padc(conv_b[l]).reshape(1, ffp) for l in range(depth)]
    log_g = jnp.log1p(-jnp.exp2(-5.0 - jnp.arange(h_a, dtype=F32)))

    x_main = x_prompt.reshape(m_main, d)
    pad_rows = lambda a: jnp.pad(a, ((0, m_rest - m_sm),) + ((0, 0),) * (a.ndim - 1))
    x_rest = pad_rows(jnp.concatenate([x_sample.reshape(m_s, d), meta.astype(dt)], axis=0))
    cos_m, sin_m = _rope_tables(n_meta + jnp.arange(seq), dk_a // 2)
    pos_rest = pad_rows(jnp.concatenate([jnp.tile(n_meta + past + jnp.arange(dseq), nbs), jnp.arange(n_meta)]))
    cos_r, sin_r = _rope_tables(pos_rest, dk_a // 2)

    def state_rows(rows2):
        return jnp.pad(padc(rows2), ((0, 0), (SUBLANE - 2, 0), (0, 0)))

    def conv_ffn(x, l, state):
        (hn,) = rmsnorm_rows(x, g_ffn[l:l + 1])
        act, tails = matmul_conv_gate(hn, w_gu_b, l, state, cw_p[l], cb_p[l], rows_per_stream=seq)
        return matmul_residual(act, w_dn_p, x, layer=l), tails[:, SUBLANE - 2:, :d_ff]

    def ret_in(h, cos, sin, period):
        qk = matmul_rope(h, w_in, cos, sin, col_off=0, n_out=2 * qk_w, head_dim=dk_a, scales=(1.0, dk_a ** -0.5),
                         split_col=qk_w, period_rows=period)
        vg = matmul(h, w_in, col_off=2 * qk_w, n_out=2 * v_w, out_dtype=BF16)
        return qk, vg

    (h_r,) = rmsnorm_rows(x_rest, g_attn[0:1])
    qk_r, vg_r = ret_in(h_r, cos_r, sin_r, m_rest)
    og_s, ret_s = retention(log_g, qk_r, vg_r, state_ret[0], nb=nbs, rows=dseq, chunk=dseq,
                            row_off=0, heads=h_a)
    og_m, ret_meta = retention(log_g, qk_r, vg_r, jnp.zeros((1, h_a, dk_a, dv_a), F32), nb=1,
                               rows=n_meta, chunk=n_meta, row_off=m_s, heads=h_a)
    x_r1 = matmul_residual(pad_rows(jnp.concatenate([og_s, og_m], axis=0)), w_out, x_rest)
    assert n_meta <= dseq and n_meta % SUBLANE == 0
    state_r0 = jnp.concatenate([state_rows(state_conv[0]), jnp.zeros((1, SUBLANE, ffp), F32)], axis=0)

    def conv_ffn_rest(x, l, state):
        (hn,) = rmsnorm_rows(x, g_ffn[l:l + 1])
        a = matmul(hn, w_gu_b, layer=l, col_off=0, n_out=d_ff)
        u = matmul(hn, w_gu_b, layer=l, col_off=d_ff, n_out=d_ff)
        act = conv_gate(padc(a), padc(u), state, cw_p[l], cb_p[l], rows_per_stream=dseq, row_tile=dseq)
        return matmul_residual(act, w_dn_p, x, layer=l), a

    x_r2, a_r0 = conv_ffn_rest(x_r1, 0, state_r0)

    (h_m,) = rmsnorm_rows(x_main, g_attn[0:1])
    qk_m, vg_m = ret_in(h_m, cos_m, sin_m, seq)
    chunk = _divisor_tile(seq, 256, 16)
    og_main, ret_p = retention(log_g, qk_m, vg_m, ret_meta, nb=nbp, rows=seq, chunk=chunk,
                               row_off=0, heads=h_a)
    x_m1 = matmul_residual(og_main, w_out, x_main)
    state_m0 = jnp.broadcast_to(state_rows(a_r0[None, m_sm - 2:m_sm, :d_ff]), (nbp, SUBLANE, ffp))
    x_m2, conv_p0 = conv_ffn(x_m1, 0, state_m0)

    def kv_side(x, g_row):
        h, hk = rmsnorm_rows(x, jnp.stack([g_row, g_kv]))
        kk = matmul_headnorm(hk, w_kv, g_k, col_off=0, n_out=d, head_dim=hd_b, scale=1.0, out_dtype=F32)
        vv = matmul(hk, w_kv, col_off=d, n_out=d)
        lf = matmul_logsigmoid(hk, w_f, b_fp)
        q = matmul_headnorm(h, w_q, g_q[0], col_off=0, n_out=d, head_dim=hd_b, scale=hd_b ** -0.5 * LOG2E,
                            out_dtype=BF16)
        return q, kk, vv, lf

    q_r, kk_r, vv_r, lf_r = kv_side(x_r2, g_attn[1])
    q_m, kk_m, vv_m, lf_m = kv_side(x_m2, g_attn[1])

    c_cache = cumsum_rows(jnp.pad(cache_logf, ((0, 0), (0, 0), (0, LANE - h_b))), jnp.zeros((1, 1, LANE), F32))
    c_s = cumsum_rows(lf_r[:m_s].reshape(nbs, dseq, LANE), c_cache[:, past - 1:past, :])
    c_meta = cumsum_rows(lf_r[m_s:m_sm].reshape(1, n_meta, LANE), jnp.zeros((1, 1, LANE), F32))
    c_main = cumsum_rows(lf_m.reshape(nbp, seq, LANE), c_meta[:, n_meta - 1:n_meta, :])

    cq_s, ck_s = _head_major(c_s, h_b)
    _, ck_cache = _head_major(c_cache, h_b)
    o_s = fox_attention(q_r, cq_s, kk_r, vv_r, ck_s,
                        (cache_k.reshape(nbs, past, d), cache_v.reshape(nbs, past, d), ck_cache),
                        nb=nbs, rows=dseq, row_off=0, heads=h_b, q_tile=dseq)
    cq_meta, ck_meta = _head_major(c_meta, h_b)
    o_meta = fox_attention(q_r, cq_meta, kk_r, vv_r, ck_meta, None,
                           nb=1, rows=n_meta, row_off=m_s, heads=h_b, q_tile=n_meta)
    cq_main, ck_main = _head_major(c_main, h_b)
    k_meta = kk_r[m_s:m_sm]
    v_meta = vv_r[m_s:m_sm]
    o_main = fox_attention(q_m, cq_main, kk_m, vv_m, ck_main, (k_meta[None], v_meta[None], ck_meta),
                           nb=nbp, rows=seq, row_off=0, heads=h_b, q_tile=_divisor_tile(seq, 512, 16))

    x_r3 = matmul_residual(pad_rows(jnp.concatenate([o_s, o_meta], axis=0)), w_o, x_r2)
    state_r1 = jnp.concatenate([state_rows(state_conv[1]), jnp.zeros((1, SUBLANE, ffp), F32)], axis=0)
    x_r4, a_r1 = conv_ffn_rest(x_r3, 1, state_r1)
    x_m3 = matmul_residual(o_main, w_o, x_m2)
    state_m1 = jnp.broadcast_to(state_rows(a_r1[None, m_sm - 2:m_sm, :d_ff]), (nbp, SUBLANE, ffp))
    x_m4, conv_p1 = conv_ffn(x_m3, 1, state_m1)

    def with_meta(meta_rows, main_rows, tail):
        mr = jnp.broadcast_to(meta_rows.reshape((1, n_meta) + tail), (nbp, n_meta) + tail)
        return jnp.concatenate([mr, main_rows.reshape((nbp, seq) + tail)], axis=1)

    y_prompt = x_m4.reshape(nbp, seq, d)
    y_sample = x_r4[:m_s].reshape(nbs, dseq, d)
    k_prompt = with_meta(k_meta, kk_m, (h_b, hd_b))
    v_prompt = with_meta(v_meta, vv_m, (h_b, hd_b))
    logf_prompt = with_meta(lf_r[m_s:m_sm, :h_b], lf_m[:, :h_b], (h_b,))
    conv_prompt = jnp.stack([conv_p0, conv_p1])
    conv_sample = jnp.stack([a[:m_s].reshape(nbs, dseq, d_ff)[:, dseq - 2:] for a in (a_r0, a_r1)])
    return (y_prompt, y_sample, k_prompt, v_prompt, logf_prompt, ret_p[None], conv_prompt,
            kk_r[:m_s].reshape(nbs, dseq, h_b, hd_b), vv_r[:m_s].reshape(nbs, dseq, h_b, hd_b),
            lf_r[:m_s, :h_b].reshape(nbs, dseq, h_b), ret_s[None], conv_sample)
```
